```python
import math
import jax
import jax.numpy as jnp
from jax import lax
import numpy as np

D_MODEL = 1024
BATCH = 8
SEQ = 2048
DEPTH = 2
DEC_BATCH = 128
DEC_SEQ = 8
PAST_LEN = 8192
PAGE_SIZE = 128

N_EVEN = (DEPTH + 1) // 2
N_ODD = DEPTH // 2
Q_BLOCK = 128
EPS = 1e-6

H_A = 8
DN_A = 64
DR_A = 32
DV_A = 64
D_CQ = 384
D_C = 256
ROPE_BASE = 10000.0
MLA_SCALE = (DN_A + DR_A) ** -0.5

H_B = 8
HS_B = 64
D_WL = 64
D_AL = 64
D_GL = 128
RW_WIDTH = 3 * H_B * HS_B + D_WL + D_AL + D_GL
RWKV_DECAY_SCALE = 0.606531
RWKV_LN_EPS = 64e-5

H_C = 8
HD_C = 64

H_D = 8
P_D = 64
N_D = 128
G_D = 2
CONV_W = 4
D_INNER = H_D * P_D
CONV_DIM = D_INNER + 2 * G_D * N_D
SSD_CHUNK = 128

MIX_EVEN = H_A * DV_A + H_B * HS_B
MIX_ODD = H_C * HD_C + D_INNER
IN_EVEN = D_CQ + D_C + DR_A + RW_WIDTH
IN_ODD = 3 * H_C * HD_C + D_INNER + CONV_DIM + H_D

N_MEM = 256
MEM_HEADS = 4
MEM_HD = 64

PEER_HEADS = 8
N_KEYS = 128
N_EXPERTS = N_KEYS * N_KEYS
PEER_DQ = 256
PEER_TOPK = 16
PEER_BLOCK = 256

kernel_name = 'hybrid_mla_rwkv7_stickbreak_ssd_peer_step'


def rmsnorm(x, g):
    xf = x.astype(jnp.float32)
    inv = lax.rsqrt(jnp.mean(xf * xf, axis=-1, keepdims=True) + EPS)
    return (xf * inv).astype(x.dtype) * g


def rope(x, pos):
    half = x.shape[-1] // 2
    freqs = ROPE_BASE ** (-jnp.arange(half, dtype=jnp.float32) / half)
    ang = pos.astype(jnp.float32)[:, None] * freqs[None, :]
    shape = (ang.shape[0],) + (1,) * (x.ndim - 3) + (half,)
    cos, sin = jnp.cos(ang).reshape(shape), jnp.sin(ang).reshape(shape)
    x1, x2 = x[..., :half], x[..., half:]
    return jnp.concatenate([x1 * cos - x2 * sin, x1 * sin + x2 * cos], axis=-1).astype(x.dtype)


def sweep_query_blocks(fn, qs, q_pos):
    T = q_pos.shape[0]
    nb = T // Q_BLOCK
    split = lambda a: jnp.moveaxis(a.reshape(a.shape[0], nb, Q_BLOCK, *a.shape[2:]), 1, 0)
    out = lax.map(lambda blk: fn(*blk[0], blk[1]), (tuple(split(a) for a in qs), q_pos.reshape(nb, Q_BLOCK)))
    out = jnp.moveaxis(out, 0, 1)
    return out.reshape(out.shape[0], T, *out.shape[3:])


def mla_attend(qn, qr, kn, kr, ckv, w_uv, q_pos, k_pos):
    s = jnp.einsum('bthd,bshd->bhts', qn, kn) + jnp.einsum('bthr,bsr->bhts', qr, kr)
    s = s.astype(jnp.float32) * MLA_SCALE
    s = jnp.where((k_pos[None, :] <= q_pos[:, None])[None, None], s, -jnp.inf)
    p = jax.nn.softmax(s, axis=-1).astype(ckv.dtype)
    o_lat = jnp.einsum('bhts,bsc->bthc', p, ckv)
    return jnp.einsum('bthc,chd->bthd', o_lat, w_uv.reshape(D_C, H_A, DV_A))


def rwkv7_mix(pr, prev, state0, P):
    B, T, _ = pr.shape
    HB = H_B * HS_B
    shifted = jnp.concatenate([prev[:, None, :], pr[:, :-1, :]], axis=1)
    xs = pr + P['mu'] * (shifted - pr)
    r, k, v, xw, xa, xg = jnp.split(xs, [HB, 2 * HB, 3 * HB, 3 * HB + D_WL, 3 * HB + D_WL + D_AL], axis=-1)
    log_w = -RWKV_DECAY_SCALE * jax.nn.sigmoid((P['w0'] + jnp.tanh(xw) @ P['w_up']).astype(jnp.float32))
    a = jax.nn.sigmoid(P['a0'] + xa @ P['a_up'])
    g = jax.nn.sigmoid(xg) @ P['g_up']
    kk = (k * P['k_k']).reshape(B, T, H_B, HS_B).astype(jnp.float32)
    kk = kk * lax.rsqrt(jnp.maximum(jnp.sum(kk * kk, axis=-1, keepdims=True), 1e-12))
    k = k * (1.0 + (a - 1.0) * P['k_a'])
    heads = lambda t: t.reshape(B, T, H_B, HS_B).astype(jnp.float32)
    r_h, k_h, v_h, a_h = heads(r), heads(k), heads(v), heads(a)
    w_h = jnp.exp(log_w).reshape(B, T, H_B, HS_B)

    def step(S, inp):
        rt, wt, kt, vt, kkt, at = inp
        sa = jnp.einsum('bhvk,bhk->bhv', S, -kkt)
        S = S * wt[:, :, None, :] + sa[..., None] * (kkt * at)[:, :, None, :] + vt[..., None] * kt[:, :, None, :]
        return S, jnp.einsum('bhvk,bhk->bhv', S, rt)

    tm = lambda t: jnp.moveaxis(t, 1, 0)
    S_final, o = lax.scan(step, state0.astype(jnp.float32), (tm(r_h), tm(w_h), tm(k_h), tm(v_h), tm(kk), tm(a_h)))
    o = jnp.moveaxis(o, 0, 1)
    mu = jnp.mean(o, axis=-1, keepdims=True)
    var = jnp.mean(jnp.square(o - mu), axis=-1, keepdims=True)
    o = ((o - mu) * lax.rsqrt(var + RWKV_LN_EPS)).reshape(B, T, HB) * P['ln_g'] + P['ln_b']
    bonus = jnp.sum(r_h * k_h * P['r_k'].astype(jnp.float32), axis=-1, keepdims=True) * v_h
    o = (o + bonus.reshape(B, T, HB)) * g
    return o.astype(pr.dtype), pr[:, -1, :], S_final.astype(state0.dtype)


def even_mixer(h, pos, past, rw_prev, rw_state, P):
    B, T, _ = h.shape
    proj = h @ P['w_in']
    c_q, c_kv, k_r, pr = jnp.split(proj, [D_CQ, D_CQ + D_C, D_CQ + D_C + DR_A], axis=-1)
    q = (rmsnorm(c_q, P['g_cq']) @ P['w_uq']).reshape(B, T, H_A, DN_A + DR_A)
    qn = rmsnorm(q[..., :DN_A], P['g_qn'])
    qr = rope(rmsnorm(q[..., DN_A:], P['g_qr']), pos)
    ckv_new = rmsnorm(c_kv, P['g_ckv'])
    kr_new = rope(rmsnorm(k_r, P['g_kr']), pos)
    if past is None:
        ckv, kr, k_pos = ckv_new, kr_new, pos
    else:
        ckv = jnp.concatenate([past[0], ckv_new], axis=1)
        kr = jnp.concatenate([past[1], kr_new], axis=1)
        k_pos = jnp.concatenate([past[2], pos])
    S = ckv.shape[1]
    kn = rmsnorm((ckv @ P['w_uk']).reshape(B, S, H_A, DN_A), P['g_kn'])
    attend = lambda qn_b, qr_b, pos_b: mla_attend(qn_b, qr_b, kn, kr, ckv, P['w_uv'], pos_b, k_pos)
    if past is None:
        o_a = sweep_query_blocks(attend, (qn, qr), pos)
    else:
        o_a = attend(qn, qr, pos)
    o_b, rw_last, rw_new = rwkv7_mix(pr, rw_prev, rw_state, P)
    out = jnp.concatenate([o_a.reshape(B, T, H_A * DV_A), o_b], axis=-1) @ P['w_out']
    return out, ckv_new, kr_new, rw_last, rw_new


def stick_breaking(q, k, v, q_pos, k_pos):
    z = jnp.einsum('bthd,bshd->bhts', q, k).astype(jnp.float32) * (HD_C ** -0.5)
    mask = (k_pos[None, :] < q_pos[:, None])[None, None]
    log_1mb = jnp.where(mask, jax.nn.log_sigmoid(-z), 0.0)
    suffix = lax.cumsum(log_1mb, axis=3, reverse=True) - log_1mb
    w = jnp.where(mask, jnp.exp(jax.nn.log_sigmoid(z) + suffix), 0.0)
    return jnp.einsum('bhts,bshd->bthd', w.astype(v.dtype), v)


def causal_dwconv(padded, w, b):
    out = lax.conv_general_dilated(padded, w[:, None, :], window_strides=(1,), padding='VALID',
                                   dimension_numbers=('NWC', 'WIO', 'NWC'), feature_group_count=padded.shape[-1])
    return out + b


def ssd_scan(x, dt, a, b, c, h0, chunk):
    Bsz, L = x.shape[:2]
    nc = L // chunk
    f32 = jnp.float32
    xr = x.reshape(Bsz, nc, chunk, H_D, P_D).astype(f32)
    dtr = dt.reshape(Bsz, nc, chunk, H_D)
    br = b.reshape(Bsz, nc, chunk, H_D, N_D).astype(f32)
    cr = c.reshape(Bsz, nc, chunk, H_D, N_D).astype(f32)
    acs = jnp.cumsum(dtr * a, axis=2)
    xdt = xr * dtr[..., None]
    causal = jnp.tril(jnp.ones((chunk, chunk), dtype=bool))[None, None, :, :, None]
    seg = acs[:, :, :, None, :] - acs[:, :, None, :, :]
    lmat = jnp.exp(jnp.where(causal, seg, -jnp.inf))
    scores = jnp.einsum('bcthn,bcshn->bctsh', cr, br) * lmat
    y_diag = jnp.einsum('bctsh,bcshp->bcthp', scores, xdt)
    decay_to_end = jnp.exp(acs[:, :, -1:, :] - acs)
    chunk_states = jnp.einsum('bclhn,bclh,bclhp->bchpn', br, decay_to_end, xdt)
    chunk_decay = jnp.exp(acs[:, :, -1, :])

    def step(hc, inp):
        st, dec = inp
        return hc * dec[:, :, None, None] + st, hc

    h_final, h_prev = lax.scan(step, h0.astype(f32), (jnp.moveaxis(chunk_states, 1, 0), jnp.moveaxis(chunk_decay, 1, 0)))
    h_prev = jnp.moveaxis(h_prev, 0, 1)
    y_off = jnp.einsum('bclhn,bchpn,bclh->bclhp', cr, h_prev, jnp.exp(acs))
    return (y_diag + y_off).reshape(Bsz, L, H_D, P_D), h_final.astype(h0.dtype)


def odd_mixer(h, pos, past, conv_buf, ssm_state, P):
    B, T, _ = h.shape
    HC = H_C * HD_C
    proj = h @ P['w_in']
    q, k, v, z, xbc, dt_raw = jnp.split(proj, [HC, 2 * HC, 3 * HC, 3 * HC + D_INNER, 3 * HC + D_INNER + CONV_DIM], axis=-1)
    q = q.reshape(B, T, H_C, HD_C)
    k_new = k.reshape(B, T, H_C, HD_C)
    v_new = v.reshape(B, T, H_C, HD_C)
    if past is None:
        o_c = sweep_query_blocks(lambda qb, pb: stick_breaking(qb, k_new, v_new, pb, pos), (q,), pos)
    else:
        k_all = jnp.concatenate([past[0], k_new], axis=1)
        v_all = jnp.concatenate([past[1], v_new], axis=1)
        o_c = stick_breaking(q, k_all, v_all, pos, jnp.concatenate([past[2], pos]))
    padded = jnp.concatenate([conv_buf, xbc], axis=1)
    xbc_c = jax.nn.silu(causal_dwconv(padded, P['conv_w'], P['conv_b']))
    new_buf = padded[:, T:, :]
    xs, b_in, c_in = jnp.split(xbc_c, [D_INNER, D_INNER + G_D * N_D], axis=-1)
    dt = jax.nn.softplus(dt_raw.astype(jnp.float32) + P['dt_bias'].astype(jnp.float32))
    a = -jnp.exp(P['a_log'].astype(jnp.float32))
    b_h = jnp.repeat(b_in.reshape(B, T, G_D, N_D), H_D // G_D, axis=2)
    c_h = jnp.repeat(c_in.reshape(B, T, G_D, N_D), H_D // G_D, axis=2)
    x_h = xs.reshape(B, T, H_D, P_D)
    chunk = SSD_CHUNK if T % SSD_CHUNK == 0 else T
    y, h_new = ssd_scan(x_h, dt, a, b_h, c_h, ssm_state, chunk)
    y = y + P['d'].astype(jnp.float32)[:, None] * x_h.astype(jnp.float32)
    y = rmsnorm(y.reshape(B, T, D_INNER) * jax.nn.silu(z.astype(jnp.float32)), P['g_norm']).astype(h.dtype)
    out = jnp.concatenate([o_c.reshape(B, T, HC), y], axis=-1) @ P['w_out']
    return out, k_new, v_new, new_buf, h_new


def memory_kv(mem, g_src, w_kv, g_k):
    B = mem.shape[0]
    kv = (rmsnorm(mem, g_src) @ w_kv).reshape(B, N_MEM, 2, MEM_HEADS, MEM_HD)
    return rmsnorm(kv[:, :, 0], g_k), kv[:, :, 1]


def memory_attend(h, k, v, w_q, g_q, w_o):
    B, T, _ = h.shape
    q = rmsnorm((h @ w_q).reshape(B, T, MEM_HEADS, MEM_HD), g_q)
    s = jnp.einsum('bthd,bmhd->bhtm', q, k).astype(jnp.float32) * (MEM_HD ** -0.5)
    p = jax.nn.softmax(s, axis=-1).astype(v.dtype)
    return jnp.einsum('bhtm,bmhd->bthd', p, v).reshape(B, T, MEM_HEADS * MEM_HD) @ w_o


def peer_ffn(h, w_q, sub_keys, u_tab, v_tab):
    B, T, D = h.shape
    n = B * T
    flat = h.reshape(n, D)
    flat = jnp.pad(flat, ((0, (-n) % PEER_BLOCK), (0, 0)))
    blocks = flat.reshape(-1, PEER_BLOCK, D)

    def one(xb):
        q = (xb @ w_q).reshape(PEER_BLOCK, PEER_HEADS, 2, PEER_DQ // 2)
        s = jnp.einsum('bhcd,hckd->bhck', q, sub_keys).astype(jnp.float32)
        top_s, top_i = lax.top_k(s, PEER_TOPK)
        cand_s = (top_s[:, :, 0, :, None] + top_s[:, :, 1, None, :]).reshape(PEER_BLOCK, PEER_HEADS, PEER_TOPK * PEER_TOPK)
        cand_i = (top_i[:, :, 0, :, None] * N_KEYS + top_i[:, :, 1, None, :]).reshape(PEER_BLOCK, PEER_HEADS, PEER_TOPK * PEER_TOPK)
        best_s, best_j = lax.top_k(cand_s, PEER_TOPK)
        idx = jnp.take_along_axis(cand_i, best_j, axis=-1)
        gate = jax.nn.softmax(best_s, axis=-1).astype(xb.dtype)
        act = jax.nn.gelu(jnp.einsum('bhkd,bd->bhk', jnp.take(u_tab, idx, axis=0), xb))
        return jnp.einsum('bhk,bhkd->bd', gate * act, jnp.take(v_tab, idx, axis=0))

    out = lax.map(one, blocks).reshape(-1, D)[:n]
    return out.reshape(B, T, D)


def setup_inputs(seed: int = 0) -> dict:
    key = jax.random.key(seed)
    keys = iter(jax.random.split(key, 96))
    f32 = jnp.float32

    def nrm(shape, scale=1.0):
        return jax.random.normal(next(keys), shape, f32) * scale

    def gain(shape):
        return 1.0 + nrm(shape, 0.05)

    def unif(shape, lo, hi):
        return jax.random.uniform(next(keys), shape, f32, lo, hi)

    n_pages = PAST_LEN // PAGE_SIZE
    n_used = DEC_BATCH * n_pages
    n_pool = n_used + n_used // 4
    page_table = jax.random.permutation(next(keys), n_pool)[:n_used].reshape(DEC_BATCH, n_pages).astype(jnp.int32)
    HB = H_B * HS_B
    dt0 = jnp.exp(unif((N_ODD, H_D), math.log(1e-3), math.log(1e-1)))
    return {
        'x_prompt': nrm((BATCH, SEQ, D_MODEL)),
        'x_sample': nrm((DEC_BATCH, DEC_SEQ, D_MODEL)),
        'mem_prompt': nrm((BATCH, N_MEM, D_MODEL)),
        'page_table': page_table,
        'cache_mla_ckv': nrm((N_EVEN, n_pool, PAGE_SIZE, D_C)),
        'cache_mla_krope': nrm((N_EVEN, n_pool, PAGE_SIZE, DR_A)),
        'state_rwkv_shift': nrm((N_EVEN, DEC_BATCH, RW_WIDTH)),
        'state_rwkv_wkv': nrm((N_EVEN, DEC_BATCH, H_B, HS_B, HS_B), 0.5),
        'cache_sb_k': nrm((N_ODD, n_pool, PAGE_SIZE, H_C, HD_C)),
        'cache_sb_v': nrm((N_ODD, n_pool, PAGE_SIZE, H_C, HD_C)),
        'state_ssm_conv': nrm((N_ODD, DEC_BATCH, CONV_W - 1, CONV_DIM)),
        'state_ssm': nrm((N_ODD, DEC_BATCH, H_D, P_D, N_D), 0.1),
        'cache_mem_k': nrm((DEPTH, DEC_BATCH, N_MEM, MEM_HEADS, MEM_HD)),
        'cache_mem_v': nrm((DEPTH, DEC_BATCH, N_MEM, MEM_HEADS, MEM_HD)),
        'norm_mix': gain((DEPTH, D_MODEL)),
        'norm_mem': gain((DEPTH, D_MODEL)),
        'norm_ffn': gain((DEPTH, D_MODEL)),
        'w_in_even': nrm((N_EVEN, D_MODEL, IN_EVEN), D_MODEL ** -0.5),
        'w_out_even': nrm((N_EVEN, MIX_EVEN, D_MODEL), MIX_EVEN ** -0.5),
        'mla_g_cq': gain((N_EVEN, D_CQ)),
        'mla_w_uq': nrm((N_EVEN, D_CQ, H_A * (DN_A + DR_A)), D_CQ ** -0.5),
        'mla_g_ckv': gain((N_EVEN, D_C)),
        'mla_w_uk': nrm((N_EVEN, D_C, H_A * DN_A), D_C ** -0.5),
        'mla_w_uv': nrm((N_EVEN, D_C, H_A * DV_A), D_C ** -0.5),
        'mla_g_qn': gain((N_EVEN, DN_A)),
        'mla_g_kn': gain((N_EVEN, DN_A)),
        'mla_g_qr': gain((N_EVEN, DR_A)),
        'mla_g_kr': gain((N_EVEN, DR_A)),
        'rw_mu': unif((N_EVEN, RW_WIDTH), 0.0, 1.0),
        'rw_w0': nrm((N_EVEN, HB), 0.5),
        'rw_w_up': nrm((N_EVEN, D_WL, HB), D_WL ** -0.5),
        'rw_a0': nrm((N_EVEN, HB), 0.1),
        'rw_a_up': nrm((N_EVEN, D_AL, HB), D_AL ** -0.5),
        'rw_g_up': nrm((N_EVEN, D_GL, HB), D_GL ** -0.5),
        'rw_k_k': 0.85 + nrm((N_EVEN, HB), 0.05),
        'rw_k_a': gain((N_EVEN, HB)),
        'rw_r_k': nrm((N_EVEN, H_B, HS_B), 0.1),
        'rw_ln_g': gain((N_EVEN, HB)),
        'rw_ln_b': nrm((N_EVEN, HB), 0.02),
        'w_in_odd': nrm((N_ODD, D_MODEL, IN_ODD), D_MODEL ** -0.5),
        'w_out_odd': nrm((N_ODD, MIX_ODD, D_MODEL), MIX_ODD ** -0.5),
        'ssm_conv_w': nrm((N_ODD, CONV_W, CONV_DIM), CONV_W ** -0.5),
        'ssm_conv_b': nrm((N_ODD, CONV_DIM), 0.02),
        'ssm_dt_bias': dt0 + jnp.log(-jnp.expm1(-dt0)),
        'ssm_a_log': jnp.log(unif((N_ODD, H_D), 1.0, 16.0)),
        'ssm_d': gain((N_ODD, H_D)),
        'ssm_g_norm': gain((N_ODD, D_INNER)),
        'mem_g_src': gain((DEPTH, D_MODEL)),
        'mem_w_q': nrm((DEPTH, D_MODEL, MEM_HEADS * MEM_HD), D_MODEL ** -0.5),
        'mem_w_kv': nrm((DEPTH, D_MODEL, 2 * MEM_HEADS * MEM_HD), D_MODEL ** -0.5),
        'mem_g_q': gain((DEPTH, MEM_HD)),
        'mem_g_k': gain((DEPTH, MEM_HD)),
        'mem_w_o': nrm((DEPTH, MEM_HEADS * MEM_HD, D_MODEL), (MEM_HEADS * MEM_HD) ** -0.5),
        'peer_w_q': nrm((DEPTH, D_MODEL, PEER_HEADS * PEER_DQ), D_MODEL ** -0.5),
        'peer_sub_keys': nrm((DEPTH, PEER_HEADS, 2, N_KEYS, PEER_DQ // 2), (PEER_DQ // 2) ** -0.5),
        'peer_u': nrm((DEPTH, N_EXPERTS, D_MODEL), D_MODEL ** -0.5),
        'peer_v': nrm((DEPTH, N_EXPERTS, D_MODEL), PEER_HEADS ** -0.5),
    }


def reference(x_prompt, x_sample, mem_prompt, page_table,
              cache_mla_ckv, cache_mla_krope, state_rwkv_shift, state_rwkv_wkv,
              cache_sb_k, cache_sb_v, state_ssm_conv, state_ssm, cache_mem_k, cache_mem_v,
              norm_mix, norm_mem, norm_ffn,
              w_in_even, w_out_even, mla_g_cq, mla_w_uq, mla_g_ckv, mla_w_uk, mla_w_uv,
              mla_g_qn, mla_g_kn, mla_g_qr, mla_g_kr,
              rw_mu, rw_w0, rw_w_up, rw_a0, rw_a_up, rw_g_up, rw_k_k, rw_k_a, rw_r_k, rw_ln_g, rw_ln_b,
              w_in_odd, w_out_odd, ssm_conv_w, ssm_conv_b, ssm_dt_bias, ssm_a_log, ssm_d, ssm_g_norm,
              mem_g_src, mem_w_q, mem_w_kv, mem_g_q, mem_g_k, mem_w_o,
              peer_w_q, peer_sub_keys, peer_u, peer_v):
    n_pages = PAST_LEN // PAGE_SIZE
    n_dec = x_sample.shape[0]
    n_pr = x_prompt.shape[0]
    pos_p = jnp.arange(SEQ, dtype=jnp.int32)
    pos_s = PAST_LEN + jnp.arange(DEC_SEQ, dtype=jnp.int32)
    pos_past = jnp.arange(PAST_LEN, dtype=jnp.int32)

    def gather_pages(pool):
        g = pool[page_table]
        return g.reshape(n_dec, n_pages * PAGE_SIZE, *pool.shape[2:])

    ckv_p, kr_p, sh_p, wkv_p, sbk_p, sbv_p, conv_p, ssm_p, memk_p, memv_p = [], [], [], [], [], [], [], [], [], []
    ckv_s, kr_s, sh_s, wkv_s, sbk_s, sbv_s, conv_s, ssm_s = [], [], [], [], [], [], [], []
    xp, xs = x_prompt, x_sample
    for layer in range(DEPTH):
        i = layer // 2
        hp = rmsnorm(xp, norm_mix[layer])
        hs = rmsnorm(xs, norm_mix[layer])
        if layer % 2 == 0:
            P = dict(w_in=w_in_even[i], w_out=w_out_even[i], g_cq=mla_g_cq[i], w_uq=mla_w_uq[i],
                     g_ckv=mla_g_ckv[i], w_uk=mla_w_uk[i], w_uv=mla_w_uv[i], g_qn=mla_g_qn[i],
                     g_kn=mla_g_kn[i], g_qr=mla_g_qr[i], g_kr=mla_g_kr[i],
                     mu=rw_mu[i], w0=rw_w0[i], w_up=rw_w_up[i], a0=rw_a0[i], a_up=rw_a_up[i],
                     g_up=rw_g_up[i], k_k=rw_k_k[i], k_a=rw_k_a[i], r_k=rw_r_k[i],
                     ln_g=rw_ln_g[i], ln_b=rw_ln_b[i])
            op, a1, a2, a3, a4 = even_mixer(hp, pos_p, None,
                                            jnp.zeros((n_pr, RW_WIDTH), xp.dtype),
                                            jnp.zeros((n_pr, H_B, HS_B, HS_B), state_rwkv_wkv.dtype), P)
            ckv_p.append(a1); kr_p.append(a2); sh_p.append(a3); wkv_p.append(a4)
            past = (gather_pages(cache_mla_ckv[i]), gather_pages(cache_mla_krope[i]), pos_past)
            os_, b1, b2, b3, b4 = even_mixer(hs, pos_s, past, state_rwkv_shift[i], state_rwkv_wkv[i], P)
            ckv_s.append(b1); kr_s.append(b2); sh_s.append(b3); wkv_s.append(b4)
        else:
            P = dict(w_in=w_in_odd[i], w_out=w_out_odd[i], conv_w=ssm_conv_w[i], conv_b=ssm_conv_b[i],
                     dt_bias=ssm_dt_bias[i], a_log=ssm_a_log[i], d=ssm_d[i], g_norm=ssm_g_norm[i])
            op, a1, a2, a3, a4 = odd_mixer(hp, pos_p, None,
                                           jnp.zeros((n_pr, CONV_W - 1, CONV_DIM), xp.dtype),
                                           jnp.zeros((n_pr, H_D, P_D, N_D), state_ssm.dtype), P)
            sbk_p.append(a1); sbv_p.append(a2); conv_p.append(a3); ssm_p.append(a4)
            past = (gather_pages(cache_sb_k[i]), gather_pages(cache_sb_v[i]), pos_past)
            os_, b1, b2, b3, b4 = odd_mixer(hs, pos_s, past, state_ssm_conv[i], state_ssm[i], P)
            sbk_s.append(b1); sbv_s.append(b2); conv_s.append(b3); ssm_s.append(b4)
        xp = xp + op
        xs = xs + os_
        mk, mv = memory_kv(mem_prompt, mem_g_src[layer], mem_w_kv[layer], mem_g_k[layer])
        memk_p.append(mk); memv_p.append(mv)
        xp = xp + memory_attend(rmsnorm(xp, norm_mem[layer]), mk, mv, mem_w_q[layer], mem_g_q[layer], mem_w_o[layer])
        xs = xs + memory_attend(rmsnorm(xs, norm_mem[layer]), cache_mem_k[layer], cache_mem_v[layer],
                                mem_w_q[layer], mem_g_q[layer], mem_w_o[layer])
        xp = xp + peer_ffn(rmsnorm(xp, norm_ffn[layer]), peer_w_q[layer], peer_sub_keys[layer], peer_u[layer], peer_v[layer])
        xs = xs + peer_ffn(rmsnorm(xs, norm_ffn[layer]), peer_w_q[layer], peer_sub_keys[layer], peer_u[layer], peer_v[layer])
    return (xp, xs,
            jnp.stack(ckv_p), jnp.stack(kr_p), jnp.stack(sh_p), jnp.stack(wkv_p),
            jnp.stack(sbk_p), jnp.stack(sbv_p), jnp.stack(conv_p), jnp.stack(ssm_p),
            jnp.stack(memk_p), jnp.stack(memv_p),
            jnp.stack(ckv_s), jnp.stack(kr_s), jnp.stack(sh_s), jnp.stack(wkv_s),
            jnp.stack(sbk_s), jnp.stack(sbv_s), jnp.stack(conv_s), jnp.stack(ssm_s))
```

```python
import functools
import math

import numpy as np
import jax
import jax.numpy as jnp
from jax import lax
from jax.experimental import pallas as pl
from jax.experimental.pallas import tpu as pltpu
from jax.experimental.pallas import tpu_sc as plsc

F32 = jnp.float32
BF16 = jnp.bfloat16
I32 = jnp.int32

D_MODEL = 1024
EPS = 1e-6
PAST_LEN = 8192
PAGE_SIZE = 128
N_PAGES = PAST_LEN // PAGE_SIZE

H_A, DN_A, DR_A, DV_A, D_CQ, D_C = 8, 64, 32, 64, 384, 256
ROPE_BASE = 10000.0
MLA_SCALE = (DN_A + DR_A) ** -0.5
H_B, HS_B, D_WL, D_AL, D_GL = 8, 64, 64, 64, 128
HB = H_B * HS_B
RW_WIDTH = 3 * HB + D_WL + D_AL + D_GL
RWKV_DECAY_SCALE = 0.606531
RWKV_LN_EPS = 64e-5
H_C, HD_C = 8, 64
HC = H_C * HD_C
H_D, P_D, N_D, G_D, CONV_W = 8, 64, 128, 2, 4
D_INNER = H_D * P_D
CONV_DIM = D_INNER + 2 * G_D * N_D
SSD_CHUNK = 128
N_MEM, MEM_HEADS, MEM_HD = 256, 4, 64
MEM_W = MEM_HEADS * MEM_HD
PEER_HEADS, N_KEYS, PEER_DQ, PEER_TOPK = 8, 128, 256, 16
LANES = 128
VMEM_LIMIT = 56 * 1024 * 1024


def _cp(*sem):
    return pltpu.CompilerParams(dimension_semantics=sem, vmem_limit_bytes=VMEM_LIMIT)


def _dot(a, b):
    return jnp.dot(a, b, preferred_element_type=F32)


def _dot_nt(a, b):
    return lax.dot_general(a, b, (((1,), (1,)), ((), ())), preferred_element_type=F32)


def _dot_tn(a, b):
    return lax.dot_general(a, b, (((0,), (0,)), ((), ())), preferred_element_type=F32)


def _split2(x):
    hi = x.astype(BF16)
    lo = (x - hi.astype(F32)).astype(BF16)
    return hi, lo


def _dot_f32(x, m):
    hi, lo = _split2(x)
    return _dot(hi, m) + _dot(lo, m)


def _dot_sel(x, m):
    hi = x.astype(BF16)
    r1 = x - hi.astype(F32)
    mid = r1.astype(BF16)
    lo = (r1 - mid.astype(F32)).astype(BF16)
    return (_dot(hi, m) + _dot(mid, m)) + _dot(lo, m)


def _rms(x, g):
    return x * lax.rsqrt(jnp.mean(x * x, axis=-1, keepdims=True) + EPS) * g


def _sigmoid(x):
    return 1.0 / (1.0 + jnp.exp(-x))


def _softplus(x):
    return jnp.maximum(x, 0.0) + jnp.log(1.0 + jnp.exp(-jnp.abs(x)))


def _blockdiag_ones(n, seg):
    i = np.arange(n)
    return (i[:, None] // seg == i[None, :] // seg).astype(np.float32)


def _nm_body(x_ref, g_ref, w_ref, o_ref):
    h = _rms(x_ref[...], g_ref[...])
    o_ref[...] = _dot(h.astype(BF16), w_ref[...])


def norm_matmul(x, g, w, tm=256):
    n, k = x.shape
    m = w.shape[1]
    return pl.pallas_call(
        _nm_body,
        grid=(n // tm,),
        in_specs=[pl.BlockSpec((tm, k), lambda i: (i, 0)),
                  pl.BlockSpec((1, k), lambda i: (0, 0)),
                  pl.BlockSpec((k, m), lambda i: (0, 0))],
        out_specs=pl.BlockSpec((tm, m), lambda i: (i, 0)),
        out_shape=jax.ShapeDtypeStruct((n, m), F32),
        compiler_params=_cp("parallel"),
        name="norm_matmul",
    )(x, g.reshape(1, k), w)


def _mr_body(*refs, n_in):
    a_refs, w_refs, r_ref, o_ref = refs[:n_in], refs[n_in:2 * n_in], refs[2 * n_in], refs[2 * n_in + 1]
    acc = r_ref[...]
    for a_ref, w_ref in zip(a_refs, w_refs):
        acc = acc + _dot(a_ref[...].astype(BF16), w_ref[...])
    o_ref[...] = acc


def matmul_res(a_list, w_list, res, tm=256):
    n, m = res.shape
    n_in = len(a_list)
    in_specs = ([pl.BlockSpec((tm, a.shape[1]), lambda i: (i, 0)) for a in a_list]
                + [pl.BlockSpec(w.shape, lambda i: (0, 0)) for w in w_list]
                + [pl.BlockSpec((tm, m), lambda i: (i, 0))])
    return pl.pallas_call(
        functools.partial(_mr_body, n_in=n_in),
        grid=(n // tm,),
        in_specs=in_specs,
        out_specs=pl.BlockSpec((tm, m), lambda i: (i, 0)),
        out_shape=jax.ShapeDtypeStruct((n, m), F32),
        compiler_params=_cp("parallel"),
        name="matmul_res",
    )(*a_list, *w_list, res)


def _memkv_body(m_ref, g_ref, w_ref, gk_ref, e_ref, k_ref, v_ref):
    h = _rms(m_ref[...], g_ref[...])
    kv = _dot(h.astype(BF16), w_ref[...])
    k = kv[:, :MEM_W]
    ms = _dot_f32(k * k, e_ref[...]) * (1.0 / MEM_HD)
    k_ref[...] = k * lax.rsqrt(ms + EPS) * gk_ref[...]
    v_ref[...] = kv[:, MEM_W:]


def memory_kv(mem2d, g_src, w_kv, g_k, tm=256):
    n = mem2d.shape[0]
    e = jnp.asarray(_blockdiag_ones(MEM_W, MEM_HD), BF16)
    gk = jnp.tile(g_k, MEM_HEADS).reshape(1, MEM_W)
    return pl.pallas_call(
        _memkv_body,
        grid=(n // tm,),
        in_specs=[pl.BlockSpec((tm, D_MODEL), lambda i: (i, 0)),
                  pl.BlockSpec((1, D_MODEL), lambda i: (0, 0)),
                  pl.BlockSpec((D_MODEL, 2 * MEM_W), lambda i: (0, 0)),
                  pl.BlockSpec((1, MEM_W), lambda i: (0, 0)),
                  pl.BlockSpec((MEM_W, MEM_W), lambda i: (0, 0))],
        out_specs=[pl.BlockSpec((tm, MEM_W), lambda i: (i, 0))] * 2,
        out_shape=[jax.ShapeDtypeStruct((n, MEM_W), F32)] * 2,
        compiler_params=_cp("parallel"),
        name="memory_kv",
    )(mem2d, g_src.reshape(1, -1), w_kv.astype(BF16), gk, e)


def _memattn_body(x_ref, g_ref, wq_ref, gq_ref, e_ref, k_ref, v_ref, wo_ref, o_ref, att_ref):
    x = x_ref[...]
    h = _rms(x, g_ref[...])
    q = _dot(h.astype(BF16), wq_ref[...])
    ms = _dot_f32(q * q, e_ref[...]) * (1.0 / MEM_HD)
    q = (q * lax.rsqrt(ms + EPS) * gq_ref[...]).astype(BF16)
    k = k_ref[0].astype(BF16)
    v = v_ref[0].astype(BF16)
    for hd in range(MEM_HEADS):
        sl = slice(hd * MEM_HD, (hd + 1) * MEM_HD)
        s = _dot_nt(q[:, sl], k[:, sl]) * (MEM_HD ** -0.5)
        p = jnp.exp(s - jnp.max(s, axis=-1, keepdims=True))
        p = p / jnp.sum(p, axis=-1, keepdims=True)
        att_ref[:, sl] = _dot(p.astype(BF16), v[:, sl])
    o_ref[...] = x + _dot(att_ref[...].astype(BF16), wo_ref[...])


def memory_attend(x, seq_t, g_norm, w_q, g_q, k, v, w_o, tq):
    n = x.shape[0]
    per_b = seq_t // tq
    e = jnp.asarray(_blockdiag_ones(MEM_W, MEM_HD), BF16)
    gq = jnp.tile(g_q, MEM_HEADS).reshape(1, MEM_W)
    return pl.pallas_call(
        _memattn_body,
        grid=(n // tq,),
        in_specs=[pl.BlockSpec((tq, D_MODEL), lambda i: (i, 0)),
                  pl.BlockSpec((1, D_MODEL), lambda i: (0, 0)),
                  pl.BlockSpec((D_MODEL, MEM_W), lambda i: (0, 0)),
                  pl.BlockSpec((1, MEM_W), lambda i: (0, 0)),
                  pl.BlockSpec((MEM_W, MEM_W), lambda i: (0, 0)),
                  pl.BlockSpec((1, N_MEM, MEM_W), lambda i: (i // per_b, 0, 0)),
                  pl.BlockSpec((1, N_MEM, MEM_W), lambda i: (i // per_b, 0, 0)),
                  pl.BlockSpec((MEM_W, D_MODEL), lambda i: (0, 0))],
        out_specs=pl.BlockSpec((tq, D_MODEL), lambda i: (i, 0)),
        out_shape=jax.ShapeDtypeStruct((n, D_MODEL), F32),
        scratch_shapes=[pltpu.VMEM((tq, MEM_W), F32)],
        compiler_params=_cp("parallel"),
        name="memory_attend",
    )(x, g_norm.reshape(1, -1), w_q.astype(BF16), gq, e, k, v, w_o.astype(BF16))


def _topk_lanes(s, k, out_off, payload=None):
    rows, w = s.shape
    lane = lax.broadcasted_iota(I32, (rows, w), 1)
    olane = lax.broadcasted_iota(I32, (rows, LANES), 1)

    def body(r, carry):
        s, acc_v, acc_i = carry
        m = jnp.max(s, axis=-1, keepdims=True)
        first = jnp.min(jnp.where(s == m, lane, w), axis=-1, keepdims=True)
        hit = lane == first
        if payload is None:
            pick = first.astype(F32)
        else:
            pick = jnp.max(jnp.where(hit, payload, -1.0), axis=-1, keepdims=True)
        here = olane == (out_off + r)
        acc_v = jnp.where(here, m, acc_v)
        acc_i = jnp.where(here, pick, acc_i)
        return jnp.where(hit, -jnp.inf, s), acc_v, acc_i

    z = jnp.zeros((rows, LANES), F32)
    _, acc_v, acc_i = lax.fori_loop(0, k, body, (s, z, z))
    return acc_v, acc_i


def _route_body(x_ref, g_ref, wq_ref, keys_ref, r0_ref, r1_ref, e16_ref, hn_ref, idx_ref, gate_ref):
    h = _rms(x_ref[...], g_ref[...])
    hn_ref[...] = h
    q = _dot(h.astype(BF16), wq_ref[...])
    tops = []
    for c in range(2):
        s = _dot_nt(q[:, c * LANES:(c + 1) * LANES].astype(BF16), keys_ref[0, c])
        tops.append(_topk_lanes(s, PEER_TOPK, 0))
    (s0, i0), (s1, i1) = tops
    cand_s = _dot_sel(s0, r0_ref[...]) + _dot_sel(s1, r1_ref[...])
    cand_i = _dot(i0.astype(BF16), r0_ref[...]) * float(N_KEYS) + _dot(i1.astype(BF16), r1_ref[...])
    best_s, best_i = _topk_lanes(cand_s, PEER_TOPK, 0, payload=cand_i)
    lane = lax.broadcasted_iota(I32, best_s.shape, 1)
    valid = lane < PEER_TOPK
    e = jnp.where(valid, jnp.exp(best_s - jnp.max(jnp.where(valid, best_s, -jnp.inf), axis=-1, keepdims=True)), 0.0)
    gate = e / jnp.sum(e, axis=-1, keepdims=True)
    idx_ref[0] = best_i[:, :PEER_TOPK].astype(I32)
    gate_ref[0] = gate[:, :PEER_TOPK]


def peer_route(x, g, w_q, sub_keys, tm=128):
    n = x.shape[0]
    keys = sub_keys.astype(BF16)
    a = np.arange(PEER_TOPK * PEER_TOPK)
    r0 = np.zeros((LANES, PEER_TOPK * PEER_TOPK), np.float32)
    r1 = np.zeros((LANES, PEER_TOPK * PEER_TOPK), np.float32)
    r0[a // PEER_TOPK, a] = 1.0
    r1[a % PEER_TOPK, a] = 1.0
    e16 = _blockdiag_ones(LANES, PEER_TOPK)
    hn, idx, gate = pl.pallas_call(
        _route_body,
        grid=(n // tm, PEER_HEADS),
        in_specs=[pl.BlockSpec((tm, D_MODEL), lambda i, hd: (i, 0)),
                  pl.BlockSpec((1, D_MODEL), lambda i, hd: (0, 0)),
                  pl.BlockSpec((D_MODEL, PEER_DQ), lambda i, hd: (0, hd)),
                  pl.BlockSpec((1, 2, N_KEYS, PEER_DQ // 2), lambda i, hd: (hd, 0, 0, 0)),
                  pl.BlockSpec(r0.shape, lambda i, hd: (0, 0)),
                  pl.BlockSpec(r1.shape, lambda i, hd: (0, 0)),
                  pl.BlockSpec(e16.shape, lambda i, hd: (0, 0))],
        out_specs=[pl.BlockSpec((tm, D_MODEL), lambda i, hd: (i, 0)),
                   pl.BlockSpec((1, tm, PEER_TOPK), lambda i, hd: (hd, i, 0)),
                   pl.BlockSpec((1, tm, PEER_TOPK), lambda i, hd: (hd, i, 0))],
        out_shape=[jax.ShapeDtypeStruct((n, D_MODEL), F32),
                   jax.ShapeDtypeStruct((PEER_HEADS, n, PEER_TOPK), I32),
                   jax.ShapeDtypeStruct((PEER_HEADS, n, PEER_TOPK), F32)],
        compiler_params=_cp("parallel", "arbitrary"),
        name="peer_route",
    )(x, g.reshape(1, -1), w_q.astype(BF16), keys, jnp.asarray(r0, BF16), jnp.asarray(r1, BF16),
      jnp.asarray(e16, BF16))
    idx = jnp.transpose(idx, (1, 0, 2)).reshape(n, PEER_HEADS * PEER_TOPK)
    gate = jnp.transpose(gate, (1, 0, 2)).reshape(n, PEER_HEADS * PEER_TOPK)
    return hn, idx, gate


HG = LANES
QW = H_A * HG
EVEN_CKV_OFF = D_CQ
EVEN_KR_OFF = D_CQ + D_C
EVEN_PR_OFF = EVEN_KR_OFF + HG
EVEN_W = EVEN_PR_OFF + RW_WIDTH


def _mla_prep_body(p_ref, cs_ref, sn_ref, gcq_ref, wuq_ref, gq_ref, gckv_ref, gkr_ref, wuk_ref, gk_ref, wuv_ref,
                   eq_ref, ek_ref, perm_ref, q_out, k_out, v_out, ckv_out, kr_out):
    cs, sn = cs_ref[...], sn_ref[...]

    def rope(x):
        return x * cs + _dot_sel(x, perm_ref[...]) * sn

    q = _dot(_rms(p_ref[:, :D_CQ], gcq_ref[...]).astype(BF16), wuq_ref[...])
    ckv = _rms(p_ref[:, EVEN_CKV_OFF:EVEN_CKV_OFF + D_C], gckv_ref[...])
    ckv_out[...] = ckv
    krr = p_ref[:, EVEN_KR_OFF:EVEN_KR_OFF + HG]
    ms = jnp.sum(krr * krr, axis=-1, keepdims=True) * (1.0 / DR_A)
    kr = rope(krr * lax.rsqrt(ms + EPS) * gkr_ref[...])
    kr_out[...] = kr
    ckv_b = ckv.astype(BF16)
    kraw = _dot(ckv_b, wuk_ref[...])
    v_out[...] = _dot(ckv_b, wuv_ref[...]).astype(BF16)
    for h in range(H_A):
        sl = slice(h * HG, (h + 1) * HG)
        qh = q[:, sl]
        qh = qh * lax.rsqrt(_dot_f32(qh * qh, eq_ref[...]) + EPS) * gq_ref[:, sl]
        q_out[:, sl] = rope(qh).astype(BF16)
        kh = kraw[:, sl]
        kh = kh * lax.rsqrt(_dot_f32(kh * kh, ek_ref[...]) + EPS) * gk_ref[:, sl]
        k_out[:, sl] = (kh + kr).astype(BF16)


def _head_groups(w, width, off=0):
    k = w.shape[0]
    out = jnp.zeros((k, H_A, HG), w.dtype)
    out = out.at[:, :, off:off + width].set(w.reshape(k, H_A, width))
    return out.reshape(k, H_A * HG)


def mla_prep(proj, pos, P, tm=256):
    n = proj.shape[0]
    half = DR_A // 2
    freqs = ROPE_BASE ** (-jnp.arange(half, dtype=F32) / half)
    ang = pos.astype(F32)[:, None] * freqs[None, :]
    cos, sin = jnp.cos(ang), jnp.sin(ang)
    one, zero = jnp.ones((n, DN_A), F32), jnp.zeros((n, DN_A), F32)
    pad = jnp.zeros((n, HG - DN_A - DR_A), F32)
    cs = jnp.concatenate([one, cos, cos, pad], axis=1)
    sn = jnp.concatenate([zero, -sin, sin, pad], axis=1)
    wuq = P['w_uq'].reshape(D_CQ, H_A, DN_A + DR_A)
    wuq = jnp.pad(wuq, ((0, 0), (0, 0), (0, HG - DN_A - DR_A))).reshape(D_CQ, QW).astype(BF16)
    gq = jnp.tile(jnp.concatenate([P['g_qn'], P['g_qr'], jnp.zeros((HG - DN_A - DR_A,), F32)]), H_A).reshape(1, QW)
    gkr = jnp.zeros((1, HG), F32).at[0, DN_A:DN_A + DR_A].set(P['g_kr'])
    wuk = _head_groups(P['w_uk'], DN_A).astype(BF16)
    wuv = _head_groups(P['w_uv'], DV_A).astype(BF16)
    gk = _head_groups(jnp.tile(P['g_kn'], H_A).reshape(1, -1), DN_A)
    eq = np.zeros((HG, HG), np.float32)
    eq[:DN_A, :DN_A] = 1.0 / DN_A
    eq[DN_A:DN_A + DR_A, DN_A:DN_A + DR_A] = 1.0 / DR_A
    ek = np.zeros((HG, HG), np.float32)
    ek[:DN_A, :DN_A] = 1.0 / DN_A
    perm = np.zeros((HG, HG), np.float32)
    j = np.arange(half)
    perm[DN_A + half + j, DN_A + j] = 1.0
    perm[DN_A + j, DN_A + half + j] = 1.0
    row = lambda a: pl.BlockSpec(a.shape, lambda i: (0, 0))
    consts = [P['g_cq'].reshape(1, -1), wuq, gq, P['g_ckv'].reshape(1, -1), gkr, wuk, gk, wuv,
              jnp.asarray(eq, BF16), jnp.asarray(ek, BF16), jnp.asarray(perm, BF16)]
    return pl.pallas_call(
        _mla_prep_body,
        grid=(n // tm,),
        in_specs=[pl.BlockSpec((tm, EVEN_PR_OFF), lambda i: (i, 0)),
                  pl.BlockSpec((tm, HG), lambda i: (i, 0)),
                  pl.BlockSpec((tm, HG), lambda i: (i, 0))] + [row(c) for c in consts],
        out_specs=[pl.BlockSpec((tm, QW), lambda i: (i, 0))] * 3
        + [pl.BlockSpec((tm, D_C), lambda i: (i, 0)), pl.BlockSpec((tm, HG), lambda i: (i, 0))],
        out_shape=[jax.ShapeDtypeStruct((n, QW), BF16)] * 3
        + [jax.ShapeDtypeStruct((n, D_C), F32), jax.ShapeDtypeStruct((n, HG), F32)],
        compiler_params=_cp("parallel"),
        name="mla_prep",
    )(proj, cs, sn, *consts)


def _mla_prompt_body(q_ref, k_ref, v_ref, o_ref, *, tq):
    qi = pl.program_id(1)
    row = lax.broadcasted_iota(I32, (tq, tq), 0)
    col = lax.broadcasted_iota(I32, (tq, tq), 1)
    for h in range(H_A):
        sl = slice(h * HG, (h + 1) * HG)
        q = q_ref[:, sl]

        def body(kb, carry):
            m, l, acc = carry
            off = pl.multiple_of(kb * tq, tq)
            s = _dot_nt(q, k_ref[pl.ds(off, tq), sl]) * MLA_SCALE
            s = jnp.where(col + kb * tq <= row + qi * tq, s, -jnp.inf)
            m_new = jnp.maximum(m, jnp.max(s, axis=-1, keepdims=True))
            alpha = jnp.exp(m - m_new)
            p = jnp.exp(s - m_new)
            l = l * alpha + jnp.sum(p, axis=-1, keepdims=True)
            acc = acc * alpha + _dot(p.astype(BF16), v_ref[pl.ds(off, tq), sl])
            return m_new, l, acc

        init = (jnp.full((tq, 1), -jnp.inf, F32), jnp.zeros((tq, 1), F32), jnp.zeros((tq, HG), F32))
        _, l, acc = lax.fori_loop(0, qi + 1, body, init)
        o_ref[:, h * DV_A:(h + 1) * DV_A] = (acc / l)[:, :DV_A].astype(BF16)


def mla_prompt_attention(qc, kc, vc, b, t, tq=256):
    nq = t // tq
    return pl.pallas_call(
        functools.partial(_mla_prompt_body, tq=tq),
        grid=(b, nq),
        in_specs=[pl.BlockSpec((tq, QW), lambda bi, qi: (bi * nq + qi, 0)),
                  pl.BlockSpec((t, QW), lambda bi, qi: (bi, 0)),
                  pl.BlockSpec((t, QW), lambda bi, qi: (bi, 0))],
        out_specs=pl.BlockSpec((tq, H_A * DV_A), lambda bi, qi: (bi * nq + qi, 0)),
        out_shape=jax.ShapeDtypeStruct((b * t, H_A * DV_A), BF16),
        compiler_params=_cp("parallel", "arbitrary"),
        name="mla_prompt_attention",
    )(qc, kc, vc)


PAGES_PER_STEP = 8
KV_STEPS = N_PAGES // PAGES_PER_STEP


def _mla_sample_body(pt_ref, *refs, t_new):
    npg = PAGES_PER_STEP
    ckv_pages, kr_pages = refs[:npg], refs[npg:2 * npg]
    (ckv_new, kr_new, qn_ref, qr_ref, wuk_ref, gk_ref, e_ref, wuv_ref, hm_ref, o_ref,
     m_scr, l_scr, acc_scr) = refs[2 * npg:]
    j = pl.program_id(1)

    def process(ckv, kr, mask):
        ckv_b = ckv.astype(BF16)
        kraw = _dot(ckv_b, wuk_ref[...])
        ms = _dot_f32(kraw * kraw, e_ref[...]) * (1.0 / DN_A)
        kn = (kraw * lax.rsqrt(ms + EPS) * gk_ref[...]).astype(BF16)
        s = (_dot(kn, qn_ref[0]) + _dot(kr.astype(BF16), qr_ref[0])) * MLA_SCALE
        if mask is not None:
            s = jnp.where(mask, s, -jnp.inf)
        m = m_scr[...]
        m_new = jnp.maximum(m, jnp.max(s, axis=0, keepdims=True))
        alpha = jnp.exp(m - m_new)
        p = jnp.exp(s - m_new)
        l_scr[...] = l_scr[...] * alpha + jnp.sum(p, axis=0, keepdims=True)
        acc_scr[...] = acc_scr[...] * alpha + _dot_tn(ckv_b, p.astype(BF16))
        m_scr[...] = m_new

    @pl.when(j == 0)
    def _():
        m_scr[...] = jnp.full(m_scr.shape, -jnp.inf, F32)
        l_scr[...] = jnp.zeros(l_scr.shape, F32)
        acc_scr[...] = jnp.zeros(acc_scr.shape, F32)
        nk, nq = PAGE_SIZE, H_A * t_new
        key = lax.broadcasted_iota(I32, (nk, nq), 0)
        qt = lax.broadcasted_iota(I32, (nk, nq), 1) % t_new
        process(ckv_new[0], kr_new[0], key <= qt)

    for p in range(npg):
        process(ckv_pages[p][0], kr_pages[p][0], None)

    @pl.when(j == KV_STEPS - 1)
    def _():
        o_lat = (acc_scr[...] / l_scr[...]).astype(BF16)
        full = _dot_tn(o_lat, wuv_ref[...]) * hm_ref[...]
        out = full[0:t_new]
        for h in range(1, H_A):
            out = out + full[h * t_new:(h + 1) * t_new]
        o_ref[0] = out.astype(BF16)


def mla_sample_attention(page_table, cache_ckv, cache_kr, ckv_new, kr_new, qn, qr, P):
    b, t_new = qn.shape[:2]
    nq = H_A * t_new
    eye = jnp.eye(H_A, dtype=BF16)
    qn_bd = jnp.einsum('bthd,hg->bhdgt', qn, eye).reshape(b, H_A * DN_A, nq)
    qr_m = jnp.transpose(qr, (0, 3, 2, 1)).reshape(b, DR_A, nq)
    pad = lambda a: jnp.pad(a, ((0, 0), (0, PAGE_SIZE - t_new), (0, 0)))
    hm = (np.arange(nq)[:, None] // t_new == np.arange(H_A * DV_A)[None, :] // DV_A).astype(np.float32)
    npg = PAGES_PER_STEP
    page = lambda p, w: pl.BlockSpec((1, PAGE_SIZE, w), lambda bi, j, pt: (pt[bi * N_PAGES + j * npg + p], 0, 0))
    per_b = lambda s: pl.BlockSpec((1,) + s, lambda bi, j, pt: (bi, 0, 0))
    const = lambda a: pl.BlockSpec(a.shape, lambda bi, j, pt: (0, 0))
    consts = [P['w_uk'].astype(BF16), jnp.tile(P['g_kn'], H_A).reshape(1, -1),
              jnp.asarray(_blockdiag_ones(H_A * DN_A, DN_A), BF16), P['w_uv'].astype(BF16), jnp.asarray(hm)]
    grid_spec = pltpu.PrefetchScalarGridSpec(
        num_scalar_prefetch=1,
        grid=(b, KV_STEPS),
        in_specs=[page(p, D_C) for p in range(npg)] + [page(p, DR_A) for p in range(npg)]
        + [per_b((PAGE_SIZE, D_C)), per_b((PAGE_SIZE, DR_A)), per_b((H_A * DN_A, nq)), per_b((DR_A, nq))]
        + [const(c) for c in consts],
        out_specs=pl.BlockSpec((1, t_new, H_A * DV_A), lambda bi, j, pt: (bi, 0, 0)),
        scratch_shapes=[pltpu.VMEM((1, nq), F32), pltpu.VMEM((1, nq), F32), pltpu.VMEM((D_C, nq), F32)],
    )
    out = pl.pallas_call(
        functools.partial(_mla_sample_body, t_new=t_new),
        grid_spec=grid_spec,
        out_shape=jax.ShapeDtypeStruct((b, t_new, H_A * DV_A), BF16),
        compiler_params=_cp("parallel", "arbitrary"),
        name="mla_sample_attention",
    )(page_table.reshape(-1), *([cache_ckv] * npg), *([cache_kr] * npg), pad(ckv_new), pad(kr_new),
      qn_bd, qr_m, *consts)
    return out.reshape(b * t_new, H_A * DV_A)


RW_LORA_OFF = 3 * HB


def _rwkv_prep_body(pr_ref, sh_ref, mu_ref, w0_ref, wup_ref, a0_ref, aup_ref, gup_ref, kk_ref, ka_ref, rk_ref, e_ref,
                    nkk_o, wr_o, w_o, kka_o, k2_o, v_o, c1_o, c2_o, g_o, bonus_o):
    pr = pr_ref[...]
    xs = pr + mu_ref[...] * (sh_ref[...] - pr)
    r, k, v = xs[:, :HB], xs[:, HB:2 * HB], xs[:, 2 * HB:3 * HB]
    xwa = xs[:, RW_LORA_OFF:RW_LORA_OFF + D_WL + D_AL]
    xg = xs[:, RW_LORA_OFF + D_WL + D_AL:]
    w = jnp.exp(-RWKV_DECAY_SCALE * _sigmoid(w0_ref[...] + _dot(jnp.tanh(xwa).astype(BF16), wup_ref[...])))
    a = _sigmoid(a0_ref[...] + _dot(xwa.astype(BF16), aup_ref[...]))
    g_o[...] = _dot(_sigmoid(xg).astype(BF16), gup_ref[...])
    kk = k * kk_ref[...]
    kk = kk * lax.rsqrt(jnp.maximum(_dot_f32(kk * kk, e_ref[...]), 1e-12))
    k2 = k * (1.0 + (a - 1.0) * ka_ref[...])
    kka = kk * a
    nkk_o[...] = -kk
    wr_o[...] = w * r
    w_o[...] = w
    kka_o[...] = kka
    k2_o[...] = k2
    v_o[...] = v
    c1_o[...] = _dot_f32(kka * r, e_ref[...])
    c2_o[...] = _dot_f32(k2 * r, e_ref[...])
    bonus_o[...] = _dot_f32(r * k2 * rk_ref[...], e_ref[...]) * v


def rwkv_prep(proj, shifted, P, tm=256):
    n = proj.shape[0]
    nblk = EVEN_PR_OFF // RW_WIDTH
    assert EVEN_PR_OFF % LANES == 0
    zw = jnp.zeros((D_WL, HB), F32)
    consts = [P['mu'].reshape(1, -1), P['w0'].reshape(1, -1),
              jnp.concatenate([P['w_up'], zw]).astype(BF16), P['a0'].reshape(1, -1),
              jnp.concatenate([zw, P['a_up']]).astype(BF16), P['g_up'].astype(BF16),
              P['k_k'].reshape(1, -1), P['k_a'].reshape(1, -1), P['r_k'].reshape(1, -1),
              jnp.asarray(_blockdiag_ones(HB, HS_B), BF16)]
    pr = lax.slice_in_dim(proj, EVEN_PR_OFF, EVEN_W, axis=1)
    return pl.pallas_call(
        _rwkv_prep_body,
        grid=(n // tm,),
        in_specs=[pl.BlockSpec((tm, RW_WIDTH), lambda i: (i, 0)), pl.BlockSpec((tm, RW_WIDTH), lambda i: (i, 0))]
        + [pl.BlockSpec(c.shape, lambda i: (0, 0)) for c in consts],
        out_specs=[pl.BlockSpec((tm, HB), lambda i: (i, 0))] * 10,
        out_shape=[jax.ShapeDtypeStruct((n, HB), F32)] * 10,
        compiler_params=_cp("parallel"),
        name="rwkv_prep",
    )(pr, shifted, *consts)


def _rwkv_scan_body(nkk_r, wr_r, w_r, kka_r, k2_r, v_r, c1_r, c2_r, s0_ref, e_ref, d_ref, o_ref, sf_ref, s_scr,
                    *, nb, tc):
    c = pl.program_id(1)

    @pl.when(c == 0)
    def _():
        s_scr[...] = s0_ref[...]

    dmask = d_ref[...]
    half = HB // 2

    def step(t, carry):
        for b in range(nb):
            row = lambda ref: ref[b, pl.ds(t, 1), :]
            s = s_scr[b]
            stacked = jnp.concatenate([s * row(nkk_r), s * row(wr_r), dmask * row(v_r)], axis=0)
            hi, lo = _split2(stacked)
            seg = jnp.concatenate(
                [_dot(hi[:, i * half:(i + 1) * half], e_ref[...]) + _dot(lo[:, i * half:(i + 1) * half], e_ref[...])
                 for i in range(2)], axis=1)
            sa, t2, vb = seg[:HS_B], seg[HS_B:2 * HS_B], seg[2 * HS_B:]
            s_scr[b] = s * row(w_r) + sa * row(kka_r) + vb * row(k2_r)
            ob = t2 + sa * row(c1_r) + vb * row(c2_r)
            o_ref[b, pl.ds(t, 1), :] = jnp.sum(ob * dmask, axis=0, keepdims=True)
        return carry

    lax.fori_loop(0, tc, step, 0)

    @pl.when(c == pl.num_programs(1) - 1)
    def _():
        sf_ref[...] = s_scr[...]


def rwkv_scan(seqs, state0, b, t, nb, tc):
    s0 = jnp.transpose(state0, (0, 2, 1, 3)).reshape(b, HS_B, HB)
    e = jnp.asarray(_blockdiag_ones(HB // 2, HS_B), BF16)
    dmask = jnp.asarray((np.arange(HS_B)[:, None] == (np.arange(HB)[None, :] % HS_B)).astype(np.float32))
    seq_spec = pl.BlockSpec((nb, tc, HB), lambda i, c: (i, c, 0))
    st_spec = pl.BlockSpec((nb, HS_B, HB), lambda i, c: (i, 0, 0))
    o, sf = pl.pallas_call(
        functools.partial(_rwkv_scan_body, nb=nb, tc=tc),
        grid=(b // nb, t // tc),
        in_specs=[seq_spec] * 8 + [st_spec, pl.BlockSpec(e.shape, lambda i, c: (0, 0)),
                                   pl.BlockSpec(dmask.shape, lambda i, c: (0, 0))],
        out_specs=[seq_spec, st_spec],
        out_shape=[jax.ShapeDtypeStruct((b, t, HB), F32), jax.ShapeDtypeStruct((b, HS_B, HB), F32)],
        scratch_shapes=[pltpu.VMEM((nb, HS_B, HB), F32)],
        compiler_params=_cp("parallel", "arbitrary"),
        name="rwkv_scan",
    )(*[a.reshape(b, t, HB) for a in seqs], s0, e, dmask)
    sf = jnp.transpose(sf.reshape(b, HS_B, H_B, HS_B), (0, 2, 1, 3))
    return o.reshape(b * t, HB), sf


def _rwkv_post_body(o_ref, g_ref, bonus_ref, lng_ref, lnb_ref, e_ref, out_ref):
    o = o_ref[...]
    mu = _dot_f32(o, e_ref[...]) * (1.0 / HS_B)
    d = o - mu
    var = _dot_f32(d * d, e_ref[...]) * (1.0 / HS_B)
    y = d * lax.rsqrt(var + RWKV_LN_EPS) * lng_ref[...] + lnb_ref[...]
    out_ref[...] = ((y + bonus_ref[...]) * g_ref[...]).astype(BF16)


def rwkv_post(o, g, bonus, P, tm=256):
    n = o.shape[0]
    e = jnp.asarray(_blockdiag_ones(HB, HS_B), BF16)
    blk = pl.BlockSpec((tm, HB), lambda i: (i, 0))
    row = pl.BlockSpec((1, HB), lambda i: (0, 0))
    return pl.pallas_call(
        _rwkv_post_body,
        grid=(n // tm,),
        in_specs=[blk, blk, blk, row, row, pl.BlockSpec(e.shape, lambda i: (0, 0))],
        out_specs=blk,
        out_shape=jax.ShapeDtypeStruct((n, HB), BF16),
        compiler_params=_cp("parallel"),
        name="rwkv_post",
    )(o, g, bonus, P['ln_g'].reshape(1, -1), P['ln_b'].reshape(1, -1), e)


def rwkv_mix(proj, prev, state0, b, t, P, nb, tc, tm):
    pr = lax.slice_in_dim(proj, EVEN_PR_OFF, EVEN_W, axis=1).reshape(b, t, RW_WIDTH)
    shifted = jnp.concatenate([prev[:, None, :], pr[:, :-1, :]], axis=1).reshape(b * t, RW_WIDTH)
    outs = rwkv_prep(proj, shifted, P, tm)
    o, s_new = rwkv_scan(outs[:8], state0, b, t, nb, tc)
    return rwkv_post(o, outs[8], outs[9], P, tm), pr[:, -1, :], s_new


ODD_Z_OFF = 3 * HC
ODD_XBC_OFF = ODD_Z_OFF + D_INNER
ODD_DT_OFF = ODD_XBC_OFF + CONV_DIM
ODD_W = ODD_DT_OFF + LANES
SB_SCALE = HD_C ** -0.5


def _sb_weights(z, mask, suffix_of, run):
    lneg = -_softplus(z)
    if mask is not None:
        lneg = jnp.where(mask, lneg, 0.0)
    w = jnp.exp(z + lneg + suffix_of(lneg) + run)
    if mask is not None:
        w = jnp.where(mask, w, 0.0)
    return w, lneg


def _sb_prompt_body(q_ref, k_ref, v_ref, u_ref, o_ref, *, tq):
    qi = pl.program_id(1)
    row = lax.broadcasted_iota(I32, (tq, tq), 0)
    col = lax.broadcasted_iota(I32, (tq, tq), 1)
    for h in range(H_C):
        sl = slice(h * HD_C, (h + 1) * HD_C)
        q = q_ref[:, sl].astype(BF16)

        def body(i, carry):
            run, acc = carry
            kb = qi - i
            off = pl.multiple_of(kb * tq, tq)
            z = _dot_nt(q, k_ref[pl.ds(off, tq), sl].astype(BF16)) * SB_SCALE
            mask = col + kb * tq < row + qi * tq
            w, lneg = _sb_weights(z, mask, lambda l: _dot_f32(l, u_ref[...]), run)
            acc = acc + _dot(w.astype(BF16), v_ref[pl.ds(off, tq), sl].astype(BF16))
            return run + jnp.sum(lneg, axis=-1, keepdims=True), acc

        _, acc = lax.fori_loop(0, qi + 1, body, (jnp.zeros((tq, 1), F32), jnp.zeros((tq, HD_C), F32)))
        o_ref[:, sl] = acc.astype(BF16)


def sb_prompt_attention(proj, b, t, tq=256):
    nq = t // tq
    u = jnp.asarray(np.tril(np.ones((tq, tq), np.float32), -1), BF16)
    return pl.pallas_call(
        functools.partial(_sb_prompt_body, tq=tq),
        grid=(b, nq),
        in_specs=[pl.BlockSpec((tq, HC), lambda bi, qi: (bi * nq + qi, 0)),
                  pl.BlockSpec((t, HC), lambda bi, qi: (bi, 1)),
                  pl.BlockSpec((t, HC), lambda bi, qi: (bi, 2)),
                  pl.BlockSpec(u.shape, lambda bi, qi: (0, 0))],
        out_specs=pl.BlockSpec((tq, HC), lambda bi, qi: (bi * nq + qi, 0)),
        out_shape=jax.ShapeDtypeStruct((b * t, HC), BF16),
        compiler_params=_cp("parallel", "arbitrary"),
        name="sb_prompt_attention",
    )(proj, proj, proj, u)


def _sb_sample_body(pt_ref, *refs, t_new):
    npg = PAGES_PER_STEP
    k_pages, v_pages = refs[:npg], refs[npg:2 * npg]
    k_new, v_new, q_ref, u_ref, hm_ref, o_ref, run_scr, acc_scr = refs[2 * npg:]
    j = pl.program_id(1)
    nq = H_C * t_new

    def suffix_of(lneg):
        hi, lo = _split2(lneg)
        return _dot(u_ref[...], hi) + _dot(u_ref[...], lo)

    def process(k, v, mask):
        z = _dot(k.astype(BF16), q_ref[0]) * SB_SCALE
        w, lneg = _sb_weights(z, mask, suffix_of, run_scr[...])
        acc_scr[...] += _dot_tn(w.astype(BF16), v.astype(BF16))
        run_scr[...] += jnp.sum(lneg, axis=0, keepdims=True)

    @pl.when(j == 0)
    def _():
        run_scr[...] = jnp.zeros(run_scr.shape, F32)
        acc_scr[...] = jnp.zeros(acc_scr.shape, F32)
        key = lax.broadcasted_iota(I32, (PAGE_SIZE, nq), 0)
        qt = lax.broadcasted_iota(I32, (PAGE_SIZE, nq), 1) % t_new
        process(k_new[0], v_new[0], key < qt)

    for p in range(npg):
        process(k_pages[p][0], v_pages[p][0], None)

    @pl.when(j == KV_STEPS - 1)
    def _():
        full = acc_scr[...] * hm_ref[...]
        out = full[0:t_new]
        for h in range(1, H_C):
            out = out + full[h * t_new:(h + 1) * t_new]
        o_ref[0] = out.astype(BF16)


def sb_sample_attention(page_table, cache_k, cache_v, q, k_new, v_new):
    b, t_new = q.shape[:2]
    nq = H_C * t_new
    eye = jnp.eye(H_C, dtype=BF16)
    q_bd = jnp.einsum('bthd,hg->bhdgt', q.astype(BF16).reshape(b, t_new, H_C, HD_C), eye).reshape(b, HC, nq)
    pad = lambda a: jnp.pad(a, ((0, 0), (0, PAGE_SIZE - t_new), (0, 0)))
    hm = (np.arange(nq)[:, None] // t_new == np.arange(HC)[None, :] // HD_C).astype(np.float32)
    u = jnp.asarray(np.triu(np.ones((PAGE_SIZE, PAGE_SIZE), np.float32), 1), BF16)
    npg = PAGES_PER_STEP
    page = lambda p: pl.BlockSpec(
        (1, PAGE_SIZE, HC), lambda bi, j, pt: (pt[bi * N_PAGES + (N_PAGES - 1 - (j * npg + p))], 0, 0))
    per_b = lambda s: pl.BlockSpec((1,) + s, lambda bi, j, pt: (bi, 0, 0))
    grid_spec = pltpu.PrefetchScalarGridSpec(
        num_scalar_prefetch=1,
        grid=(b, KV_STEPS),
        in_specs=[page(p) for p in range(npg)] * 2
        + [per_b((PAGE_SIZE, HC)), per_b((PAGE_SIZE, HC)), per_b((HC, nq)),
           pl.BlockSpec(u.shape, lambda bi, j, pt: (0, 0)), pl.BlockSpec(hm.shape, lambda bi, j, pt: (0, 0))],
        out_specs=pl.BlockSpec((1, t_new, HC), lambda bi, j, pt: (bi, 0, 0)),
        scratch_shapes=[pltpu.VMEM((1, nq), F32), pltpu.VMEM((nq, HC), F32)],
    )
    out = pl.pallas_call(
        functools.partial(_sb_sample_body, t_new=t_new),
        grid_spec=grid_spec,
        out_shape=jax.ShapeDtypeStruct((b, t_new, HC), BF16),
        compiler_params=_cp("parallel", "arbitrary"),
        name="sb_sample_attention",
    )(page_table.reshape(-1), *([cache_k] * npg), *([cache_v] * npg), pad(k_new), pad(v_new), q_bd, u, jnp.asarray(hm))
    return out.reshape(b * t_new, HC)


TAIL = 8


def _ssd_body(z_ref, xbc_ref, dt_ref, cbuf_ref, h0_ref, cw_ref, cb_ref, dtb_ref, alog_ref, drow_ref, gn_ref, tri_ref,
              y_ref, hf_ref, h_scr, tail_scr, y_scr, *, L):
    c = pl.program_id(1)
    mm = (lambda x: x.astype(BF16)) if L >= 16 else (lambda x: x.astype(BF16).astype(F32))

    @pl.when(c == 0)
    def _():
        h_scr[...] = h0_ref[0]
        tail_scr[...] = jnp.zeros(tail_scr.shape, F32)
        tail_scr[TAIL - (CONV_W - 1):, :] = cbuf_ref[0]

    xbc = xbc_ref[...]
    ext = jnp.concatenate([tail_scr[...], xbc], axis=0)
    conv = cb_ref[...]
    for i in range(CONV_W):
        s = TAIL - (CONV_W - 1) + i
        conv = conv + cw_ref[i:i + 1, :] * ext[s:s + L]
    tail_scr[...] = xbc[L - TAIL:]
    xc = conv * _sigmoid(conv)
    dt = _softplus(dt_ref[...] + dtb_ref[...])
    a = -jnp.exp(alog_ref[...])
    lane = lax.broadcasted_iota(I32, (1, LANES), 1)
    dta = jnp.where(lane < H_D, dt * a, 0.0)
    hi = dta.astype(BF16)
    r1 = dta - hi.astype(F32)
    mid = r1.astype(BF16)
    lo = (r1 - mid.astype(F32)).astype(BF16)
    tri = tri_ref[...]
    acs = (_dot(tri, mm(hi)) + _dot(tri, mm(mid))) + _dot(tri, mm(lo))
    hi, mid, lo = acs.astype(BF16), None, None
    r1 = acs - hi.astype(F32)
    mid = r1.astype(BF16)
    lo = (r1 - mid.astype(F32)).astype(BF16)
    row = lax.broadcasted_iota(I32, (L, L), 0)
    col = lax.broadcasted_iota(I32, (L, L), 1)
    lane_l = lax.broadcasted_iota(I32, (L, LANES), 1)
    acs_last = acs[L - 1:L, :]
    cbs = []
    for g in range(G_D):
        bg = xc[:, D_INNER + g * N_D:D_INNER + (g + 1) * N_D]
        cg = xc[:, D_INNER + G_D * N_D + g * N_D:D_INNER + G_D * N_D + (g + 1) * N_D]
        cbs.append((bg, cg, _dot_nt(mm(cg), mm(bg))))
    for h in range(H_D):
        bg, cg, cb = cbs[h // (H_D // G_D)]
        sel = mm((lane_l == h).astype(F32))
        acs_row = (_dot_nt(sel, mm(hi)) + _dot_nt(sel, mm(mid))) + _dot_nt(sel, mm(lo))
        acs_col = acs[:, h:h + 1]
        lmat = jnp.exp(jnp.where(col <= row, acs_col - acs_row, -jnp.inf))
        xh = xc[:, h * P_D:(h + 1) * P_D]
        xdt = xh * dt[:, h:h + 1]
        hprev = h_scr[h]
        y = _dot(mm(cb * lmat), mm(xdt)) + _dot_nt(mm(cg * jnp.exp(acs_col)), mm(hprev))
        y_scr[:, h * P_D:(h + 1) * P_D] = y + drow_ref[:, h * P_D:(h + 1) * P_D] * xh
        last = acs_last[:, h:h + 1]
        h_scr[h] = hprev * jnp.exp(last) + _dot_tn(mm(xdt), mm(bg * jnp.exp(last - acs_col)))
    z = z_ref[...]
    y_ref[...] = _rms(y_scr[...] * (z * _sigmoid(z)), gn_ref[...]).astype(BF16)

    @pl.when(c == pl.num_programs(1) - 1)
    def _():
        hf_ref[0] = h_scr[...]


def ssd_mix(proj, conv_buf, state0, b, t, P):
    L = SSD_CHUNK if t % SSD_CHUNK == 0 else t
    assert L % TAIL == 0
    nc = t // L
    pad_h = lambda v: jnp.pad(v.reshape(1, -1), ((0, 0), (0, LANES - H_D)))
    consts = [P['conv_w'], P['conv_b'].reshape(1, -1), pad_h(P['dt_bias']), pad_h(P['a_log']),
              jnp.repeat(P['d'], P_D).reshape(1, -1), P['g_norm'].reshape(1, -1)]
    tri_np = np.tril(np.ones((L, L), np.float32))
    tri = jnp.asarray(tri_np, BF16 if L >= 16 else F32)
    zc, xc, dc = ODD_Z_OFF // D_INNER, ODD_XBC_OFF // CONV_DIM, ODD_DT_OFF // LANES
    assert ODD_Z_OFF % D_INNER == 0 and ODD_XBC_OFF % CONV_DIM == 0 and ODD_DT_OFF % LANES == 0
    y, hf = pl.pallas_call(
        functools.partial(_ssd_body, L=L),
        grid=(b, nc),
        in_specs=[pl.BlockSpec((L, D_INNER), lambda bi, c: (bi * nc + c, zc)),
                  pl.BlockSpec((L, CONV_DIM), lambda bi, c: (bi * nc + c, xc)),
                  pl.BlockSpec((L, LANES), lambda bi, c: (bi * nc + c, dc)),
                  pl.BlockSpec((1, CONV_W - 1, CONV_DIM), lambda bi, c: (bi, 0, 0)),
                  pl.BlockSpec((1, H_D, P_D, N_D), lambda bi, c: (bi, 0, 0, 0))]
        + [pl.BlockSpec(cst.shape, lambda bi, c: (0, 0)) for cst in consts]
        + [pl.BlockSpec(tri.shape, lambda bi, c: (0, 0))],
        out_specs=[pl.BlockSpec((L, D_INNER), lambda bi, c: (bi * nc + c, 0)),
                   pl.BlockSpec((1, H_D, P_D, N_D), lambda bi, c: (bi, 0, 0, 0))],
        out_shape=[jax.ShapeDtypeStruct((b * t, D_INNER), BF16), jax.ShapeDtypeStruct((b, H_D, P_D, N_D), F32)],
        scratch_shapes=[pltpu.VMEM((H_D, P_D, N_D), F32), pltpu.VMEM((TAIL, CONV_DIM), F32),
                        pltpu.VMEM((L, D_INNER), F32)],
        compiler_params=_cp("parallel", "arbitrary"),
        name="ssd_mix",
    )(proj, proj, proj, conv_buf, state0, *consts, tri)
    return y, hf


SC_L = 16
SC_TB = 16
GELU_C = math.sqrt(2.0 / math.pi)


def _sc_gelu(a):
    y = GELU_C * (a + 0.044715 * (a * a * a))
    return 0.5 * a * (2.0 - 2.0 / (jnp.exp(2.0 * y) + 1.0))


def peer_experts(hn, idx, gate, u_tab, v_tab, x):
    n = hn.shape[0]
    info = plsc.get_sparse_core_info()
    nc, ns = info.num_cores, info.num_subcores
    assert info.num_lanes == SC_L and PEER_TOPK == SC_L
    nw = nc * ns
    per_w = n // nw
    assert n % (nw * SC_TB) == 0
    n_blk = per_w // SC_TB
    n_chunk = SC_TB * PEER_HEADS
    n_vec = D_MODEL // SC_L
    mesh = plsc.VectorSubcoreMesh(core_axis_name="c", subcore_axis_name="s")

    @functools.partial(
        pl.kernel, mesh=mesh, out_type=jax.ShapeDtypeStruct((n, D_MODEL), F32),
        scratch_types=[pltpu.VMEM((SC_TB, D_MODEL), F32), pltpu.VMEM((SC_TB, D_MODEL), F32),
                       pltpu.VMEM((SC_TB, PEER_HEADS, SC_L), I32), pltpu.VMEM((SC_TB, PEER_HEADS * SC_L), F32),
                       pltpu.VMEM((2, SC_L, D_MODEL), F32), pltpu.VMEM((2, SC_L, D_MODEL), F32),
                       pltpu.SemaphoreType.DMA((2,)), pltpu.SemaphoreType.DMA((2,))],
        compiler_params=pltpu.CompilerParams(needs_layout_passes=False),
        name="peer_experts")
    def run(hn_hbm, idx_hbm, gate_hbm, u_hbm, v_hbm, x_hbm, out_hbm, hnv, outv, idxv, gv, ubuf, vbuf, usem, vsem):
        wid = lax.axis_index("s") * nc + lax.axis_index("c")
        lanes = lax.iota(I32, SC_L)

        def copies(q, slot):
            tl, hd = q // PEER_HEADS, q % PEER_HEADS
            rows = idxv.at[tl, hd]
            return (pltpu.make_async_copy(u_hbm.at[rows], ubuf.at[slot], usem.at[slot]),
                    pltpu.make_async_copy(v_hbm.at[rows], vbuf.at[slot], vsem.at[slot]))

        def start(q, slot):
            for cp in copies(q, slot):
                cp.start()

        def compute(q, slot):
            tl, hd = q // PEER_HEADS, q % PEER_HEADS
            cu, cv = copies(q, slot)
            cu.wait()

            def ubody(c, accs):
                off = pl.multiple_of(c * SC_L, SC_L)
                xc = hnv[tl, pl.ds(off, SC_L)]
                return tuple(accs[j] + ubuf[slot, j, pl.ds(off, SC_L)] * xc for j in range(SC_L))

            zero = jnp.zeros((SC_L,), F32)
            accs = lax.fori_loop(0, n_vec, ubody, (zero,) * SC_L)
            acts = zero
            for j in range(SC_L):
                acts = jnp.where(lanes == j, jnp.sum(accs[j]), acts)
            coef = gv[tl, pl.ds(pl.multiple_of(hd * SC_L, SC_L), SC_L)] * _sc_gelu(acts)
            cj = [jnp.sum(jnp.where(lanes == j, coef, 0.0)) for j in range(SC_L)]
            cv.wait()

            def vbody(c, carry):
                off = pl.multiple_of(c * SC_L, SC_L)
                o = outv[tl, pl.ds(off, SC_L)]
                for j in range(SC_L):
                    o = o + cj[j] * vbuf[slot, j, pl.ds(off, SC_L)]
                outv[tl, pl.ds(off, SC_L)] = o
                return carry

            lax.fori_loop(0, n_vec, vbody, 0)

        def block(bi, carry):
            t0 = pl.multiple_of(wid * per_w + bi * SC_TB, SC_TB)
            pltpu.sync_copy(hn_hbm.at[pl.ds(t0, SC_TB)], hnv)
            pltpu.sync_copy(x_hbm.at[pl.ds(t0, SC_TB)], outv)
            pltpu.sync_copy(idx_hbm.at[pl.ds(t0, SC_TB)], idxv)
            pltpu.sync_copy(gate_hbm.at[pl.ds(t0, SC_TB)], gv)
            start(0, 0)

            def pair(p, c2):
                q = 2 * p
                start(q + 1, 1)
                compute(q, 0)

                @pl.when(q + 2 < n_chunk)
                def _():
                    start(q + 2, 0)

                compute(q + 1, 1)
                return c2

            lax.fori_loop(0, n_chunk // 2, pair, 0)
            pltpu.sync_copy(outv, out_hbm.at[pl.ds(t0, SC_TB)])
            return carry

        lax.fori_loop(0, n_blk, block, 0)

    return run(hn, idx.reshape(n, PEER_HEADS, SC_L), gate, u_tab, v_tab, x)


def peer_ffn(x, g, w_q, sub_keys, u_tab, v_tab):
    hn, idx, gate = peer_route(x, g, w_q, sub_keys)
    return peer_experts(hn, idx, gate, u_tab, v_tab, x)


def _even_w_in(w):
    out = jnp.zeros((D_MODEL, EVEN_W), F32)
    out = out.at[:, :EVEN_KR_OFF].set(w[:, :D_CQ + D_C])
    out = out.at[:, EVEN_KR_OFF + DN_A:EVEN_KR_OFF + DN_A + DR_A].set(w[:, D_CQ + D_C:D_CQ + D_C + DR_A])
    out = out.at[:, EVEN_PR_OFF:].set(w[:, D_CQ + D_C + DR_A:])
    return out.astype(BF16)


def _even_layer(xp, xs, bp, tp, bs, ts, pos_p, pos_s, page_table, cache_ckv, cache_kr, rw_shift, rw_wkv, g_mix, P):
    w_in = _even_w_in(P['w_in'])
    w_out = P['w_out'].astype(BF16)
    w_oa, w_ob = w_out[:H_A * DV_A], w_out[H_A * DV_A:]
    proj = norm_matmul(xp, g_mix, w_in)
    qc, kc, vc, ckv_p, kr_p = mla_prep(proj, pos_p, P)
    o_a = mla_prompt_attention(qc, kc, vc, bp, tp)
    o_b, sh_p, wkv_p = rwkv_mix(proj, jnp.zeros((bp, RW_WIDTH), F32), jnp.zeros((bp, H_B, HS_B, HS_B), F32),
                                bp, tp, P, nb=4, tc=64, tm=256)
    xp = matmul_res([o_a, o_b], [w_oa, w_ob], xp)
    proj = norm_matmul(xs, g_mix, w_in)
    qc, _, _, ckv_s, kr_s = mla_prep(proj, pos_s, P)
    qc = qc.reshape(bs, ts, H_A, HG)
    kr_s = kr_s[:, DN_A:DN_A + DR_A]
    o_a = mla_sample_attention(page_table, cache_ckv, cache_kr, ckv_s.reshape(bs, ts, D_C), kr_s.reshape(bs, ts, DR_A),
                               qc[..., :DN_A], qc[..., DN_A:DN_A + DR_A], P)
    o_b, sh_s, wkv_s = rwkv_mix(proj, rw_shift, rw_wkv, bs, ts, P, nb=8, tc=ts, tm=256)
    xs = matmul_res([o_a, o_b], [w_oa, w_ob], xs)
    state_p = (ckv_p.reshape(bp, tp, D_C), kr_p[:, DN_A:DN_A + DR_A].reshape(bp, tp, DR_A), sh_p, wkv_p)
    state_s = (ckv_s.reshape(bs, ts, D_C), kr_s.reshape(bs, ts, DR_A), sh_s, wkv_s)
    return xp, xs, state_p, state_s


def _odd_layer(xp, xs, bp, tp, bs, ts, page_table, cache_k, cache_v, conv_state, ssm_state, g_mix, P):
    w_in = jnp.pad(P['w_in'], ((0, 0), (0, ODD_W - P['w_in'].shape[1]))).astype(BF16)
    w_out = P['w_out'].astype(BF16)
    w_oc, w_od = w_out[:HC], w_out[HC:]
    n_pool = cache_k.shape[0]

    def states(proj, b, t):
        p3 = proj.reshape(b, t, ODD_W)
        return (p3[:, :, HC:2 * HC].reshape(b, t, H_C, HD_C), p3[:, :, 2 * HC:3 * HC].reshape(b, t, H_C, HD_C),
                p3[:, t - (CONV_W - 1):, ODD_XBC_OFF:ODD_XBC_OFF + CONV_DIM])

    proj = norm_matmul(xp, g_mix, w_in)
    o_c = sb_prompt_attention(proj, bp, tp)
    y, ssm_p = ssd_mix(proj, jnp.zeros((bp, CONV_W - 1, CONV_DIM), F32), jnp.zeros((bp, H_D, P_D, N_D), F32), bp, tp, P)
    xp = matmul_res([o_c, y], [w_oc, w_od], xp)
    sbk_p, sbv_p, conv_p = states(proj, bp, tp)
    proj = norm_matmul(xs, g_mix, w_in)
    p3 = proj.reshape(bs, ts, ODD_W)
    o_c = sb_sample_attention(page_table, cache_k.reshape(n_pool, PAGE_SIZE, HC), cache_v.reshape(n_pool, PAGE_SIZE, HC),
                              p3[:, :, :HC], p3[:, :, HC:2 * HC], p3[:, :, 2 * HC:3 * HC])
    y, ssm_s = ssd_mix(proj, conv_state, ssm_state, bs, ts, P)
    xs = matmul_res([o_c, y], [w_oc, w_od], xs)
    sbk_s, sbv_s, conv_s = states(proj, bs, ts)
    return xp, xs, (sbk_p, sbv_p, conv_p, ssm_p), (sbk_s, sbv_s, conv_s, ssm_s)


def kernel(x_prompt, x_sample, mem_prompt, page_table, cache_mla_ckv, cache_mla_krope, state_rwkv_shift, state_rwkv_wkv, cache_sb_k, cache_sb_v, state_ssm_conv, state_ssm, cache_mem_k, cache_mem_v, norm_mix, norm_mem, norm_ffn, w_in_even, w_out_even, mla_g_cq, mla_w_uq, mla_g_ckv, mla_w_uk, mla_w_uv, mla_g_qn, mla_g_kn, mla_g_qr, mla_g_kr, rw_mu, rw_w0, rw_w_up, rw_a0, rw_a_up, rw_g_up, rw_k_k, rw_k_a, rw_r_k, rw_ln_g, rw_ln_b, w_in_odd, w_out_odd, ssm_conv_w, ssm_conv_b, ssm_dt_bias, ssm_a_log, ssm_d, ssm_g_norm, mem_g_src, mem_w_q, mem_w_kv, mem_g_q, mem_g_k, mem_w_o, peer_w_q, peer_sub_keys, peer_u, peer_v):
    bp, tp = x_prompt.shape[:2]
    bs, ts = x_sample.shape[:2]
    depth = norm_mix.shape[0]
    xp = x_prompt.reshape(bp * tp, D_MODEL)
    xs = x_sample.reshape(bs * ts, D_MODEL)
    pos_p = jnp.tile(jnp.arange(tp, dtype=I32), bp)
    pos_s = jnp.tile(PAST_LEN + jnp.arange(ts, dtype=I32), bs)
    even_p, even_s, odd_p, odd_s, mem_k, mem_v = [], [], [], [], [], []
    for layer in range(depth):
        i = layer // 2
        if layer % 2 == 0:
            P = dict(w_in=w_in_even[i], w_out=w_out_even[i], g_cq=mla_g_cq[i], w_uq=mla_w_uq[i],
                     g_ckv=mla_g_ckv[i], w_uk=mla_w_uk[i], w_uv=mla_w_uv[i], g_qn=mla_g_qn[i],
                     g_kn=mla_g_kn[i], g_qr=mla_g_qr[i], g_kr=mla_g_kr[i],
                     mu=rw_mu[i], w0=rw_w0[i], w_up=rw_w_up[i], a0=rw_a0[i], a_up=rw_a_up[i],
                     g_up=rw_g_up[i], k_k=rw_k_k[i], k_a=rw_k_a[i], r_k=rw_r_k[i],
                     ln_g=rw_ln_g[i], ln_b=rw_ln_b[i])
            xp, xs, sp, ss = _even_layer(xp, xs, bp, tp, bs, ts, pos_p, pos_s, page_table, cache_mla_ckv[i],
                                         cache_mla_krope[i], state_rwkv_shift[i], state_rwkv_wkv[i], norm_mix[layer], P)
            even_p.append(sp)
            even_s.append(ss)
        else:
            P = dict(w_in=w_in_odd[i], w_out=w_out_odd[i], conv_w=ssm_conv_w[i], conv_b=ssm_conv_b[i],
                     dt_bias=ssm_dt_bias[i], a_log=ssm_a_log[i], d=ssm_d[i], g_norm=ssm_g_norm[i])
            xp, xs, sp, ss = _odd_layer(xp, xs, bp, tp, bs, ts, page_table, cache_sb_k[i], cache_sb_v[i],
                                        state_ssm_conv[i], state_ssm[i], norm_mix[layer], P)
            odd_p.append(sp)
            odd_s.append(ss)
        mk, mv = memory_kv(mem_prompt.reshape(bp * N_MEM, D_MODEL), mem_g_src[layer], mem_w_kv[layer], mem_g_k[layer])
        mem_k.append(mk.reshape(bp, N_MEM, MEM_HEADS, MEM_HD))
        mem_v.append(mv.reshape(bp, N_MEM, MEM_HEADS, MEM_HD))
        xp = memory_attend(xp, tp, norm_mem[layer], mem_w_q[layer], mem_g_q[layer], mk.reshape(bp, N_MEM, MEM_W),
                           mv.reshape(bp, N_MEM, MEM_W), mem_w_o[layer], tq=256)
        xs = memory_attend(xs, ts, norm_mem[layer], mem_w_q[layer], mem_g_q[layer],
                           cache_mem_k[layer].reshape(bs, N_MEM, MEM_W), cache_mem_v[layer].reshape(bs, N_MEM, MEM_W),
                           mem_w_o[layer], tq=ts)
        xp = peer_ffn(xp, norm_ffn[layer], peer_w_q[layer], peer_sub_keys[layer], peer_u[layer], peer_v[layer])
        xs = peer_ffn(xs, norm_ffn[layer], peer_w_q[layer], peer_sub_keys[layer], peer_u[layer], peer_v[layer])
    stack = lambda groups, k: jnp.stack([g[k] for g in groups])
    return (xp.reshape(bp, tp, D_MODEL), xs.reshape(bs, ts, D_MODEL),
            stack(even_p, 0), stack(even_p, 1), stack(even_p, 2), stack(even_p, 3),
            stack(odd_p, 0), stack(odd_p, 1), stack(odd_p, 2), stack(odd_p, 3),
            jnp.stack(mem_k), jnp.stack(mem_v),
            stack(even_s, 0), stack(even_s, 1), stack(even_s, 2), stack(even_s, 3),
            stack(odd_s, 0), stack(odd_s, 1), stack(odd_s, 2), stack(odd_s, 3))
```

```python
import functools
import math

import numpy as np
import jax
import jax.numpy as jnp
from jax import lax
from jax.experimental import pallas as pl
from jax.experimental.pallas import tpu as pltpu
from jax.experimental.pallas import tpu_sc as plsc

F32 = jnp.float32
BF16 = jnp.bfloat16
I32 = jnp.int32

D_MODEL = 1024
EPS = 1e-6
PAST_LEN = 8192
PAGE_SIZE = 128
N_PAGES = PAST_LEN // PAGE_SIZE

H_A, DN_A, DR_A, DV_A, D_CQ, D_C = 8, 64, 32, 64, 384, 256
ROPE_BASE = 10000.0
MLA_SCALE = (DN_A + DR_A) ** -0.5
H_B, HS_B, D_WL, D_AL, D_GL = 8, 64, 64, 64, 128
HB = H_B * HS_B
RW_WIDTH = 3 * HB + D_WL + D_AL + D_GL
RWKV_DECAY_SCALE = 0.606531
RWKV_LN_EPS = 64e-5
H_C, HD_C = 8, 64
HC = H_C * HD_C
H_D, P_D, N_D, G_D, CONV_W = 8, 64, 128, 2, 4
D_INNER = H_D * P_D
CONV_DIM = D_INNER + 2 * G_D * N_D
SSD_CHUNK = 128
N_MEM, MEM_HEADS, MEM_HD = 256, 4, 64
MEM_W = MEM_HEADS * MEM_HD
PEER_HEADS, N_KEYS, PEER_DQ, PEER_TOPK = 8, 128, 256, 16
LANES = 128
VMEM_LIMIT = 56 * 1024 * 1024


def _cp(*sem):
    return pltpu.CompilerParams(dimension_semantics=sem, vmem_limit_bytes=VMEM_LIMIT)


def _dot(a, b):
    return jnp.dot(a, b, preferred_element_type=F32)


def _dot_nt(a, b):
    return lax.dot_general(a, b, (((1,), (1,)), ((), ())), preferred_element_type=F32)


def _dot_tn(a, b):
    return lax.dot_general(a, b, (((0,), (0,)), ((), ())), preferred_element_type=F32)


def _split2(x):
    hi = x.astype(BF16)
    lo = (x - hi.astype(F32)).astype(BF16)
    return hi, lo


def _dot_f32(x, m):
    hi, lo = _split2(x)
    return _dot(hi, m) + _dot(lo, m)


def _dot_sel(x, m):
    hi = x.astype(BF16)
    r1 = x - hi.astype(F32)
    mid = r1.astype(BF16)
    lo = (r1 - mid.astype(F32)).astype(BF16)
    return (_dot(hi, m) + _dot(mid, m)) + _dot(lo, m)


def _rms(x, g):
    return x * lax.rsqrt(jnp.mean(x * x, axis=-1, keepdims=True) + EPS) * g


def _sigmoid(x):
    return 1.0 / (1.0 + jnp.exp(-x))


def _softplus(x):
    return jnp.maximum(x, 0.0) + jnp.log(1.0 + jnp.exp(-jnp.abs(x)))


def _blockdiag_ones(n, seg):
    i = np.arange(n)
    return (i[:, None] // seg == i[None, :] // seg).astype(np.float32)


def _nm_body(x_ref, g_ref, w_ref, o_ref):
    h = _rms(x_ref[...], g_ref[...])
    o_ref[...] = _dot(h.astype(BF16), w_ref[...])


def norm_matmul(x, g, w, tm=256):
    n, k = x.shape
    m = w.shape[1]
    return pl.pallas_call(
        _nm_body,
        grid=(n // tm,),
        in_specs=[pl.BlockSpec((tm, k), lambda i: (i, 0)),
                  pl.BlockSpec((1, k), lambda i: (0, 0)),
                  pl.BlockSpec((k, m), lambda i: (0, 0))],
        out_specs=pl.BlockSpec((tm, m), lambda i: (i, 0)),
        out_shape=jax.ShapeDtypeStruct((n, m), F32),
        compiler_params=_cp("parallel"),
        name="norm_matmul",
    )(x, g.reshape(1, k), w)


def _mr_body(*refs, n_in):
    a_refs, w_refs, r_ref, o_ref = refs[:n_in], refs[n_in:2 * n_in], refs[2 * n_in], refs[2 * n_in + 1]
    acc = r_ref[...]
    for a_ref, w_ref in zip(a_refs, w_refs):
        acc = acc + _dot(a_ref[...].astype(BF16), w_ref[...])
    o_ref[...] = acc


def matmul_res(a_list, w_list, res, tm=256):
    n, m = res.shape
    n_in = len(a_list)
    in_specs = ([pl.BlockSpec((tm, a.shape[1]), lambda i: (i, 0)) for a in a_list]
                + [pl.BlockSpec(w.shape, lambda i: (0, 0)) for w in w_list]
                + [pl.BlockSpec((tm, m), lambda i: (i, 0))])
    return pl.pallas_call(
        functools.partial(_mr_body, n_in=n_in),
        grid=(n // tm,),
        in_specs=in_specs,
        out_specs=pl.BlockSpec((tm, m), lambda i: (i, 0)),
        out_shape=jax.ShapeDtypeStruct((n, m), F32),
        compiler_params=_cp("parallel"),
        name="matmul_res",
    )(*a_list, *w_list, res)


def _memkv_body(m_ref, g_ref, w_ref, gk_ref, e_ref, k_ref, v_ref):
    h = _rms(m_ref[...], g_ref[...])
    kv = _dot(h.astype(BF16), w_ref[...])
    k = kv[:, :MEM_W]
    ms = _dot_f32(k * k, e_ref[...]) * (1.0 / MEM_HD)
    k_ref[...] = k * lax.rsqrt(ms + EPS) * gk_ref[...]
    v_ref[...] = kv[:, MEM_W:]


def memory_kv(mem2d, g_src, w_kv, g_k, tm=256):
    n = mem2d.shape[0]
    e = jnp.asarray(_blockdiag_ones(MEM_W, MEM_HD), BF16)
    gk = jnp.tile(g_k, MEM_HEADS).reshape(1, MEM_W)
    return pl.pallas_call(
        _memkv_body,
        grid=(n // tm,),
        in_specs=[pl.BlockSpec((tm, D_MODEL), lambda i: (i, 0)),
                  pl.BlockSpec((1, D_MODEL), lambda i: (0, 0)),
                  pl.BlockSpec((D_MODEL, 2 * MEM_W), lambda i: (0, 0)),
                  pl.BlockSpec((1, MEM_W), lambda i: (0, 0)),
                  pl.BlockSpec((MEM_W, MEM_W), lambda i: (0, 0))],
        out_specs=[pl.BlockSpec((tm, MEM_W), lambda i: (i, 0))] * 2,
        out_shape=[jax.ShapeDtypeStruct((n, MEM_W), F32)] * 2,
        compiler_params=_cp("parallel"),
        name="memory_kv",
    )(mem2d, g_src.reshape(1, -1), w_kv.astype(BF16), gk, e)


def _memattn_body(x_ref, g_ref, wq_ref, gq_ref, e_ref, k_ref, v_ref, wo_ref, o_ref, att_ref):
    x = x_ref[...]
    h = _rms(x, g_ref[...])
    q = _dot(h.astype(BF16), wq_ref[...])
    ms = _dot_f32(q * q, e_ref[...]) * (1.0 / MEM_HD)
    q = (q * lax.rsqrt(ms + EPS) * gq_ref[...]).astype(BF16)
    k = k_ref[0].astype(BF16)
    v = v_ref[0].astype(BF16)
    for hd in range(MEM_HEADS):
        sl = slice(hd * MEM_HD, (hd + 1) * MEM_HD)
        s = _dot_nt(q[:, sl], k[:, sl]) * (MEM_HD ** -0.5)
        p = jnp.exp(s - jnp.max(s, axis=-1, keepdims=True))
        p = p / jnp.sum(p, axis=-1, keepdims=True)
        att_ref[:, sl] = _dot(p.astype(BF16), v[:, sl])
    o_ref[...] = x + _dot(att_ref[...].astype(BF16), wo_ref[...])


def memory_attend(x, seq_t, g_norm, w_q, g_q, k, v, w_o, tq):
    n = x.shape[0]
    per_b = seq_t // tq
    e = jnp.asarray(_blockdiag_ones(MEM_W, MEM_HD), BF16)
    gq = jnp.tile(g_q, MEM_HEADS).reshape(1, MEM_W)
    return pl.pallas_call(
        _memattn_body,
        grid=(n // tq,),
        in_specs=[pl.BlockSpec((tq, D_MODEL), lambda i: (i, 0)),
                  pl.BlockSpec((1, D_MODEL), lambda i: (0, 0)),
                  pl.BlockSpec((D_MODEL, MEM_W), lambda i: (0, 0)),
                  pl.BlockSpec((1, MEM_W), lambda i: (0, 0)),
                  pl.BlockSpec((MEM_W, MEM_W), lambda i: (0, 0)),
                  pl.BlockSpec((1, N_MEM, MEM_W), lambda i: (i // per_b, 0, 0)),
                  pl.BlockSpec((1, N_MEM, MEM_W), lambda i: (i // per_b, 0, 0)),
                  pl.BlockSpec((MEM_W, D_MODEL), lambda i: (0, 0))],
        out_specs=pl.BlockSpec((tq, D_MODEL), lambda i: (i, 0)),
        out_shape=jax.ShapeDtypeStruct((n, D_MODEL), F32),
        scratch_shapes=[pltpu.VMEM((tq, MEM_W), F32)],
        compiler_params=_cp("parallel"),
        name="memory_attend",
    )(x, g_norm.reshape(1, -1), w_q.astype(BF16), gq, e, k, v, w_o.astype(BF16))


def _topk_rows(s, k, payload=None):
    rows, cols = s.shape
    ridx = lax.broadcasted_iota(I32, (rows, cols), 0)
    orow = lax.broadcasted_iota(I32, (k, cols), 0)

    def body(r, carry):
        s, acc_v, acc_i = carry
        m = jnp.max(s, axis=0, keepdims=True)
        first = jnp.min(jnp.where(s == m, ridx, rows), axis=0, keepdims=True)
        hit = ridx == first
        if payload is None:
            pick = first.astype(F32)
        else:
            pick = jnp.max(jnp.where(hit, payload, -1.0), axis=0, keepdims=True)
        here = orow == r
        return jnp.where(hit, -jnp.inf, s), jnp.where(here, m, acc_v), jnp.where(here, pick, acc_i)

    z = jnp.zeros((k, cols), F32)
    _, acc_v, acc_i = lax.fori_loop(0, k, body, (s, z, z))
    return acc_v, acc_i


def _route_body(x_ref, g_ref, wq_ref, keys_ref, hn_ref, idx_ref, gate_ref):
    h = _rms(x_ref[...], g_ref[...])
    hn_ref[...] = h
    q = _dot(h.astype(BF16), wq_ref[...])
    tops = []
    for c in range(2):
        s = _dot_nt(keys_ref[0, c], q[:, c * LANES:(c + 1) * LANES].astype(BF16))
        tops.append(_topk_rows(s, PEER_TOPK))
    (s0, i0), (s1, i1) = tops
    cand_s = jnp.concatenate([s0[a:a + 1, :] + s1 for a in range(PEER_TOPK)], axis=0)
    cand_i = jnp.concatenate([i0[a:a + 1, :] * float(N_KEYS) + i1 for a in range(PEER_TOPK)], axis=0)
    best_s, best_i = _topk_rows(cand_s, PEER_TOPK, payload=cand_i)
    e = jnp.exp(best_s - best_s[0:1, :])
    idx_ref[0] = best_i.astype(I32)
    gate_ref[0] = e / jnp.sum(e, axis=0, keepdims=True)


def peer_route(x, g, w_q, sub_keys, tm=128):
    n = x.shape[0]
    keys = sub_keys.astype(BF16)
    hn, idx, gate = pl.pallas_call(
        _route_body,
        grid=(n // tm, PEER_HEADS),
        in_specs=[pl.BlockSpec((tm, D_MODEL), lambda i, hd: (i, 0)),
                  pl.BlockSpec((1, D_MODEL), lambda i, hd: (0, 0)),
                  pl.BlockSpec((D_MODEL, PEER_DQ), lambda i, hd: (0, hd)),
                  pl.BlockSpec((1, 2, N_KEYS, PEER_DQ // 2), lambda i, hd: (hd, 0, 0, 0))],
        out_specs=[pl.BlockSpec((tm, D_MODEL), lambda i, hd: (i, 0)),
                   pl.BlockSpec((1, PEER_TOPK, tm), lambda i, hd: (hd, 0, i)),
                   pl.BlockSpec((1, PEER_TOPK, tm), lambda i, hd: (hd, 0, i))],
        out_shape=[jax.ShapeDtypeStruct((n, D_MODEL), F32),
                   jax.ShapeDtypeStruct((PEER_HEADS, PEER_TOPK, n), I32),
                   jax.ShapeDtypeStruct((PEER_HEADS, PEER_TOPK, n), F32)],
        compiler_params=_cp("parallel", "arbitrary"),
        name="peer_route",
    )(x, g.reshape(1, -1), w_q.astype(BF16), keys)
    idx = jnp.transpose(idx, (2, 0, 1)).reshape(n, PEER_HEADS * PEER_TOPK)
    gate = jnp.transpose(gate, (2, 0, 1)).reshape(n, PEER_HEADS * PEER_TOPK)
    return hn, idx, gate


HG = LANES
QW = H_A * HG
EVEN_CKV_OFF = D_CQ
EVEN_KR_OFF = D_CQ + D_C
EVEN_PR_OFF = EVEN_KR_OFF + HG
EVEN_W = EVEN_PR_OFF + RW_WIDTH


def _mla_prep_body(p_ref, cs_ref, sn_ref, gcq_ref, wuq_ref, gq_ref, gckv_ref, gkr_ref, wuk_ref, gk_ref, wuv_ref,
                   eq_ref, ek_ref, perm_ref, q_out, k_out, v_out, ckv_out, kr_out):
    cs, sn = cs_ref[...], sn_ref[...]

    def rope(x):
        return x * cs + _dot_sel(x, perm_ref[...]) * sn

    q = _dot(_rms(p_ref[:, :D_CQ], gcq_ref[...]).astype(BF16), wuq_ref[...])
    ckv = _rms(p_ref[:, EVEN_CKV_OFF:EVEN_CKV_OFF + D_C], gckv_ref[...])
    ckv_out[...] = ckv
    krr = p_ref[:, EVEN_KR_OFF:EVEN_KR_OFF + HG]
    ms = jnp.sum(krr * krr, axis=-1, keepdims=True) * (1.0 / DR_A)
    kr = rope(krr * lax.rsqrt(ms + EPS) * gkr_ref[...])
    kr_out[...] = kr
    ckv_b = ckv.astype(BF16)
    kraw = _dot(ckv_b, wuk_ref[...])
    v_out[...] = _dot(ckv_b, wuv_ref[...]).astype(BF16)
    for h in range(H_A):
        sl = slice(h * HG, (h + 1) * HG)
        qh = q[:, sl]
        qh = qh * lax.rsqrt(_dot_f32(qh * qh, eq_ref[...]) + EPS) * gq_ref[:, sl]
        q_out[:, sl] = rope(qh).astype(BF16)
        kh = kraw[:, sl]
        kh = kh * lax.rsqrt(_dot_f32(kh * kh, ek_ref[...]) + EPS) * gk_ref[:, sl]
        k_out[:, sl] = (kh + kr).astype(BF16)


def _head_groups(w, width, off=0):
    k = w.shape[0]
    out = jnp.zeros((k, H_A, HG), w.dtype)
    out = out.at[:, :, off:off + width].set(w.reshape(k, H_A, width))
    return out.reshape(k, H_A * HG)


def mla_prep(proj, pos, P, tm=256):
    n = proj.shape[0]
    half = DR_A // 2
    freqs = ROPE_BASE ** (-jnp.arange(half, dtype=F32) / half)
    ang = pos.astype(F32)[:, None] * freqs[None, :]
    cos, sin = jnp.cos(ang), jnp.sin(ang)
    one, zero = jnp.ones((n, DN_A), F32), jnp.zeros((n, DN_A), F32)
    pad = jnp.zeros((n, HG - DN_A - DR_A), F32)
    cs = jnp.concatenate([one, cos, cos, pad], axis=1)
    sn = jnp.concatenate([zero, -sin, sin, pad], axis=1)
    wuq = P['w_uq'].reshape(D_CQ, H_A, DN_A + DR_A)
    wuq = jnp.pad(wuq, ((0, 0), (0, 0), (0, HG - DN_A - DR_A))).reshape(D_CQ, QW).astype(BF16)
    gq = jnp.tile(jnp.concatenate([P['g_qn'], P['g_qr'], jnp.zeros((HG - DN_A - DR_A,), F32)]), H_A).reshape(1, QW)
    gkr = jnp.zeros((1, HG), F32).at[0, DN_A:DN_A + DR_A].set(P['g_kr'])
    wuk = _head_groups(P['w_uk'], DN_A).astype(BF16)
    wuv = _head_groups(P['w_uv'], DV_A).astype(BF16)
    gk = _head_groups(jnp.tile(P['g_kn'], H_A).reshape(1, -1), DN_A)
    eq = np.zeros((HG, HG), np.float32)
    eq[:DN_A, :DN_A] = 1.0 / DN_A
    eq[DN_A:DN_A + DR_A, DN_A:DN_A + DR_A] = 1.0 / DR_A
    ek = np.zeros((HG, HG), np.float32)
    ek[:DN_A, :DN_A] = 1.0 / DN_A
    perm = np.zeros((HG, HG), np.float32)
    j = np.arange(half)
    perm[DN_A + half + j, DN_A + j] = 1.0
    perm[DN_A + j, DN_A + half + j] = 1.0
    row = lambda a: pl.BlockSpec(a.shape, lambda i: (0, 0))
    consts = [P['g_cq'].reshape(1, -1), wuq, gq, P['g_ckv'].reshape(1, -1), gkr, wuk, gk, wuv,
              jnp.asarray(eq, BF16), jnp.asarray(ek, BF16), jnp.asarray(perm, BF16)]
    return pl.pallas_call(
        _mla_prep_body,
        grid=(n // tm,),
        in_specs=[pl.BlockSpec((tm, EVEN_PR_OFF), lambda i: (i, 0)),
                  pl.BlockSpec((tm, HG), lambda i: (i, 0)),
                  pl.BlockSpec((tm, HG), lambda i: (i, 0))] + [row(c) for c in consts],
        out_specs=[pl.BlockSpec((tm, QW), lambda i: (i, 0))] * 3
        + [pl.BlockSpec((tm, D_C), lambda i: (i, 0)), pl.BlockSpec((tm, HG), lambda i: (i, 0))],
        out_shape=[jax.ShapeDtypeStruct((n, QW), BF16)] * 3
        + [jax.ShapeDtypeStruct((n, D_C), F32), jax.ShapeDtypeStruct((n, HG), F32)],
        compiler_params=_cp("parallel"),
        name="mla_prep",
    )(proj, cs, sn, *consts)


def _mla_prompt_body(q_ref, k_ref, v_ref, o_ref, *, tq):
    qi = pl.program_id(1)
    row = lax.broadcasted_iota(I32, (tq, tq), 0)
    col = lax.broadcasted_iota(I32, (tq, tq), 1)
    for h in range(H_A):
        sl = slice(h * HG, (h + 1) * HG)
        q = q_ref[:, sl]

        def body(kb, carry):
            m, l, acc = carry
            off = pl.multiple_of(kb * tq, tq)
            s = _dot_nt(q, k_ref[pl.ds(off, tq), sl]) * MLA_SCALE
            s = jnp.where(col + kb * tq <= row + qi * tq, s, -jnp.inf)
            m_new = jnp.maximum(m, jnp.max(s, axis=-1, keepdims=True))
            alpha = jnp.exp(m - m_new)
            p = jnp.exp(s - m_new)
            l = l * alpha + jnp.sum(p, axis=-1, keepdims=True)
            acc = acc * alpha + _dot(p.astype(BF16), v_ref[pl.ds(off, tq), sl])
            return m_new, l, acc

        init = (jnp.full((tq, 1), -jnp.inf, F32), jnp.zeros((tq, 1), F32), jnp.zeros((tq, HG), F32))
        _, l, acc = lax.fori_loop(0, qi + 1, body, init)
        o_ref[:, h * DV_A:(h + 1) * DV_A] = (acc / l)[:, :DV_A].astype(BF16)


def mla_prompt_attention(qc, kc, vc, b, t, tq=256):
    nq = t // tq
    return pl.pallas_call(
        functools.partial(_mla_prompt_body, tq=tq),
        grid=(b, nq),
        in_specs=[pl.BlockSpec((tq, QW), lambda bi, qi: (bi * nq + qi, 0)),
                  pl.BlockSpec((t, QW), lambda bi, qi: (bi, 0)),
                  pl.BlockSpec((t, QW), lambda bi, qi: (bi, 0))],
        out_specs=pl.BlockSpec((tq, H_A * DV_A), lambda bi, qi: (bi * nq + qi, 0)),
        out_shape=jax.ShapeDtypeStruct((b * t, H_A * DV_A), BF16),
        compiler_params=_cp("parallel", "arbitrary"),
        name="mla_prompt_attention",
    )(qc, kc, vc)


PAGES_PER_STEP = 8
KV_STEPS = N_PAGES // PAGES_PER_STEP


def _mla_sample_body(pt_ref, *refs, t_new):
    npg = PAGES_PER_STEP
    ckv_pages, kr_pages = refs[:npg], refs[npg:2 * npg]
    (ckv_new, kr_new, qn_ref, qr_ref, wuk_ref, e_ref, wuv_ref, hm_ref, o_ref,
     m_scr, l_scr, acc_scr) = refs[2 * npg:]
    j = pl.program_id(1)

    def process(ckv, kr, mask):
        ckv_b = ckv.astype(BF16)
        kraw = _dot(ckv_b, wuk_ref[...])
        ms = _dot_f32(kraw * kraw, e_ref[...]) * (1.0 / DN_A)
        s = (_dot(kraw.astype(BF16), qn_ref[0]) * lax.rsqrt(ms + EPS)
             + _dot(kr.astype(BF16), qr_ref[0])) * MLA_SCALE
        if mask is not None:
            s = jnp.where(mask, s, -jnp.inf)
        m = m_scr[...]
        m_new = jnp.maximum(m, jnp.max(s, axis=0, keepdims=True))
        alpha = jnp.exp(m - m_new)
        p = jnp.exp(s - m_new)
        l_scr[...] = l_scr[...] * alpha + jnp.sum(p, axis=0, keepdims=True)
        acc_scr[...] = acc_scr[...] * alpha + _dot_tn(ckv_b, p.astype(BF16))
        m_scr[...] = m_new

    @pl.when(j == 0)
    def _():
        m_scr[...] = jnp.full(m_scr.shape, -jnp.inf, F32)
        l_scr[...] = jnp.zeros(l_scr.shape, F32)
        acc_scr[...] = jnp.zeros(acc_scr.shape, F32)
        nk, nq = PAGE_SIZE, H_A * t_new
        key = lax.broadcasted_iota(I32, (nk, nq), 0)
        qt = lax.broadcasted_iota(I32, (nk, nq), 1) % t_new
        process(ckv_new[0], kr_new[0], key <= qt)

    process(jnp.concatenate([r[0] for r in ckv_pages], axis=0), jnp.concatenate([r[0] for r in kr_pages], axis=0), None)

    @pl.when(j == KV_STEPS - 1)
    def _():
        o_lat =(acc_scr[...] / l_scr[...]).astype(BF16)
        full = _dot_tn(o_lat, wuv_ref[...]) * hm_ref[...]
        out = full[0:t_new]
        for h in range(1, H_A):
            out = out + full[h * t_new:(h + 1) * t_new]
        o_ref[0] = out.astype(BF16)


def mla_sample_attention(page_table, cache_ckv, cache_kr, ckv_new, kr_new, qn, qr, P):
    b, t_new = qn.shape[:2]
    nq = H_A * t_new
    eye = jnp.eye(H_A, dtype=BF16)
    qn_g = (qn.astype(F32) * P['g_kn']).astype(BF16)
    qn_bd = jnp.einsum('bthd,hg->bhdgt', qn_g, eye).reshape(b, H_A * DN_A, nq)
    qr_m = jnp.transpose(qr, (0, 3, 2, 1)).reshape(b, DR_A, nq)
    pad = lambda a: jnp.pad(a, ((0, 0), (0, PAGE_SIZE - t_new), (0, 0)))
    hm = (np.arange(nq)[:, None] // t_new == np.arange(H_A * DV_A)[None, :] // DV_A).astype(np.float32)
    e_head = (np.arange(H_A * DN_A)[:, None] // DN_A == np.arange(nq)[None, :] // t_new).astype(np.float32)
    npg = PAGES_PER_STEP
    page = lambda p, w: pl.BlockSpec((1, PAGE_SIZE, w), lambda bi, j, pt: (pt[bi * N_PAGES + j * npg + p], 0, 0))
    per_b = lambda s: pl.BlockSpec((1,) + s, lambda bi, j, pt: (bi, 0, 0))
    const = lambda a: pl.BlockSpec(a.shape, lambda bi, j, pt: (0, 0))
    consts = [P['w_uk'].astype(BF16), jnp.asarray(e_head, BF16), P['w_uv'].astype(BF16), jnp.asarray(hm)]
    grid_spec = pltpu.PrefetchScalarGridSpec(
        num_scalar_prefetch=1,
        grid=(b, KV_STEPS),
        in_specs=[page(p, D_C) for p in range(npg)] + [page(p, DR_A) for p in range(npg)]
        + [per_b((PAGE_SIZE, D_C)), per_b((PAGE_SIZE, DR_A)), per_b((H_A * DN_A, nq)), per_b((DR_A, nq))]
        + [const(c) for c in consts],
        out_specs=pl.BlockSpec((1, t_new, H_A * DV_A), lambda bi, j, pt: (bi, 0, 0)),
        scratch_shapes=[pltpu.VMEM((1, nq), F32), pltpu.VMEM((1, nq), F32), pltpu.VMEM((D_C, nq), F32)],
    )
    out = pl.pallas_call(
        functools.partial(_mla_sample_body, t_new=t_new),
        grid_spec=grid_spec,
        out_shape=jax.ShapeDtypeStruct((b, t_new, H_A * DV_A), BF16),
        compiler_params=_cp("parallel", "arbitrary"),
        name="mla_sample_attention",
    )(page_table.reshape(-1), *([cache_ckv] * npg), *([cache_kr] * npg), pad(ckv_new), pad(kr_new),
      qn_bd, qr_m, *consts)
    return out.reshape(b * t_new, H_A * DV_A)


RW_LORA_OFF = 3 * HB


def _rwkv_prep_body(pr_ref, sh_ref, mu_ref, w0_ref, wup_ref, a0_ref, aup_ref, gup_ref, kk_ref, ka_ref, rk_ref, e_ref,
                    nkk_o, wr_o, w_o, kka_o, k2_o, v_o, c1_o, c2_o, g_o, bonus_o):
    pr = pr_ref[...]
    xs = pr + mu_ref[...] * (sh_ref[...] - pr)
    r, k, v = xs[:, :HB], xs[:, HB:2 * HB], xs[:, 2 * HB:3 * HB]
    xwa = xs[:, RW_LORA_OFF:RW_LORA_OFF + D_WL + D_AL]
    xg = xs[:, RW_LORA_OFF + D_WL + D_AL:]
    w = jnp.exp(-RWKV_DECAY_SCALE * _sigmoid(w0_ref[...] + _dot(jnp.tanh(xwa).astype(BF16), wup_ref[...])))
    a = _sigmoid(a0_ref[...] + _dot(xwa.astype(BF16), aup_ref[...]))
    g_o[...] = _dot(_sigmoid(xg).astype(BF16), gup_ref[...])
    kk = k * kk_ref[...]
    kk = kk * lax.rsqrt(jnp.maximum(_dot_f32(kk * kk, e_ref[...]), 1e-12))
    k2 = k * (1.0 + (a - 1.0) * ka_ref[...])
    kka = kk * a
    nkk_o[...] = -kk
    wr_o[...] = w * r
    w_o[...] = w
    kka_o[...] = kka
    k2_o[...] = k2
    v_o[...] = v
    c1_o[...] = _dot_f32(kka * r, e_ref[...])
    c2_o[...] = _dot_f32(k2 * r, e_ref[...])
    bonus_o[...] = _dot_f32(r * k2 * rk_ref[...], e_ref[...]) * v


def rwkv_prep(proj, shifted, P, tm=256):
    n = proj.shape[0]
    nblk = EVEN_PR_OFF // RW_WIDTH
    assert EVEN_PR_OFF % LANES == 0
    zw = jnp.zeros((D_WL, HB), F32)
    consts = [P['mu'].reshape(1, -1), P['w0'].reshape(1, -1),
              jnp.concatenate([P['w_up'], zw]).astype(BF16), P['a0'].reshape(1, -1),
              jnp.concatenate([zw, P['a_up']]).astype(BF16), P['g_up'].astype(BF16),
              P['k_k'].reshape(1, -1), P['k_a'].reshape(1, -1), P['r_k'].reshape(1, -1),
              jnp.asarray(_blockdiag_ones(HB, HS_B), BF16)]
    pr = lax.slice_in_dim(proj, EVEN_PR_OFF, EVEN_W, axis=1)
    return pl.pallas_call(
        _rwkv_prep_body,
        grid=(n // tm,),
        in_specs=[pl.BlockSpec((tm, RW_WIDTH), lambda i: (i, 0)), pl.BlockSpec((tm, RW_WIDTH), lambda i: (i, 0))]
        + [pl.BlockSpec(c.shape, lambda i: (0, 0)) for c in consts],
        out_specs=[pl.BlockSpec((tm, HB), lambda i: (i, 0))] * 10,
        out_shape=[jax.ShapeDtypeStruct((n, HB), F32)] * 10,
        compiler_params=_cp("parallel"),
        name="rwkv_prep",
    )(pr, shifted, *consts)


def _rwkv_scan_body(nkk_r, wr_r, w_r, kka_r, k2_r, v_r, c1_r, c2_r, s0_ref, e_ref, d_ref, o_ref, sf_ref, s_scr,
                    *, nb, tc):
    c = pl.program_id(1)

    @pl.when(c == 0)
    def _():
        s_scr[...] = s0_ref[...]

    dmask = d_ref[...]
    half = HB // 2

    def step(t, carry):
        for b in range(nb):
            row = lambda ref: ref[b, pl.ds(t, 1), :]
            s = s_scr[b]
            stacked = jnp.concatenate([s * row(nkk_r), s * row(wr_r), dmask * row(v_r)], axis=0)
            hi, lo = _split2(stacked)
            seg = jnp.concatenate(
                [_dot(hi[:, i * half:(i + 1) * half], e_ref[...]) + _dot(lo[:, i * half:(i + 1) * half], e_ref[...])
                 for i in range(2)], axis=1)
            sa, t2, vb = seg[:HS_B], seg[HS_B:2 * HS_B], seg[2 * HS_B:]
            s_scr[b] = s * row(w_r) + sa * row(kka_r) + vb * row(k2_r)
            ob = t2 + sa * row(c1_r) + vb * row(c2_r)
            o_ref[b, pl.ds(t, 1), :] = jnp.sum(ob * dmask, axis=0, keepdims=True)
        return carry

    lax.fori_loop(0, tc, step, 0)

    @pl.when(c == pl.num_programs(1) - 1)
    def _():
        sf_ref[...] = s_scr[...]


def rwkv_scan(seqs, state0, b, t, nb, tc):
    s0 = jnp.transpose(state0, (0, 2, 1, 3)).reshape(b, HS_B, HB)
    e = jnp.asarray(_blockdiag_ones(HB // 2, HS_B), BF16)
    dmask = jnp.asarray((np.arange(HS_B)[:, None] == (np.arange(HB)[None, :] % HS_B)).astype(np.float32))
    seq_spec = pl.BlockSpec((nb, tc, HB), lambda i, c: (i, c, 0))
    st_spec = pl.BlockSpec((nb, HS_B, HB), lambda i, c: (i, 0, 0))
    o, sf = pl.pallas_call(
        functools.partial(_rwkv_scan_body, nb=nb, tc=tc),
        grid=(b // nb, t // tc),
        in_specs=[seq_spec] * 8 + [st_spec, pl.BlockSpec(e.shape, lambda i, c: (0, 0)),
                                   pl.BlockSpec(dmask.shape, lambda i, c: (0, 0))],
        out_specs=[seq_spec, st_spec],
        out_shape=[jax.ShapeDtypeStruct((b, t, HB), F32), jax.ShapeDtypeStruct((b, HS_B, HB), F32)],
        scratch_shapes=[pltpu.VMEM((nb, HS_B, HB), F32)],
        compiler_params=_cp("parallel", "arbitrary"),
        name="rwkv_scan",
    )(*[a.reshape(b, t, HB) for a in seqs], s0, e, dmask)
    sf = jnp.transpose(sf.reshape(b, HS_B, H_B, HS_B), (0, 2, 1, 3))
    return o.reshape(b * t, HB), sf


def _rwkv_post_body(o_ref, g_ref, bonus_ref, lng_ref, lnb_ref, e_ref, out_ref):
    o = o_ref[...]
    mu = _dot_f32(o, e_ref[...]) * (1.0 / HS_B)
    d = o - mu
    var = _dot_f32(d * d, e_ref[...]) * (1.0 / HS_B)
    y = d * lax.rsqrt(var + RWKV_LN_EPS) * lng_ref[...] + lnb_ref[...]
    out_ref[...] = ((y + bonus_ref[...]) * g_ref[...]).astype(BF16)


def rwkv_post(o, g, bonus, P, tm=256):
    n = o.shape[0]
    e = jnp.asarray(_blockdiag_ones(HB, HS_B), BF16)
    blk = pl.BlockSpec((tm, HB), lambda i: (i, 0))
    row = pl.BlockSpec((1, HB), lambda i: (0, 0))
    return pl.pallas_call(
        _rwkv_post_body,
        grid=(n // tm,),
        in_specs=[blk, blk, blk, row, row, pl.BlockSpec(e.shape, lambda i: (0, 0))],
        out_specs=blk,
        out_shape=jax.ShapeDtypeStruct((n, HB), BF16),
        compiler_params=_cp("parallel"),
        name="rwkv_post",
    )(o, g, bonus, P['ln_g'].reshape(1, -1), P['ln_b'].reshape(1, -1), e)


def rwkv_mix(proj, prev, state0, b, t, P, nb, tc, tm):
    pr = lax.slice_in_dim(proj, EVEN_PR_OFF, EVEN_W, axis=1).reshape(b, t, RW_WIDTH)
    shifted = jnp.concatenate([prev[:, None, :], pr[:, :-1, :]], axis=1).reshape(b * t, RW_WIDTH)
    outs = rwkv_prep(proj, shifted, P, tm)
    o, s_new = rwkv_scan(outs[:8], state0, b, t, nb, tc)
    return rwkv_post(o, outs[8], outs[9], P, tm), pr[:, -1, :], s_new


ODD_Z_OFF = 3 * HC
ODD_XBC_OFF = ODD_Z_OFF + D_INNER
ODD_DT_OFF = ODD_XBC_OFF + CONV_DIM
ODD_W = ODD_DT_OFF + LANES
SB_SCALE = HD_C ** -0.5


def _sb_weights(z, mask, suffix_of, run):
    lneg = -_softplus(z)
    if mask is not None:
        lneg = jnp.where(mask, lneg, 0.0)
    w = jnp.exp(z + lneg + suffix_of(lneg) + run)
    if mask is not None:
        w = jnp.where(mask, w, 0.0)
    return w, lneg


def _sb_prompt_body(q_ref, k_ref, v_ref, u_ref, o_ref, *, tq):
    qi = pl.program_id(1)
    row = lax.broadcasted_iota(I32, (tq, tq), 0)
    col = lax.broadcasted_iota(I32, (tq, tq), 1)
    for h in range(H_C):
        sl = slice(h * HD_C, (h + 1) * HD_C)
        q = q_ref[:, sl].astype(BF16)

        def body(i, carry):
            run, acc = carry
            kb = qi - i
            off = pl.multiple_of(kb * tq, tq)
            z = _dot_nt(q, k_ref[pl.ds(off, tq), sl].astype(BF16)) * SB_SCALE
            mask = col + kb * tq < row + qi * tq
            w, lneg = _sb_weights(z, mask, lambda l: _dot_f32(l, u_ref[...]), run)
            acc = acc + _dot(w.astype(BF16), v_ref[pl.ds(off, tq), sl].astype(BF16))
            return run + jnp.sum(lneg, axis=-1, keepdims=True), acc

        _, acc = lax.fori_loop(0, qi + 1, body, (jnp.zeros((tq, 1), F32), jnp.zeros((tq, HD_C), F32)))
        o_ref[:, sl] = acc.astype(BF16)


def sb_prompt_attention(proj, b, t, tq=256):
    nq = t // tq
    u = jnp.asarray(np.tril(np.ones((tq, tq), np.float32), -1), BF16)
    return pl.pallas_call(
        functools.partial(_sb_prompt_body, tq=tq),
        grid=(b, nq),
        in_specs=[pl.BlockSpec((tq, HC), lambda bi, qi: (bi * nq + qi, 0)),
                  pl.BlockSpec((t, HC), lambda bi, qi: (bi, 1)),
                  pl.BlockSpec((t, HC), lambda bi, qi: (bi, 2)),
                  pl.BlockSpec(u.shape, lambda bi, qi: (0, 0))],
        out_specs=pl.BlockSpec((tq, HC), lambda bi, qi: (bi * nq + qi, 0)),
        out_shape=jax.ShapeDtypeStruct((b * t, HC), BF16),
        compiler_params=_cp("parallel", "arbitrary"),
        name="sb_prompt_attention",
    )(proj, proj, proj, u)


def _sb_sample_body(pt_ref, *refs, t_new):
    npg = PAGES_PER_STEP
    k_pages, v_pages = refs[:npg], refs[npg:2 * npg]
    k_new, v_new, q_ref, u_ref, o_ref, run_scr, acc_scr = refs[2 * npg:]
    j = pl.program_id(1)
    nq = H_C * t_new
    col_head = lax.broadcasted_iota(I32, (nq, PAGE_SIZE), 0) // t_new

    def suffix_of(lneg):
        hi, lo = _split2(lneg)
        return _dot(u_ref[...], hi) + _dot(u_ref[...], lo)

    def process(k_ref, v_ref, mask):
        head = lambda ref, h: ref[0, pl.ds(h, PAGE_SIZE, stride=H_C), :].astype(BF16)
        z = _dot(head(k_ref, 0), q_ref[0, 0])
        for h in range(1, H_C):
            z = z + _dot(head(k_ref, h), q_ref[0, h])
        w, lneg = _sb_weights(z * SB_SCALE, mask, suffix_of, run_scr[...])
        wt = w.T[:nq]
        acc = acc_scr[...]
        for h in range(H_C):
            acc = acc + _dot(jnp.where(col_head == h, wt, 0.0).astype(BF16), head(v_ref, h))
        acc_scr[...] = acc
        run_scr[...] += jnp.sum(lneg, axis=0, keepdims=True)

    @pl.when(j == 0)
    def _():
        run_scr[...] = jnp.zeros(run_scr.shape, F32)
        acc_scr[...] = jnp.zeros(acc_scr.shape, F32)
        key = lax.broadcasted_iota(I32, (PAGE_SIZE, LANES), 0)
        qt = lax.broadcasted_iota(I32, (PAGE_SIZE, LANES), 1) % t_new
        process(k_new, v_new, key < qt)

    for p in range(npg):
        process(k_pages[p], v_pages[p], None)

    @pl.when(j == KV_STEPS - 1)
    def _():
        acc = acc_scr[...]
        for h in range(H_C):
            o_ref[0, :, h * HD_C:(h + 1) * HD_C] = acc[h * t_new:(h + 1) * t_new].astype(BF16)


def sb_sample_attention(page_table, cache_k, cache_v, q, k_new, v_new):
    b, t_new = q.shape[:2]
    nq = H_C * t_new
    assert nq <= LANES
    eye = jnp.eye(H_C, dtype=BF16)
    q_sel = jnp.einsum('bthd,hg->bhdgt', q.astype(BF16), eye).reshape(b, H_C, HD_C, nq)
    q_sel = jnp.pad(q_sel, ((0, 0), (0, 0), (0, 0), (0, LANES - nq)))
    rows = lambda a: a.reshape(a.shape[0], -1, HD_C)
    pad = lambda a: rows(jnp.pad(a, ((0, 0), (0, PAGE_SIZE - t_new), (0, 0), (0, 0))))
    cache_k, cache_v = rows(cache_k), rows(cache_v)
    u = jnp.asarray(np.triu(np.ones((PAGE_SIZE, PAGE_SIZE), np.float32), 1), BF16)
    npg = PAGES_PER_STEP
    blk = (1, PAGE_SIZE * H_C, HD_C)
    page = lambda p: pl.BlockSpec(
        blk, lambda bi, j, pt: (pt[bi * N_PAGES + (N_PAGES - 1 - (j * npg + p))], 0, 0))
    per_b = lambda s: pl.BlockSpec((1,) + s, lambda bi, j, pt: (bi,) + (0,) * len(s))
    grid_spec = pltpu.PrefetchScalarGridSpec(
        num_scalar_prefetch=1,
        grid=(b, KV_STEPS),
        in_specs=[page(p) for p in range(npg)] * 2
        + [per_b(blk[1:]), per_b(blk[1:]), per_b((H_C, HD_C, LANES)), pl.BlockSpec(u.shape, lambda bi, j, pt: (0, 0))],
        out_specs=pl.BlockSpec((1, t_new, HC), lambda bi, j, pt: (bi, 0, 0)),
        scratch_shapes=[pltpu.VMEM((1, LANES), F32), pltpu.VMEM((nq, HD_C), F32)],
    )
    out = pl.pallas_call(
        functools.partial(_sb_sample_body, t_new=t_new),
        grid_spec=grid_spec,
        out_shape=jax.ShapeDtypeStruct((b, t_new, HC), BF16),
        compiler_params=_cp("parallel", "arbitrary"),
        name="sb_sample_attention",
    )(page_table.reshape(-1), *([cache_k] * npg), *([cache_v] * npg), pad(k_new), pad(v_new), q_sel, u)
    return out.reshape(b * t_new, HC)


TAIL = 8


def _ssd_body(z_ref, xbc_ref, dt_ref, cbuf_ref, h0_ref, cw_ref, cb_ref, dtb_ref, alog_ref, drow_ref, gn_ref, tri_ref,
              y_ref, hf_ref, h_scr, tail_scr, y_scr, *, L):
    c = pl.program_id(1)
    mm = (lambda x: x.astype(BF16)) if L >= 16 else (lambda x: x.astype(BF16).astype(F32))

    @pl.when(c == 0)
    def _():
        h_scr[...] = h0_ref[0]
        tail_scr[...] = jnp.zeros(tail_scr.shape, F32)
        tail_scr[TAIL - (CONV_W - 1):, :] = cbuf_ref[0]

    xbc = xbc_ref[...]
    ext = jnp.concatenate([tail_scr[...], xbc], axis=0)
    conv = cb_ref[...]
    for i in range(CONV_W):
        s = TAIL - (CONV_W - 1) + i
        conv = conv + cw_ref[i:i + 1, :] * ext[s:s + L]
    tail_scr[...] = xbc[L - TAIL:]
    xc = conv * _sigmoid(conv)
    dt = _softplus(dt_ref[...] + dtb_ref[...])
    a = -jnp.exp(alog_ref[...])
    lane = lax.broadcasted_iota(I32, (1, LANES), 1)
    dta = jnp.where(lane < H_D, dt * a, 0.0)
    hi = dta.astype(BF16)
    r1 = dta - hi.astype(F32)
    mid = r1.astype(BF16)
    lo = (r1 - mid.astype(F32)).astype(BF16)
    tri = tri_ref[...]
    acs = (_dot(tri, mm(hi)) + _dot(tri, mm(mid))) + _dot(tri, mm(lo))
    hi, mid, lo = acs.astype(BF16), None, None
    r1 = acs - hi.astype(F32)
    mid = r1.astype(BF16)
    lo = (r1 - mid.astype(F32)).astype(BF16)
    row = lax.broadcasted_iota(I32, (L, L), 0)
    col = lax.broadcasted_iota(I32, (L, L), 1)
    lane_l = lax.broadcasted_iota(I32, (L, LANES), 1)
    acs_last = acs[L - 1:L, :]
    cbs = []
    for g in range(G_D):
        bg = xc[:, D_INNER + g * N_D:D_INNER + (g + 1) * N_D]
        cg = xc[:, D_INNER + G_D * N_D + g * N_D:D_INNER + G_D * N_D + (g + 1) * N_D]
        cbs.append((bg, cg, _dot_nt(mm(cg), mm(bg))))
    for h in range(H_D):
        bg, cg, cb = cbs[h // (H_D // G_D)]
        sel = mm((lane_l == h).astype(F32))
        acs_row = (_dot_nt(sel, mm(hi)) + _dot_nt(sel, mm(mid))) + _dot_nt(sel, mm(lo))
        acs_col = acs[:, h:h + 1]
        lmat = jnp.exp(jnp.where(col <= row, acs_col - acs_row, -jnp.inf))
        xh = xc[:, h * P_D:(h + 1) * P_D]
        xdt = xh * dt[:, h:h + 1]
        hprev = h_scr[h]
        y = _dot(mm(cb * lmat), mm(xdt)) + _dot_nt(mm(cg * jnp.exp(acs_col)), mm(hprev))
        y_scr[:, h * P_D:(h + 1) * P_D] = y + drow_ref[:, h * P_D:(h + 1) * P_D] * xh
        last = acs_last[:, h:h + 1]
        bw = bg * jnp.exp(last - acs_col)
        if L >= 16:
            st = _dot_tn(mm(xdt), mm(bw))
        else:
            xr, br = xdt - mm(xdt), bw - mm(bw)
            st = (_dot_tn(mm(xdt), mm(bw)) + _dot_tn(mm(xdt), mm(br))) + _dot_tn(mm(xr), mm(bw))
        h_scr[h] = hprev * jnp.exp(last) + st
    z = z_ref[...]
    y_ref[...] = _rms(y_scr[...] * (z * _sigmoid(z)), gn_ref[...]).astype(BF16)

    @pl.when(c == pl.num_programs(1) - 1)
    def _():
        hf_ref[0] = h_scr[...]


def ssd_mix(proj, conv_buf, state0, b, t, P):
    L = SSD_CHUNK if t % SSD_CHUNK == 0 else t
    assert L % TAIL == 0
    nc = t // L
    pad_h = lambda v: jnp.pad(v.reshape(1, -1), ((0, 0), (0, LANES - H_D)))
    consts = [P['conv_w'], P['conv_b'].reshape(1, -1), pad_h(P['dt_bias']), pad_h(P['a_log']),
              jnp.repeat(P['d'], P_D).reshape(1, -1), P['g_norm'].reshape(1, -1)]
    tri_np = np.tril(np.ones((L, L), np.float32))
    tri = jnp.asarray(tri_np, BF16 if L >= 16 else F32)
    zc, xc, dc = ODD_Z_OFF // D_INNER, ODD_XBC_OFF // CONV_DIM, ODD_DT_OFF // LANES
    assert ODD_Z_OFF % D_INNER == 0 and ODD_XBC_OFF % CONV_DIM == 0 and ODD_DT_OFF % LANES == 0
    y, hf = pl.pallas_call(
        functools.partial(_ssd_body, L=L),
        grid=(b, nc),
        in_specs=[pl.BlockSpec((L, D_INNER), lambda bi, c: (bi * nc + c, zc)),
                  pl.BlockSpec((L, CONV_DIM), lambda bi, c: (bi * nc + c, xc)),
                  pl.BlockSpec((L, LANES), lambda bi, c: (bi * nc + c, dc)),
                  pl.BlockSpec((1, CONV_W - 1, CONV_DIM), lambda bi, c: (bi, 0, 0)),
                  pl.BlockSpec((1, H_D, P_D, N_D), lambda bi, c: (bi, 0, 0, 0))]
        + [pl.BlockSpec(cst.shape, lambda bi, c: (0, 0)) for cst in consts]
        + [pl.BlockSpec(tri.shape, lambda bi, c: (0, 0))],
        out_specs=[pl.BlockSpec((L, D_INNER), lambda bi, c: (bi * nc + c, 0)),
                   pl.BlockSpec((1, H_D, P_D, N_D), lambda bi, c: (bi, 0, 0, 0))],
        out_shape=[jax.ShapeDtypeStruct((b * t, D_INNER), BF16), jax.ShapeDtypeStruct((b, H_D, P_D, N_D), F32)],
        scratch_shapes=[pltpu.VMEM((H_D, P_D, N_D), F32), pltpu.VMEM((TAIL, CONV_DIM), F32),
                        pltpu.VMEM((L, D_INNER), F32)],
        compiler_params=_cp("parallel", "arbitrary"),
        name="ssd_mix",
    )(proj, proj, proj, conv_buf, state0, *consts, tri)
    return y, hf


SC_L = 16
SC_TB = 16
GELU_C = math.sqrt(2.0 / math.pi)


def _sc_gelu(a):
    y = GELU_C * (a + 0.044715 * (a * a * a))
    return 0.5 * a * (2.0 - 2.0 / (jnp.exp(2.0 * y) + 1.0))


def peer_experts(hn, idx, gate, u_tab, v_tab, x):
    n = hn.shape[0]
    info = plsc.get_sparse_core_info()
    nc, ns = info.num_cores, info.num_subcores
    assert info.num_lanes == SC_L and PEER_TOPK == SC_L
    nw = nc * ns
    per_w = n // nw
    assert n % (nw * SC_TB) == 0
    n_blk = per_w // SC_TB
    n_chunk = SC_TB * PEER_HEADS
    n_vec = D_MODEL // SC_L
    mesh = plsc.VectorSubcoreMesh(core_axis_name="c", subcore_axis_name="s")

    @functools.partial(
        pl.kernel, mesh=mesh, out_type=jax.ShapeDtypeStruct((n, D_MODEL), F32),
        scratch_types=[pltpu.VMEM((SC_TB, D_MODEL), F32), pltpu.VMEM((SC_TB, D_MODEL), F32),
                       pltpu.VMEM((SC_TB, PEER_HEADS, SC_L), I32), pltpu.VMEM((SC_TB, PEER_HEADS * SC_L), F32),
                       pltpu.VMEM((2, SC_L, D_MODEL), F32), pltpu.VMEM((2, SC_L, D_MODEL), F32),
                       pltpu.SemaphoreType.DMA((2,)), pltpu.SemaphoreType.DMA((2,))],
        compiler_params=pltpu.CompilerParams(needs_layout_passes=False),
        name="peer_experts")
    def run(hn_hbm, idx_hbm, gate_hbm, u_hbm, v_hbm, x_hbm, out_hbm, hnv, outv, idxv, gv, ubuf, vbuf, usem, vsem):
        wid = lax.axis_index("s") * nc + lax.axis_index("c")
        lanes = lax.iota(I32, SC_L)

        def copies(q, slot):
            tl, hd = q // PEER_HEADS, q % PEER_HEADS
            rows = idxv.at[tl, hd]
            return (pltpu.make_async_copy(u_hbm.at[rows], ubuf.at[slot], usem.at[slot]),
                    pltpu.make_async_copy(v_hbm.at[rows], vbuf.at[slot], vsem.at[slot]))

        def start(q, slot):
            for cp in copies(q, slot):
                cp.start()

        def compute(q, slot):
            tl, hd = q // PEER_HEADS, q % PEER_HEADS
            cu, cv = copies(q, slot)
            cu.wait()

            def ubody(c, accs):
                off = pl.multiple_of(c * SC_L, SC_L)
                xc = hnv[tl, pl.ds(off, SC_L)]
                return tuple(accs[j] + ubuf[slot, j, pl.ds(off, SC_L)] * xc for j in range(SC_L))

            zero = jnp.zeros((SC_L,), F32)
            accs = lax.fori_loop(0, n_vec, ubody, (zero,) * SC_L)
            acts = zero
            for j in range(SC_L):
                acts = jnp.where(lanes == j, jnp.sum(accs[j]), acts)
            coef = gv[tl, pl.ds(pl.multiple_of(hd * SC_L, SC_L), SC_L)] * _sc_gelu(acts)
            cj = [jnp.sum(jnp.where(lanes == j, coef, 0.0)) for j in range(SC_L)]
            cv.wait()

            def vbody(c, carry):
                off = pl.multiple_of(c * SC_L, SC_L)
                o = outv[tl, pl.ds(off, SC_L)]
                for j in range(SC_L):
                    o = o + cj[j] * vbuf[slot, j, pl.ds(off, SC_L)]
                outv[tl, pl.ds(off, SC_L)] = o
                return carry

            lax.fori_loop(0, n_vec, vbody, 0)

        def block(bi, carry):
            t0 = pl.multiple_of(wid * per_w + bi * SC_TB, SC_TB)
            pltpu.sync_copy(hn_hbm.at[pl.ds(t0, SC_TB)], hnv)
            pltpu.sync_copy(x_hbm.at[pl.ds(t0, SC_TB)], outv)
            pltpu.sync_copy(idx_hbm.at[pl.ds(t0, SC_TB)], idxv)
            pltpu.sync_copy(gate_hbm.at[pl.ds(t0, SC_TB)], gv)
            start(0, 0)

            def pair(p, c2):
                q = 2 * p
                start(q + 1, 1)
                compute(q, 0)

                @pl.when(q + 2 < n_chunk)
                def _():
                    start(q + 2, 0)

                compute(q + 1, 1)
                return c2

            lax.fori_loop(0, n_chunk // 2, pair, 0)
            pltpu.sync_copy(outv, out_hbm.at[pl.ds(t0, SC_TB)])
            return carry

        lax.fori_loop(0, n_blk, block, 0)

    return run(hn, idx.reshape(n, PEER_HEADS, SC_L), gate, u_tab, v_tab, x)


def peer_ffn(x, g, w_q, sub_keys, u_tab, v_tab):
    hn, idx, gate = peer_route(x, g, w_q, sub_keys)
    return peer_experts(hn, idx, gate, u_tab, v_tab, x)


def _even_w_in(w):
    out = jnp.zeros((D_MODEL, EVEN_W), F32)
    out = out.at[:, :EVEN_KR_OFF].set(w[:, :D_CQ + D_C])
    out = out.at[:, EVEN_KR_OFF + DN_A:EVEN_KR_OFF + DN_A + DR_A].set(w[:, D_CQ + D_C:D_CQ + D_C + DR_A])
    out = out.at[:, EVEN_PR_OFF:].set(w[:, D_CQ + D_C + DR_A:])
    return out.astype(BF16)


def _even_prompt(x, b, t, pos, g_mix, P, W):
    proj = norm_matmul(x, g_mix, W['in'])
    qc, kc, vc, ckv, kr = mla_prep(proj, pos, P)
    o_a = mla_prompt_attention(qc, kc, vc, b, t)
    o_b, sh, wkv = rwkv_mix(proj, jnp.zeros((b, RW_WIDTH), F32), jnp.zeros((b, H_B, HS_B, HS_B), F32),
                            b, t, P, nb=min(b, 4), tc=64, tm=256)
    x = matmul_res([o_a, o_b], W['out'], x)
    return x, (ckv.reshape(b, t, D_C), kr[:, DN_A:DN_A + DR_A].reshape(b, t, DR_A), sh, wkv)


def _even_sample(x, b, t, pos, page_table, cache_ckv, cache_kr, rw_shift, rw_wkv, g_mix, P, W):
    proj = norm_matmul(x, g_mix, W['in'])
    qc, _, _, ckv, kr = mla_prep(proj, pos, P)
    qc = qc.reshape(b, t, H_A, HG)
    ckv = ckv.reshape(b, t, D_C)
    kr = kr[:, DN_A:DN_A + DR_A].reshape(b, t, DR_A)
    o_a = mla_sample_attention(page_table, cache_ckv, cache_kr, ckv, kr, qc[..., :DN_A], qc[..., DN_A:DN_A + DR_A], P)
    o_b, sh, wkv = rwkv_mix(proj, rw_shift, rw_wkv, b, t, P, nb=8, tc=t, tm=256)
    x = matmul_res([o_a, o_b], W['out'], x)
    return x, (ckv, kr, sh, wkv)


def _odd_states(proj, b, t):
    p3 = proj.reshape(b, t, ODD_W)
    return (p3[:, :, HC:2 * HC].reshape(b, t, H_C, HD_C), p3[:, :, 2 * HC:3 * HC].reshape(b, t, H_C, HD_C),
            p3[:, t - (CONV_W - 1):, ODD_XBC_OFF:ODD_XBC_OFF + CONV_DIM])


def _odd_prompt(x, b, t, g_mix, P, W):
    proj = norm_matmul(x, g_mix, W['in'])
    o_c = sb_prompt_attention(proj, b, t)
    y, ssm = ssd_mix(proj, jnp.zeros((b, CONV_W - 1, CONV_DIM), F32), jnp.zeros((b, H_D, P_D, N_D), F32), b, t, P)
    x = matmul_res([o_c, y], W['out'], x)
    return x, _odd_states(proj, b, t) + (ssm,)


def _odd_sample(x, b, t, page_table, cache_k, cache_v, conv_state, ssm_state, g_mix, P, W):
    proj = norm_matmul(x, g_mix, W['in'])
    heads = lambda c: proj[:, c * HC:(c + 1) * HC].reshape(b, t, H_C, HD_C)
    o_c = sb_sample_attention(page_table, cache_k, cache_v, heads(0), heads(1), heads(2))
    y, ssm = ssd_mix(proj, conv_state, ssm_state, b, t, P)
    x = matmul_res([o_c, y], W['out'], x)
    return x, _odd_states(proj, b, t) + (ssm,)


PROMPT_SPLITS = 2


def kernel(x_prompt, x_sample, mem_prompt, page_table, cache_mla_ckv, cache_mla_krope, state_rwkv_shift, state_rwkv_wkv, cache_sb_k, cache_sb_v, state_ssm_conv, state_ssm, cache_mem_k, cache_mem_v, norm_mix, norm_mem, norm_ffn, w_in_even, w_out_even, mla_g_cq, mla_w_uq, mla_g_ckv, mla_w_uk, mla_w_uv, mla_g_qn, mla_g_kn, mla_g_qr, mla_g_kr, rw_mu, rw_w0, rw_w_up, rw_a0, rw_a_up, rw_g_up, rw_k_k, rw_k_a, rw_r_k, rw_ln_g, rw_ln_b, w_in_odd, w_out_odd, ssm_conv_w, ssm_conv_b, ssm_dt_bias, ssm_a_log, ssm_d, ssm_g_norm, mem_g_src, mem_w_q, mem_w_kv, mem_g_q, mem_g_k, mem_w_o, peer_w_q, peer_sub_keys, peer_u, peer_v):
    bp, tp = x_prompt.shape[:2]
    bs, ts = x_sample.shape[:2]
    depth = norm_mix.shape[0]
    nsp = PROMPT_SPLITS if bp % PROMPT_SPLITS == 0 else 1
    bh = bp // nsp
    part = lambda a, k: a[k * bh:(k + 1) * bh]
    xps = [part(x_prompt, k).reshape(bh * tp, D_MODEL) for k in range(nsp)]
    xs = x_sample.reshape(bs * ts, D_MODEL)
    pos_p = jnp.tile(jnp.arange(tp, dtype=I32), bh)
    pos_s = jnp.tile(PAST_LEN + jnp.arange(ts, dtype=I32), bs)
    even_p, even_s, odd_p, odd_s, mem_k, mem_v = [], [], [], [], [], []
    for layer in range(depth):
        i = layer // 2
        even = layer % 2 == 0
        g_mix = norm_mix[layer]
        if even:
            P = dict(w_in=w_in_even[i], w_out=w_out_even[i], g_cq=mla_g_cq[i], w_uq=mla_w_uq[i],
                     g_ckv=mla_g_ckv[i], w_uk=mla_w_uk[i], w_uv=mla_w_uv[i], g_qn=mla_g_qn[i],
                     g_kn=mla_g_kn[i], g_qr=mla_g_qr[i], g_kr=mla_g_kr[i],
                     mu=rw_mu[i], w0=rw_w0[i], w_up=rw_w_up[i], a0=rw_a0[i], a_up=rw_a_up[i],
                     g_up=rw_g_up[i], k_k=rw_k_k[i], k_a=rw_k_a[i], r_k=rw_r_k[i],
                     ln_g=rw_ln_g[i], ln_b=rw_ln_b[i])
            w_out = P['w_out'].astype(BF16)
            W = {'in': _even_w_in(P['w_in']), 'out': [w_out[:H_A * DV_A], w_out[H_A * DV_A:]]}
        else:
            P = dict(w_in=w_in_odd[i], w_out=w_out_odd[i], conv_w=ssm_conv_w[i], conv_b=ssm_conv_b[i],
                     dt_bias=ssm_dt_bias[i], a_log=ssm_a_log[i], d=ssm_d[i], g_norm=ssm_g_norm[i])
            w_out = P['w_out'].astype(BF16)
            W = {'in': jnp.pad(P['w_in'], ((0, 0), (0, ODD_W - P['w_in'].shape[1]))).astype(BF16),
                 'out': [w_out[:HC], w_out[HC:]]}
        peer = lambda x: peer_ffn(x, norm_ffn[layer], peer_w_q[layer], peer_sub_keys[layer], peer_u[layer], peer_v[layer])
        mem = lambda x, t, k, v, tq: memory_attend(x, t, norm_mem[layer], mem_w_q[layer], mem_g_q[layer], k, v,
                                                   mem_w_o[layer], tq=tq)
        if even:
            xs, ss = _even_sample(xs, bs, ts, pos_s, page_table, cache_mla_ckv[i], cache_mla_krope[i],
                                  state_rwkv_shift[i], state_rwkv_wkv[i], g_mix, P, W)
            even_s.append(ss)
        else:
            xs, ss = _odd_sample(xs, bs, ts, page_table, cache_sb_k[i], cache_sb_v[i], state_ssm_conv[i],
                                 state_ssm[i], g_mix, P, W)
            odd_s.append(ss)
        xs = peer(mem(xs, ts, cache_mem_k[layer].reshape(bs, N_MEM, MEM_W), cache_mem_v[layer].reshape(bs, N_MEM, MEM_W),
                      ts))
        mk, mv = memory_kv(mem_prompt.reshape(bp * N_MEM, D_MODEL), mem_g_src[layer], mem_w_kv[layer], mem_g_k[layer])
        mem_k.append(mk.reshape(bp, N_MEM, MEM_HEADS, MEM_HD))
        mem_v.append(mv.reshape(bp, N_MEM, MEM_HEADS, MEM_HD))
        mk, mv = mk.reshape(bp, N_MEM, MEM_W), mv.reshape(bp, N_MEM, MEM_W)
        parts = []
        for k in range(nsp):
            if even:
                x, sp = _even_prompt(xps[k], bh, tp, pos_p, g_mix, P, W)
            else:
                x, sp = _odd_prompt(xps[k], bh, tp, g_mix, P, W)
            parts.append(sp)
            xps[k] = peer(mem(x, tp, part(mk, k), part(mv, k), 256))
        sp = tuple(jnp.concatenate([p[j] for p in parts], axis=0) for j in range(len(parts[0])))
        (even_p if even else odd_p).append(sp)
    stack = lambda groups, k: jnp.stack([g[k] for g in groups])
    xp = jnp.concatenate(xps, axis=0)
    return (xp.reshape(bp, tp, D_MODEL), xs.reshape(bs, ts, D_MODEL),
            stack(even_p, 0), stack(even_p, 1), stack(even_p, 2), stack(even_p, 3),
            stack(odd_p, 0), stack(odd_p, 1), stack(odd_p, 2), stack(odd_p, 3),
            jnp.stack(mem_k), jnp.stack(mem_v),
            stack(even_s, 0), stack(even_s, 1), stack(even_s, 2), stack(even_s, 3),
            stack(odd_s, 0), stack(odd_s, 1), stack(odd_s, 2), stack(odd_s, 3))
```

```python
import functools
import math

import numpy as np
import jax
import jax.numpy as jnp
from jax import lax
from jax.experimental import pallas as pl
from jax.experimental.pallas import tpu as pltpu
from jax.experimental.pallas import tpu_sc as plsc

F32 = jnp.float32
BF16 = jnp.bfloat16
I32 = jnp.int32

D_MODEL = 1024
EPS = 1e-6
PAST_LEN = 8192
PAGE_SIZE = 128
N_PAGES = PAST_LEN // PAGE_SIZE

H_A, DN_A, DR_A, DV_A, D_CQ, D_C = 8, 64, 32, 64, 384, 256
ROPE_BASE = 10000.0
MLA_SCALE = (DN_A + DR_A) ** -0.5
H_B, HS_B, D_WL, D_AL, D_GL = 8, 64, 64, 64, 128
HB = H_B * HS_B
RW_WIDTH = 3 * HB + D_WL + D_AL + D_GL
RWKV_DECAY_SCALE = 0.606531
RWKV_LN_EPS = 64e-5
H_C, HD_C = 8, 64
HC = H_C * HD_C
H_D, P_D, N_D, G_D, CONV_W = 8, 64, 128, 2, 4
D_INNER = H_D * P_D
CONV_DIM = D_INNER + 2 * G_D * N_D
SSD_CHUNK = 128
N_MEM, MEM_HEADS, MEM_HD = 256, 4, 64
MEM_W = MEM_HEADS * MEM_HD
PEER_HEADS, N_KEYS, PEER_DQ, PEER_TOPK = 8, 128, 256, 16
LANES = 128
VMEM_LIMIT = 56 * 1024 * 1024


def _cp(*sem):
    return pltpu.CompilerParams(dimension_semantics=sem, vmem_limit_bytes=VMEM_LIMIT)


def _dot(a, b):
    return jnp.dot(a, b, preferred_element_type=F32)


def _dot_nt(a, b):
    return lax.dot_general(a, b, (((1,), (1,)), ((), ())), preferred_element_type=F32)


def _dot_tn(a, b):
    return lax.dot_general(a, b, (((0,), (0,)), ((), ())), preferred_element_type=F32)


def _split2(x):
    hi = x.astype(BF16)
    lo = (x - hi.astype(F32)).astype(BF16)
    return hi, lo


def _dot_f32(x, m):
    hi, lo = _split2(x)
    return _dot(hi, m) + _dot(lo, m)


def _dot_sel(x, m):
    hi = x.astype(BF16)
    r1 = x - hi.astype(F32)
    mid = r1.astype(BF16)
    lo = (r1 - mid.astype(F32)).astype(BF16)
    return (_dot(hi, m) + _dot(mid, m)) + _dot(lo, m)


def _rms(x, g):
    return x * lax.rsqrt(jnp.mean(x * x, axis=-1, keepdims=True) + EPS) * g


def _sigmoid(x):
    return 1.0 / (1.0 + jnp.exp(-x))


def _softplus(x):
    return jnp.maximum(x, 0.0) + jnp.log(1.0 + jnp.exp(-jnp.abs(x)))


def _blockdiag_ones(n, seg):
    i = np.arange(n)
    return (i[:, None] // seg == i[None, :] // seg).astype(np.float32)


def _nm_body(x_ref, g_ref, w_ref, *rest):
    o_ref = rest[-1]
    h = _rms(x_ref[...], g_ref[...])
    o_ref[...] = _dot(h.astype(BF16), w_ref[...])


def norm_matmul(x, g, w, tm=256, after=None):
    n, k = x.shape
    m = w.shape[1]
    extra = [] if after is None else [after]
    return pl.pallas_call(
        _nm_body,
        grid=(n // tm,),
        in_specs=[pl.BlockSpec((tm, k), lambda i: (i, 0)),
                  pl.BlockSpec((1, k), lambda i: (0, 0)),
                  pl.BlockSpec((k, m), lambda i: (0, 0))] + [pl.BlockSpec(memory_space=pl.ANY) for _ in extra],
        out_specs=pl.BlockSpec((tm, m), lambda i: (i, 0)),
        out_shape=jax.ShapeDtypeStruct((n, m), F32),
        compiler_params=_cp("parallel"),
        name="norm_matmul",
    )(x, g.reshape(1, k), w, *extra)


def _mr_body(*refs, n_in):
    a_refs, w_refs, r_ref, o_ref = refs[:n_in], refs[n_in:2 * n_in], refs[2 * n_in], refs[2 * n_in + 1]
    acc = r_ref[...]
    for a_ref, w_ref in zip(a_refs, w_refs):
        acc = acc + _dot(a_ref[...].astype(BF16), w_ref[...])
    o_ref[...] = acc


def matmul_res(a_list, w_list, res, tm=256):
    n, m = res.shape
    n_in = len(a_list)
    in_specs = ([pl.BlockSpec((tm, a.shape[1]), lambda i: (i, 0)) for a in a_list]
                + [pl.BlockSpec(w.shape, lambda i: (0, 0)) for w in w_list]
                + [pl.BlockSpec((tm, m), lambda i: (i, 0))])
    return pl.pallas_call(
        functools.partial(_mr_body, n_in=n_in),
        grid=(n // tm,),
        in_specs=in_specs,
        out_specs=pl.BlockSpec((tm, m), lambda i: (i, 0)),
        out_shape=jax.ShapeDtypeStruct((n, m), F32),
        compiler_params=_cp("parallel"),
        name="matmul_res",
    )(*a_list, *w_list, res)


def _memkv_body(m_ref, g_ref, w_ref, gk_ref, e_ref, k_ref, v_ref):
    h = _rms(m_ref[...], g_ref[...])
    kv = _dot(h.astype(BF16), w_ref[...])
    k = kv[:, :MEM_W]
    ms = _dot_f32(k * k, e_ref[...]) * (1.0 / MEM_HD)
    k_ref[...] = k * lax.rsqrt(ms + EPS) * gk_ref[...]
    v_ref[...] = kv[:, MEM_W:]


def memory_kv(mem2d, g_src, w_kv, g_k, tm=256):
    n = mem2d.shape[0]
    e = jnp.asarray(_blockdiag_ones(MEM_W, MEM_HD), BF16)
    gk = jnp.tile(g_k, MEM_HEADS).reshape(1, MEM_W)
    return pl.pallas_call(
        _memkv_body,
        grid=(n // tm,),
        in_specs=[pl.BlockSpec((tm, D_MODEL), lambda i: (i, 0)),
                  pl.BlockSpec((1, D_MODEL), lambda i: (0, 0)),
                  pl.BlockSpec((D_MODEL, 2 * MEM_W), lambda i: (0, 0)),
                  pl.BlockSpec((1, MEM_W), lambda i: (0, 0)),
                  pl.BlockSpec((MEM_W, MEM_W), lambda i: (0, 0))],
        out_specs=[pl.BlockSpec((tm, MEM_W), lambda i: (i, 0))] * 2,
        out_shape=[jax.ShapeDtypeStruct((n, MEM_W), F32)] * 2,
        compiler_params=_cp("parallel"),
        name="memory_kv",
    )(mem2d, g_src.reshape(1, -1), w_kv.astype(BF16), gk, e)


def _memattn_body(x_ref, g_ref, wq_ref, gq_ref, e_ref, k_ref, v_ref, wo_ref, o_ref, att_ref):
    x = x_ref[...]
    h = _rms(x, g_ref[...])
    q = _dot(h.astype(BF16), wq_ref[...])
    ms = _dot_f32(q * q, e_ref[...]) * (1.0 / MEM_HD)
    q = (q * lax.rsqrt(ms + EPS) * gq_ref[...]).astype(BF16)
    k = k_ref[0].astype(BF16)
    v = v_ref[0].astype(BF16)
    for hd in range(MEM_HEADS):
        sl = slice(hd * MEM_HD, (hd + 1) * MEM_HD)
        s = _dot_nt(q[:, sl], k[:, sl]) * (MEM_HD ** -0.5)
        p = jnp.exp(s - jnp.max(s, axis=-1, keepdims=True))
        p = p / jnp.sum(p, axis=-1, keepdims=True)
        att_ref[:, sl] = _dot(p.astype(BF16), v[:, sl])
    o_ref[...] = x + _dot(att_ref[...].astype(BF16), wo_ref[...])


def memory_attend(x, seq_t, g_norm, w_q, g_q, k, v, w_o, tq):
    n = x.shape[0]
    per_b = seq_t // tq
    e = jnp.asarray(_blockdiag_ones(MEM_W, MEM_HD), BF16)
    gq = jnp.tile(g_q, MEM_HEADS).reshape(1, MEM_W)
    return pl.pallas_call(
        _memattn_body,
        grid=(n // tq,),
        in_specs=[pl.BlockSpec((tq, D_MODEL), lambda i: (i, 0)),
                  pl.BlockSpec((1, D_MODEL), lambda i: (0, 0)),
                  pl.BlockSpec((D_MODEL, MEM_W), lambda i: (0, 0)),
                  pl.BlockSpec((1, MEM_W), lambda i: (0, 0)),
                  pl.BlockSpec((MEM_W, MEM_W), lambda i: (0, 0)),
                  pl.BlockSpec((1, N_MEM, MEM_W), lambda i: (i // per_b, 0, 0)),
                  pl.BlockSpec((1, N_MEM, MEM_W), lambda i: (i // per_b, 0, 0)),
                  pl.BlockSpec((MEM_W, D_MODEL), lambda i: (0, 0))],
        out_specs=pl.BlockSpec((tq, D_MODEL), lambda i: (i, 0)),
        out_shape=jax.ShapeDtypeStruct((n, D_MODEL), F32),
        scratch_shapes=[pltpu.VMEM((tq, MEM_W), F32)],
        compiler_params=_cp("parallel"),
        name="memory_attend",
    )(x, g_norm.reshape(1, -1), w_q.astype(BF16), gq, e, k, v, w_o.astype(BF16))


def _topk_rows(s, k, payload=None, order=None):
    rows, cols = s.shape
    ridx = lax.broadcasted_iota(I32, (rows, cols), 0) if order is None else order
    orow = lax.broadcasted_iota(I32, (k, cols), 0)

    def body(r, carry):
        s, acc_v, acc_i = carry
        m = jnp.max(s, axis=0, keepdims=True)
        first = jnp.min(jnp.where(s == m, ridx, jnp.iinfo(jnp.int32).max), axis=0, keepdims=True)
        hit = ridx == first
        if payload is None:
            pick = first.astype(F32)
        else:
            pick = jnp.max(jnp.where(hit, payload, -1.0), axis=0, keepdims=True)
        here = orow == r
        return jnp.where(hit, -jnp.inf, s), jnp.where(here, m, acc_v), jnp.where(here, pick, acc_i)

    z = jnp.zeros((k, cols), F32)
    _, acc_v, acc_i = lax.fori_loop(0, k, body, (s, z, z))
    return acc_v, acc_i


def _route_body(x_ref, g_ref, wq_ref, keys_ref, hn_ref, idx_ref, gate_ref):
    h = _rms(x_ref[...], g_ref[...])
    hn_ref[...] = h
    q = _dot(h.astype(BF16), wq_ref[...])
    tops = []
    for c in range(2):
        s = _dot_nt(keys_ref[0, c], q[:, c * LANES:(c + 1) * LANES].astype(BF16))
        tops.append(_topk_rows(s, PEER_TOPK))
    (s0, i0), (s1, i1) = tops
    tm = s0.shape[1]
    ss, ii, ff = [], [], []

    def rows_a(a, nb):
        ss.append(s0[a:a + 1, :] + s1[:nb])
        ii.append(i0[a:a + 1, :] * float(N_KEYS) + i1[:nb])
        ff.append(a * PEER_TOPK + lax.broadcasted_iota(I32, (nb, tm), 0))

    def rows_b(b, na, a_min):
        a_idx = lax.broadcasted_iota(I32, (na, tm), 0)
        ss.append(jnp.where(a_idx >= a_min, s0[:na] + s1[b:b + 1, :], -jnp.inf))
        ii.append(i0[:na] * float(N_KEYS) + i1[b:b + 1, :])
        ff.append(a_idx * PEER_TOPK + b)

    half = PEER_TOPK // 2
    plan_a = [(0, PEER_TOPK)] + [(a, half) for a in range(1, 4)]
    plan_b = [(0, PEER_TOPK, 4)] + [(b, half, 4) for b in range(1, 3)]
    covered = [(a, b) for a, nb in plan_a for b in range(nb)] + [(a, b) for b, na, lo in plan_b for a in range(lo, na)]
    needed = {(a, b) for a in range(PEER_TOPK) for b in range(PEER_TOPK) if (a + 1) * (b + 1) <= PEER_TOPK}
    assert len(set(covered)) == len(covered) and needed <= set(covered)
    for a, nb in plan_a:
        rows_a(a, nb)
    for b, na, lo in plan_b:
        rows_b(b, na, lo)
    cand_s, cand_i, cand_f = (jnp.concatenate(p, axis=0) for p in (ss, ii, ff))
    best_s, best_i = _topk_rows(cand_s, PEER_TOPK, payload=cand_i, order=cand_f)
    e = jnp.exp(best_s - best_s[0:1, :])
    idx_ref[0] = best_i.astype(I32)
    gate_ref[0] = e / jnp.sum(e, axis=0, keepdims=True)


def peer_route(x, g, w_q, sub_keys, tm=128):
    n = x.shape[0]
    keys = sub_keys.astype(BF16)
    hn, idx, gate = pl.pallas_call(
        _route_body,
        grid=(n // tm, PEER_HEADS),
        in_specs=[pl.BlockSpec((tm, D_MODEL), lambda i, hd: (i, 0)),
                  pl.BlockSpec((1, D_MODEL), lambda i, hd: (0, 0)),
                  pl.BlockSpec((D_MODEL, PEER_DQ), lambda i, hd: (0, hd)),
                  pl.BlockSpec((1, 2, N_KEYS, PEER_DQ // 2), lambda i, hd: (hd, 0, 0, 0))],
        out_specs=[pl.BlockSpec((tm, D_MODEL), lambda i, hd: (i, 0)),
                   pl.BlockSpec((1, PEER_TOPK, tm), lambda i, hd: (hd, 0, i)),
                   pl.BlockSpec((1, PEER_TOPK, tm), lambda i, hd: (hd, 0, i))],
        out_shape=[jax.ShapeDtypeStruct((n, D_MODEL), F32),
                   jax.ShapeDtypeStruct((PEER_HEADS, PEER_TOPK, n), I32),
                   jax.ShapeDtypeStruct((PEER_HEADS, PEER_TOPK, n), F32)],
        compiler_params=_cp("parallel", "arbitrary"),
        name="peer_route",
    )(x, g.reshape(1, -1), w_q.astype(BF16), keys)
    idx = jnp.transpose(idx, (2, 0, 1)).reshape(n, PEER_HEADS * PEER_TOPK)
    gate = jnp.transpose(gate, (2, 0, 1)).reshape(n, PEER_HEADS * PEER_TOPK)
    return hn, idx, gate


HG = LANES
QW = H_A * HG
EVEN_CKV_OFF = D_CQ
EVEN_KR_OFF = D_CQ + D_C
EVEN_PR_OFF = EVEN_KR_OFF + HG
EVEN_W = EVEN_PR_OFF + RW_WIDTH


def _mla_prep_body(p_ref, cs_ref, sn_ref, gcq_ref, wuq_ref, gq_ref, gckv_ref, gkr_ref, wuk_ref, gk_ref, wuv_ref,
                   eq_ref, ek_ref, perm_ref, q_out, k_out, v_out, ckv_out, kr_out):
    cs, sn = cs_ref[...], sn_ref[...]

    def rope(x):
        return x * cs + _dot_sel(x, perm_ref[...]) * sn

    q = _dot(_rms(p_ref[:, :D_CQ], gcq_ref[...]).astype(BF16), wuq_ref[...])
    ckv = _rms(p_ref[:, EVEN_CKV_OFF:EVEN_CKV_OFF + D_C], gckv_ref[...])
    ckv_out[...] = ckv
    krr = p_ref[:, EVEN_KR_OFF:EVEN_KR_OFF + HG]
    ms = jnp.sum(krr * krr, axis=-1, keepdims=True) * (1.0 / DR_A)
    kr = rope(krr * lax.rsqrt(ms + EPS) * gkr_ref[...])
    kr_out[...] = kr
    ckv_b = ckv.astype(BF16)
    kraw = _dot(ckv_b, wuk_ref[...])
    v_out[...] = _dot(ckv_b, wuv_ref[...]).astype(BF16)
    for h in range(H_A):
        sl = slice(h * HG, (h + 1) * HG)
        qh = q[:, sl]
        qh = qh * lax.rsqrt(_dot_f32(qh * qh, eq_ref[...]) + EPS) * gq_ref[:, sl]
        q_out[:, sl] = rope(qh).astype(BF16)
        kh = kraw[:, sl]
        kh = kh * lax.rsqrt(_dot_f32(kh * kh, ek_ref[...]) + EPS) * gk_ref[:, sl]
        k_out[:, sl] = (kh + kr).astype(BF16)


def _head_groups(w, width, off=0):
    k = w.shape[0]
    out = jnp.zeros((k, H_A, HG), w.dtype)
    out = out.at[:, :, off:off + width].set(w.reshape(k, H_A, width))
    return out.reshape(k, H_A * HG)


def mla_prep(proj, pos, P, tm=256):
    n = proj.shape[0]
    half = DR_A // 2
    freqs = ROPE_BASE ** (-jnp.arange(half, dtype=F32) / half)
    ang = pos.astype(F32)[:, None] * freqs[None, :]
    cos, sin = jnp.cos(ang), jnp.sin(ang)
    one, zero = jnp.ones((n, DN_A), F32), jnp.zeros((n, DN_A), F32)
    pad = jnp.zeros((n, HG - DN_A - DR_A), F32)
    cs = jnp.concatenate([one, cos, cos, pad], axis=1)
    sn = jnp.concatenate([zero, -sin, sin, pad], axis=1)
    wuq = P['w_uq'].reshape(D_CQ, H_A, DN_A + DR_A)
    wuq = jnp.pad(wuq, ((0, 0), (0, 0), (0, HG - DN_A - DR_A))).reshape(D_CQ, QW).astype(BF16)
    gq = jnp.tile(jnp.concatenate([P['g_qn'], P['g_qr'], jnp.zeros((HG - DN_A - DR_A,), F32)]), H_A).reshape(1, QW)
    gkr = jnp.zeros((1, HG), F32).at[0, DN_A:DN_A + DR_A].set(P['g_kr'])
    wuk = _head_groups(P['w_uk'], DN_A).astype(BF16)
    wuv = _head_groups(P['w_uv'], DV_A).astype(BF16)
    gk = _head_groups(jnp.tile(P['g_kn'], H_A).reshape(1, -1), DN_A)
    eq = np.zeros((HG, HG), np.float32)
    eq[:DN_A, :DN_A] = 1.0 / DN_A
    eq[DN_A:DN_A + DR_A, DN_A:DN_A + DR_A] = 1.0 / DR_A
    ek = np.zeros((HG, HG), np.float32)
    ek[:DN_A, :DN_A] = 1.0 / DN_A
    perm = np.zeros((HG, HG), np.float32)
    j = np.arange(half)
    perm[DN_A + half + j, DN_A + j] = 1.0
    perm[DN_A + j, DN_A + half + j] = 1.0
    row = lambda a: pl.BlockSpec(a.shape, lambda i: (0, 0))
    consts = [P['g_cq'].reshape(1, -1), wuq, gq, P['g_ckv'].reshape(1, -1), gkr, wuk, gk, wuv,
              jnp.asarray(eq, BF16), jnp.asarray(ek, BF16), jnp.asarray(perm, BF16)]
    return pl.pallas_call(
        _mla_prep_body,
        grid=(n // tm,),
        in_specs=[pl.BlockSpec((tm, EVEN_PR_OFF), lambda i: (i, 0)),
                  pl.BlockSpec((tm, HG), lambda i: (i, 0)),
                  pl.BlockSpec((tm, HG), lambda i: (i, 0))] + [row(c) for c in consts],
        out_specs=[pl.BlockSpec((tm, QW), lambda i: (i, 0))] * 3
        + [pl.BlockSpec((tm, D_C), lambda i: (i, 0)), pl.BlockSpec((tm, HG), lambda i: (i, 0))],
        out_shape=[jax.ShapeDtypeStruct((n, QW), BF16)] * 3
        + [jax.ShapeDtypeStruct((n, D_C), F32), jax.ShapeDtypeStruct((n, HG), F32)],
        compiler_params=_cp("parallel"),
        name="mla_prep",
    )(proj, cs, sn, *consts)


def _mla_prompt_body(q_ref, k_ref, v_ref, o_ref, *, tq):
    qi = pl.program_id(1)
    row = lax.broadcasted_iota(I32, (tq, tq), 0)
    col = lax.broadcasted_iota(I32, (tq, tq), 1)
    for h in range(H_A):
        sl = slice(h * HG, (h + 1) * HG)
        q = q_ref[:, sl]

        def body(kb, carry):
            m, l, acc = carry
            off = pl.multiple_of(kb * tq, tq)
            s = _dot_nt(q, k_ref[pl.ds(off, tq), sl]) * MLA_SCALE
            s = jnp.where(col + kb * tq <= row + qi * tq, s, -jnp.inf)
            m_new = jnp.maximum(m, jnp.max(s, axis=-1, keepdims=True))
            alpha = jnp.exp(m - m_new)
            p = jnp.exp(s - m_new)
            l = l * alpha + jnp.sum(p, axis=-1, keepdims=True)
            acc = acc * alpha + _dot(p.astype(BF16), v_ref[pl.ds(off, tq), sl])
            return m_new, l, acc

        init = (jnp.full((tq, 1), -jnp.inf, F32), jnp.zeros((tq, 1), F32), jnp.zeros((tq, HG), F32))
        _, l, acc = lax.fori_loop(0, qi + 1, body, init)
        o_ref[:, h * DV_A:(h + 1) * DV_A] = (acc / l)[:, :DV_A].astype(BF16)


def mla_prompt_attention(qc, kc, vc, b, t, tq=256):
    nq = t // tq
    return pl.pallas_call(
        functools.partial(_mla_prompt_body, tq=tq),
        grid=(b, nq),
        in_specs=[pl.BlockSpec((tq, QW), lambda bi, qi: (bi * nq + qi, 0)),
                  pl.BlockSpec((t, QW), lambda bi, qi: (bi, 0)),
                  pl.BlockSpec((t, QW), lambda bi, qi: (bi, 0))],
        out_specs=pl.BlockSpec((tq, H_A * DV_A), lambda bi, qi: (bi * nq + qi, 0)),
        out_shape=jax.ShapeDtypeStruct((b * t, H_A * DV_A), BF16),
        compiler_params=_cp("parallel", "arbitrary"),
        name="mla_prompt_attention",
    )(qc, kc, vc)


PAGES_PER_STEP = 8
KV_STEPS = N_PAGES // PAGES_PER_STEP


def _mla_sample_body(pt_ref, *refs, t_new):
    npg = PAGES_PER_STEP
    ckv_pages, kr_pages = refs[:npg], refs[npg:2 * npg]
    (ckv_new, kr_new, qn_ref, qr_ref, wuk_ref, e_ref, wuv_ref, hm_ref, o_ref,
     m_scr, l_scr, acc_scr) = refs[2 * npg:]
    j = pl.program_id(1)

    def process(ckv, kr, mask):
        ckv_b = ckv.astype(BF16)
        kraw = _dot(ckv_b, wuk_ref[...])
        ms = _dot_f32(kraw * kraw, e_ref[...]) * (1.0 / DN_A)
        s = (_dot(kraw.astype(BF16), qn_ref[0]) * lax.rsqrt(ms + EPS)
             + _dot_tn(kr.astype(BF16), qr_ref[0])) * MLA_SCALE
        if mask is not None:
            s = jnp.where(mask, s, -jnp.inf)
        m = m_scr[...]
        m_new = jnp.maximum(m, jnp.max(s, axis=0, keepdims=True))
        alpha = jnp.exp(m - m_new)
        p = jnp.exp(s - m_new)
        l_scr[...] = l_scr[...] * alpha + jnp.sum(p, axis=0, keepdims=True)
        acc_scr[...] = acc_scr[...] * alpha + _dot_tn(ckv_b, p.astype(BF16))
        m_scr[...] = m_new

    @pl.when(j == 0)
    def _():
        m_scr[...] = jnp.full(m_scr.shape, -jnp.inf, F32)
        l_scr[...] = jnp.zeros(l_scr.shape, F32)
        acc_scr[...] = jnp.zeros(acc_scr.shape, F32)
        nk, nq = PAGE_SIZE, H_A * t_new
        key = lax.broadcasted_iota(I32, (nk, nq), 0)
        qt = lax.broadcasted_iota(I32, (nk, nq), 1) % t_new
        process(ckv_new[0], kr_new[0], key <= qt)

    process(jnp.concatenate([r[0] for r in ckv_pages], axis=0), jnp.concatenate([r[0] for r in kr_pages], axis=1), None)

    @pl.when(j == KV_STEPS - 1)
    def _():
        o_lat =(acc_scr[...] / l_scr[...]).astype(BF16)
        full = _dot_tn(o_lat, wuv_ref[...]) * hm_ref[...]
        out = full[0:t_new]
        for h in range(1, H_A):
            out = out + full[h * t_new:(h + 1) * t_new]
        o_ref[0] = out.astype(BF16)


def mla_sample_attention(page_table, cache_ckv, cache_kr, ckv_new, kr_new, qn, qr, P):
    b, t_new = qn.shape[:2]
    nq = H_A * t_new
    eye = jnp.eye(H_A, dtype=BF16)
    qn_g = (qn.astype(F32) * P['g_kn']).astype(BF16)
    qn_bd = jnp.einsum('bthd,hg->bhdgt', qn_g, eye).reshape(b, H_A * DN_A, nq)
    qr_m = jnp.transpose(qr, (0, 3, 2, 1)).reshape(b, DR_A, nq)
    pad = lambda a: jnp.pad(a, ((0, 0), (0, PAGE_SIZE - t_new), (0, 0)))
    keys_last = lambda a: jnp.swapaxes(a, 1, 2)
    cache_kr = keys_last(cache_kr)
    hm = (np.arange(nq)[:, None] // t_new == np.arange(H_A * DV_A)[None, :] // DV_A).astype(np.float32)
    e_head = (np.arange(H_A * DN_A)[:, None] // DN_A == np.arange(nq)[None, :] // t_new).astype(np.float32)
    npg = PAGES_PER_STEP
    page = lambda p, s: pl.BlockSpec((1,) + s, lambda bi, j, pt: (pt[bi * N_PAGES + j * npg + p], 0, 0))
    per_b = lambda s: pl.BlockSpec((1,) + s, lambda bi, j, pt: (bi, 0, 0))
    const = lambda a: pl.BlockSpec(a.shape, lambda bi, j, pt: (0, 0))
    consts = [P['w_uk'].astype(BF16), jnp.asarray(e_head, BF16), P['w_uv'].astype(BF16), jnp.asarray(hm)]
    grid_spec = pltpu.PrefetchScalarGridSpec(
        num_scalar_prefetch=1,
        grid=(b, KV_STEPS),
        in_specs=[page(p, (PAGE_SIZE, D_C)) for p in range(npg)] + [page(p, (DR_A, PAGE_SIZE)) for p in range(npg)]
        + [per_b((PAGE_SIZE, D_C)), per_b((DR_A, PAGE_SIZE)), per_b((H_A * DN_A, nq)), per_b((DR_A, nq))]
        + [const(c) for c in consts],
        out_specs=pl.BlockSpec((1, t_new, H_A * DV_A), lambda bi, j, pt: (bi, 0, 0)),
        scratch_shapes=[pltpu.VMEM((1, nq), F32), pltpu.VMEM((1, nq), F32), pltpu.VMEM((D_C, nq), F32)],
    )
    out = pl.pallas_call(
        functools.partial(_mla_sample_body, t_new=t_new),
        grid_spec=grid_spec,
        out_shape=jax.ShapeDtypeStruct((b, t_new, H_A * DV_A), BF16),
        compiler_params=_cp("parallel", "arbitrary"),
        name="mla_sample_attention",
    )(page_table.reshape(-1), *([cache_ckv] * npg), *([cache_kr] * npg), pad(ckv_new), keys_last(pad(kr_new)),
      qn_bd, qr_m, *consts)
    return out.reshape(b * t_new, H_A * DV_A)


RW_LORA_OFF = 3 * HB


def _rwkv_prep_body(pr_ref, sh_ref, mu_ref, w0_ref, wup_ref, a0_ref, aup_ref, gup_ref, kk_ref, ka_ref, rk_ref, e_ref,
                    nkk_o, wr_o, w_o, kka_o, k2_o, v_o, c1_o, c2_o, g_o, bonus_o):
    pr = pr_ref[...]
    xs = pr + mu_ref[...] * (sh_ref[...] - pr)
    r, k, v = xs[:, :HB], xs[:, HB:2 * HB], xs[:, 2 * HB:3 * HB]
    xwa = xs[:, RW_LORA_OFF:RW_LORA_OFF + D_WL + D_AL]
    xg = xs[:, RW_LORA_OFF + D_WL + D_AL:]
    w = jnp.exp(-RWKV_DECAY_SCALE * _sigmoid(w0_ref[...] + _dot(jnp.tanh(xwa).astype(BF16), wup_ref[...])))
    a = _sigmoid(a0_ref[...] + _dot(xwa.astype(BF16), aup_ref[...]))
    g_o[...] = _dot(_sigmoid(xg).astype(BF16), gup_ref[...])
    kk = k * kk_ref[...]
    kk = kk * lax.rsqrt(jnp.maximum(_dot_f32(kk * kk, e_ref[...]), 1e-12))
    k2 = k * (1.0 + (a - 1.0) * ka_ref[...])
    kka = kk * a
    nkk_o[...] = -kk
    wr_o[...] = w * r
    w_o[...] = w
    kka_o[...] = kka
    k2_o[...] = k2
    v_o[...] = v
    c1_o[...] = _dot_f32(kka * r, e_ref[...])
    c2_o[...] = _dot_f32(k2 * r, e_ref[...])
    bonus_o[...] = _dot_f32(r * k2 * rk_ref[...], e_ref[...]) * v


def rwkv_prep(proj, shifted, P, tm=256):
    n = proj.shape[0]
    nblk = EVEN_PR_OFF // RW_WIDTH
    assert EVEN_PR_OFF % LANES == 0
    zw = jnp.zeros((D_WL, HB), F32)
    consts = [P['mu'].reshape(1, -1), P['w0'].reshape(1, -1),
              jnp.concatenate([P['w_up'], zw]).astype(BF16), P['a0'].reshape(1, -1),
              jnp.concatenate([zw, P['a_up']]).astype(BF16), P['g_up'].astype(BF16),
              P['k_k'].reshape(1, -1), P['k_a'].reshape(1, -1), P['r_k'].reshape(1, -1),
              jnp.asarray(_blockdiag_ones(HB, HS_B), BF16)]
    pr = lax.slice_in_dim(proj, EVEN_PR_OFF, EVEN_W, axis=1)
    return pl.pallas_call(
        _rwkv_prep_body,
        grid=(n // tm,),
        in_specs=[pl.BlockSpec((tm, RW_WIDTH), lambda i: (i, 0)), pl.BlockSpec((tm, RW_WIDTH), lambda i: (i, 0))]
        + [pl.BlockSpec(c.shape, lambda i: (0, 0)) for c in consts],
        out_specs=[pl.BlockSpec((tm, HB), lambda i: (i, 0))] * 10,
        out_shape=[jax.ShapeDtypeStruct((n, HB), F32)] * 10,
        compiler_params=_cp("parallel"),
        name="rwkv_prep",
    )(pr, shifted, *consts)


def _rwkv_scan_body(nkk_r, wr_r, w_r, kka_r, k2_r, v_r, c1_r, c2_r, s0_ref, e_ref, d_ref, o_ref, sf_ref, s_scr,
                    *, nb, tc):
    c = pl.program_id(1)

    @pl.when(c == 0)
    def _():
        s_scr[...] = s0_ref[...]

    dmask = d_ref[...]
    half = HB // 2

    def step(t, carry):
        for b in range(nb):
            row = lambda ref: ref[b, pl.ds(t, 1), :]
            s = s_scr[b]
            stacked = jnp.concatenate([s * row(nkk_r), s * row(wr_r), dmask * row(v_r)], axis=0)
            hi, lo = _split2(stacked)
            seg = jnp.concatenate(
                [_dot(hi[:, i * half:(i + 1) * half], e_ref[...]) + _dot(lo[:, i * half:(i + 1) * half], e_ref[...])
                 for i in range(2)], axis=1)
            sa, t2, vb = seg[:HS_B], seg[HS_B:2 * HS_B], seg[2 * HS_B:]
            s_scr[b] = s * row(w_r) + sa * row(kka_r) + vb * row(k2_r)
            ob = t2 + sa * row(c1_r) + vb * row(c2_r)
            o_ref[b, pl.ds(t, 1), :] = jnp.sum(ob * dmask, axis=0, keepdims=True)
        return carry

    lax.fori_loop(0, tc, step, 0)

    @pl.when(c == pl.num_programs(1) - 1)
    def _():
        sf_ref[...] = s_scr[...]


def rwkv_scan(seqs, state0, b, t, nb, tc):
    s0 = jnp.transpose(state0, (0, 2, 1, 3)).reshape(b, HS_B, HB)
    e = jnp.asarray(_blockdiag_ones(HB // 2, HS_B), BF16)
    dmask = jnp.asarray((np.arange(HS_B)[:, None] == (np.arange(HB)[None, :] % HS_B)).astype(np.float32))
    seq_spec = pl.BlockSpec((nb, tc, HB), lambda i, c: (i, c, 0))
    st_spec = pl.BlockSpec((nb, HS_B, HB), lambda i, c: (i, 0, 0))
    o, sf = pl.pallas_call(
        functools.partial(_rwkv_scan_body, nb=nb, tc=tc),
        grid=(b // nb, t // tc),
        in_specs=[seq_spec] * 8 + [st_spec, pl.BlockSpec(e.shape, lambda i, c: (0, 0)),
                                   pl.BlockSpec(dmask.shape, lambda i, c: (0, 0))],
        out_specs=[seq_spec, st_spec],
        out_shape=[jax.ShapeDtypeStruct((b, t, HB), F32), jax.ShapeDtypeStruct((b, HS_B, HB), F32)],
        scratch_shapes=[pltpu.VMEM((nb, HS_B, HB), F32)],
        compiler_params=_cp("parallel", "arbitrary"),
        name="rwkv_scan",
    )(*[a.reshape(b, t, HB) for a in seqs], s0, e, dmask)
    sf = jnp.transpose(sf.reshape(b, HS_B, H_B, HS_B), (0, 2, 1, 3))
    return o.reshape(b * t, HB), sf


def _rwkv_post_body(o_ref, g_ref, bonus_ref, lng_ref, lnb_ref, e_ref, out_ref):
    o = o_ref[...]
    mu = _dot_f32(o, e_ref[...]) * (1.0 / HS_B)
    d = o - mu
    var = _dot_f32(d * d, e_ref[...]) * (1.0 / HS_B)
    y = d * lax.rsqrt(var + RWKV_LN_EPS) * lng_ref[...] + lnb_ref[...]
    out_ref[...] = ((y + bonus_ref[...]) * g_ref[...]).astype(BF16)


def rwkv_post(o, g, bonus, P, tm=256):
    n = o.shape[0]
    e = jnp.asarray(_blockdiag_ones(HB, HS_B), BF16)
    blk = pl.BlockSpec((tm, HB), lambda i: (i, 0))
    row = pl.BlockSpec((1, HB), lambda i: (0, 0))
    return pl.pallas_call(
        _rwkv_post_body,
        grid=(n // tm,),
        in_specs=[blk, blk, blk, row, row, pl.BlockSpec(e.shape, lambda i: (0, 0))],
        out_specs=blk,
        out_shape=jax.ShapeDtypeStruct((n, HB), BF16),
        compiler_params=_cp("parallel"),
        name="rwkv_post",
    )(o, g, bonus, P['ln_g'].reshape(1, -1), P['ln_b'].reshape(1, -1), e)


def rwkv_mix(proj, prev, state0, b, t, P, nb, tc, tm):
    pr = lax.slice_in_dim(proj, EVEN_PR_OFF, EVEN_W, axis=1).reshape(b, t, RW_WIDTH)
    shifted = jnp.concatenate([prev[:, None, :], pr[:, :-1, :]], axis=1).reshape(b * t, RW_WIDTH)
    outs = rwkv_prep(proj, shifted, P, tm)
    o, s_new = rwkv_scan(outs[:8], state0, b, t, nb, tc)
    return rwkv_post(o, outs[8], outs[9], P, tm), pr[:, -1, :], s_new


ODD_Z_OFF = 3 * HC
ODD_XBC_OFF = ODD_Z_OFF + D_INNER
ODD_DT_OFF = ODD_XBC_OFF + CONV_DIM
ODD_W = ODD_DT_OFF + LANES
SB_SCALE = HD_C ** -0.5


def _sb_weights(z, mask, suffix_of, run):
    lneg = -_softplus(z)
    if mask is not None:
        lneg = jnp.where(mask, lneg, 0.0)
    w = jnp.exp(z + lneg + suffix_of(lneg) + run)
    if mask is not None:
        w = jnp.where(mask, w, 0.0)
    return w, lneg


def _sb_prompt_body(q_ref, k_ref, v_ref, u_ref, o_ref, *, tq):
    qi = pl.program_id(1)
    row = lax.broadcasted_iota(I32, (tq, tq), 0)
    col = lax.broadcasted_iota(I32, (tq, tq), 1)
    for h in range(H_C):
        sl = slice(h * HD_C, (h + 1) * HD_C)
        q = q_ref[:, sl].astype(BF16)

        def body(i, carry):
            run, acc = carry
            kb = qi - i
            off = pl.multiple_of(kb * tq, tq)
            z = _dot_nt(q, k_ref[pl.ds(off, tq), sl].astype(BF16)) * SB_SCALE
            mask = col + kb * tq < row + qi * tq
            w, lneg = _sb_weights(z, mask, lambda l: _dot_f32(l, u_ref[...]), run)
            acc = acc + _dot(w.astype(BF16), v_ref[pl.ds(off, tq), sl].astype(BF16))
            return run + jnp.sum(lneg, axis=-1, keepdims=True), acc

        _, acc = lax.fori_loop(0, qi + 1, body, (jnp.zeros((tq, 1), F32), jnp.zeros((tq, HD_C), F32)))
        o_ref[:, sl] = acc.astype(BF16)


def sb_prompt_attention(proj, b, t, tq=256):
    nq = t // tq
    u = jnp.asarray(np.tril(np.ones((tq, tq), np.float32), -1), BF16)
    return pl.pallas_call(
        functools.partial(_sb_prompt_body, tq=tq),
        grid=(b, nq),
        in_specs=[pl.BlockSpec((tq, HC), lambda bi, qi: (bi * nq + qi, 0)),
                  pl.BlockSpec((t, HC), lambda bi, qi: (bi, 1)),
                  pl.BlockSpec((t, HC), lambda bi, qi: (bi, 2)),
                  pl.BlockSpec(u.shape, lambda bi, qi: (0, 0))],
        out_specs=pl.BlockSpec((tq, HC), lambda bi, qi: (bi * nq + qi, 0)),
        out_shape=jax.ShapeDtypeStruct((b * t, HC), BF16),
        compiler_params=_cp("parallel", "arbitrary"),
        name="sb_prompt_attention",
    )(proj, proj, proj, u)


def _kv_heads_body(k_ref, v_ref, eye_ref, kt_ref, vt_ref):
    for src, dst in ((k_ref, kt_ref), (v_ref, vt_ref)):
        x = src[...]
        hi = x.astype(BF16)
        r1 = x - hi.astype(F32)
        mid = r1.astype(BF16)
        lo = (r1 - mid.astype(F32)).astype(BF16)
        for h in range(H_C):
            sl = slice(h * HD_C, (h + 1) * HD_C)
            eye = eye_ref[...]
            dst[0, h] = (_dot_nt(eye, hi[:, sl]) + _dot_nt(eye, mid[:, sl])) + _dot_nt(eye, lo[:, sl])


def sb_kv_state(proj, b, t, tq=512):
    tq = min(tq, t)
    assert t % tq == 0
    nq = t // tq
    eye = jnp.eye(HD_C, dtype=BF16)
    out = pl.BlockSpec((1, H_C, HD_C, tq), lambda bi, qi: (bi, 0, 0, qi))
    kt, vt = pl.pallas_call(
        _kv_heads_body,
        grid=(b, nq),
        in_specs=[pl.BlockSpec((tq, HC), lambda bi, qi: (bi * nq + qi, 1)),
                  pl.BlockSpec((tq, HC), lambda bi, qi: (bi * nq + qi, 2)),
                  pl.BlockSpec(eye.shape, lambda bi, qi: (0, 0))],
        out_specs=[out, out],
        out_shape=[jax.ShapeDtypeStruct((b, H_C, HD_C, t), F32)] * 2,
        compiler_params=_cp("parallel", "parallel"),
        name="sb_kv_state",
    )(proj, proj, eye)
    return jnp.transpose(kt, (0, 3, 1, 2)), jnp.transpose(vt, (0, 3, 1, 2))


def _sb_sample_body(pt_ref, *refs, t_new):
    npg = PAGES_PER_STEP
    k_pages, v_pages = refs[:npg], refs[npg:2 * npg]
    k_new, v_new, q_ref, u_ref, o_ref, q_scr, run_scr, acc_scr = refs[2 * npg:]
    j = pl.program_id(1)
    nq = H_C * t_new
    row_head = lax.broadcasted_iota(I32, (nq, PAGE_SIZE), 0) // t_new
    hsl = lambda h: slice(h * HD_C, (h + 1) * HD_C)

    def process(k_of, v_of, mask, dot_k, dot_v):
        z = dot_k(q_scr[0], k_of(0))
        for h in range(1, H_C):
            z = z + dot_k(q_scr[h], k_of(h))
        w, lneg = _sb_weights(z * SB_SCALE, mask, lambda l: _dot_f32(l, u_ref[...]), run_scr[...])
        acc = acc_scr[...]
        for h in range(H_C):
            acc = acc + dot_v(jnp.where(row_head == h, w, 0.0).astype(BF16), v_of(h))
        acc_scr[...] = acc
        run_scr[...] += jnp.sum(lneg, axis=1, keepdims=True)

    @pl.when(j == 0)
    def _():
        run_scr[...] = jnp.zeros(run_scr.shape, F32)
        acc_scr[...] = jnp.zeros(acc_scr.shape, F32)
        q = q_ref[0]
        for h in range(H_C):
            rows = [jnp.zeros((t_new, HD_C), F32)] * H_C
            rows[h] = q[:, hsl(h)]
            q_scr[h] = jnp.concatenate(rows, axis=0).astype(BF16)
        fill = jnp.zeros((PAGE_SIZE - t_new, HC), F32)
        k_pad = jnp.concatenate([k_new[0], fill], axis=0).astype(BF16)
        v_pad = jnp.concatenate([v_new[0], fill], axis=0).astype(BF16)
        key = lax.broadcasted_iota(I32, (nq, PAGE_SIZE), 1)
        qt = lax.broadcasted_iota(I32, (nq, PAGE_SIZE), 0) % t_new
        process(lambda h: k_pad[:, hsl(h)], lambda h: v_pad[:, hsl(h)], key < qt, _dot_nt, _dot)

    for p in range(npg):
        process(lambda h: k_pages[p][0, h].astype(BF16), lambda h: v_pages[p][0, h].astype(BF16), None, _dot, _dot_nt)

    @pl.when(j == KV_STEPS - 1)
    def _():
        acc = acc_scr[...]
        for h in range(H_C):
            o_ref[0, :, h * HD_C:(h + 1) * HD_C] = acc[h * t_new:(h + 1) * t_new].astype(BF16)


def sb_sample_attention(page_table, cache_k, cache_v, proj, b, t_new):
    proj = proj.reshape(b, t_new, ODD_W)
    nq = H_C * t_new
    keys_last = lambda a: jnp.transpose(a, (0, 2, 3, 1))
    cache_k, cache_v = keys_last(cache_k), keys_last(cache_v)
    u = jnp.asarray(np.tril(np.ones((PAGE_SIZE, PAGE_SIZE), np.float32), -1), BF16)
    npg = PAGES_PER_STEP
    page = lambda p: pl.BlockSpec(
        (1, H_C, HD_C, PAGE_SIZE), lambda bi, j, pt: (pt[bi * N_PAGES + (N_PAGES - 1 - (j * npg + p))], 0, 0, 0))
    col = lambda c: pl.BlockSpec((1, t_new, HC), lambda bi, j, pt: (bi, 0, c))
    grid_spec = pltpu.PrefetchScalarGridSpec(
        num_scalar_prefetch=1,
        grid=(b, KV_STEPS),
        in_specs=[page(p) for p in range(npg)] * 2
        + [col(1), col(2), col(0), pl.BlockSpec(u.shape, lambda bi, j, pt: (0, 0))],
        out_specs=col(0),
        scratch_shapes=[pltpu.VMEM((H_C, nq, HD_C), BF16), pltpu.VMEM((nq, 1), F32), pltpu.VMEM((nq, HD_C), F32)],
    )
    out = pl.pallas_call(
        functools.partial(_sb_sample_body, t_new=t_new),
        grid_spec=grid_spec,
        out_shape=jax.ShapeDtypeStruct((b, t_new, HC), BF16),
        compiler_params=_cp("parallel", "arbitrary"),
        name="sb_sample_attention",
    )(page_table.reshape(-1), *([cache_k] * npg), *([cache_v] * npg), proj, proj, proj, u)
    return out.reshape(b * t_new, HC)


TAIL = 8


def _ssd_body(z_ref, xbc_ref, dt_ref, cbuf_ref, h0_ref, cw_ref, cb_ref, dtb_ref, alog_ref, drow_ref, gn_ref, tri_ref,
              y_ref, hf_ref, h_scr, tail_scr, y_scr, *, L):
    c = pl.program_id(1)
    mm = (lambda x: x.astype(BF16)) if L >= 16 else (lambda x: x.astype(BF16).astype(F32))

    @pl.when(c == 0)
    def _():
        h_scr[...] = h0_ref[0]
        tail_scr[...] = jnp.zeros(tail_scr.shape, F32)
        tail_scr[TAIL - (CONV_W - 1):, :] = cbuf_ref[0]

    xbc = xbc_ref[...]
    ext = jnp.concatenate([tail_scr[...], xbc], axis=0)
    conv = cb_ref[...]
    for i in range(CONV_W):
        s = TAIL - (CONV_W - 1) + i
        conv = conv + cw_ref[i:i + 1, :] * ext[s:s + L]
    tail_scr[...] = xbc[L - TAIL:]
    xc = conv * _sigmoid(conv)
    dt = _softplus(dt_ref[...] + dtb_ref[...])
    a = -jnp.exp(alog_ref[...])
    lane = lax.broadcasted_iota(I32, (1, LANES), 1)
    dta = jnp.where(lane < H_D, dt * a, 0.0)
    hi = dta.astype(BF16)
    r1 = dta - hi.astype(F32)
    mid = r1.astype(BF16)
    lo = (r1 - mid.astype(F32)).astype(BF16)
    tri = tri_ref[...]
    acs = (_dot(tri, mm(hi)) + _dot(tri, mm(mid))) + _dot(tri, mm(lo))
    hi, mid, lo = acs.astype(BF16), None, None
    r1 = acs - hi.astype(F32)
    mid = r1.astype(BF16)
    lo = (r1 - mid.astype(F32)).astype(BF16)
    row = lax.broadcasted_iota(I32, (L, L), 0)
    col = lax.broadcasted_iota(I32, (L, L), 1)
    lane_l = lax.broadcasted_iota(I32, (L, LANES), 1)
    acs_last = acs[L - 1:L, :]
    cbs = []
    for g in range(G_D):
        bg = xc[:, D_INNER + g * N_D:D_INNER + (g + 1) * N_D]
        cg = xc[:, D_INNER + G_D * N_D + g * N_D:D_INNER + G_D * N_D + (g + 1) * N_D]
        cbs.append((bg, cg, _dot_nt(mm(cg), mm(bg))))
    for h in range(H_D):
        bg, cg, cb = cbs[h // (H_D // G_D)]
        sel = mm((lane_l == h).astype(F32))
        acs_row = (_dot_nt(sel, mm(hi)) + _dot_nt(sel, mm(mid))) + _dot_nt(sel, mm(lo))
        acs_col = acs[:, h:h + 1]
        lmat = jnp.exp(jnp.where(col <= row, acs_col - acs_row, -jnp.inf))
        xh = xc[:, h * P_D:(h + 1) * P_D]
        xdt = xh * dt[:, h:h + 1]
        hprev = h_scr[h]
        y = _dot(mm(cb * lmat), mm(xdt)) + _dot_nt(mm(cg * jnp.exp(acs_col)), mm(hprev))
        y_scr[:, h * P_D:(h + 1) * P_D] = y + drow_ref[:, h * P_D:(h + 1) * P_D] * xh
        last = acs_last[:, h:h + 1]
        bw = bg * jnp.exp(last - acs_col)
        if L >= 16:
            st = _dot_tn(mm(xdt), mm(bw))
        else:
            xr, br = xdt - mm(xdt), bw - mm(bw)
            st = (_dot_tn(mm(xdt), mm(bw)) + _dot_tn(mm(xdt), mm(br))) + _dot_tn(mm(xr), mm(bw))
        h_scr[h] = hprev * jnp.exp(last) + st
    z = z_ref[...]
    y_ref[...] = _rms(y_scr[...] * (z * _sigmoid(z)), gn_ref[...]).astype(BF16)

    @pl.when(c == pl.num_programs(1) - 1)
    def _():
        hf_ref[0] = h_scr[...]


def ssd_mix(proj, conv_buf, state0, b, t, P):
    L = SSD_CHUNK if t % SSD_CHUNK == 0 else t
    assert L % TAIL == 0
    nc = t // L
    pad_h = lambda v: jnp.pad(v.reshape(1, -1), ((0, 0), (0, LANES - H_D)))
    consts = [P['conv_w'], P['conv_b'].reshape(1, -1), pad_h(P['dt_bias']), pad_h(P['a_log']),
              jnp.repeat(P['d'], P_D).reshape(1, -1), P['g_norm'].reshape(1, -1)]
    tri_np = np.tril(np.ones((L, L), np.float32))
    tri = jnp.asarray(tri_np, BF16 if L >= 16 else F32)
    zc, xc, dc = ODD_Z_OFF // D_INNER, ODD_XBC_OFF // CONV_DIM, ODD_DT_OFF // LANES
    assert ODD_Z_OFF % D_INNER == 0 and ODD_XBC_OFF % CONV_DIM == 0 and ODD_DT_OFF % LANES == 0
    y, hf = pl.pallas_call(
        functools.partial(_ssd_body, L=L),
        grid=(b, nc),
        in_specs=[pl.BlockSpec((L, D_INNER), lambda bi, c: (bi * nc + c, zc)),
                  pl.BlockSpec((L, CONV_DIM), lambda bi, c: (bi * nc + c, xc)),
                  pl.BlockSpec((L, LANES), lambda bi, c: (bi * nc + c, dc)),
                  pl.BlockSpec((1, CONV_W - 1, CONV_DIM), lambda bi, c: (bi, 0, 0)),
                  pl.BlockSpec((1, H_D, P_D, N_D), lambda bi, c: (bi, 0, 0, 0))]
        + [pl.BlockSpec(cst.shape, lambda bi, c: (0, 0)) for cst in consts]
        + [pl.BlockSpec(tri.shape, lambda bi, c: (0, 0))],
        out_specs=[pl.BlockSpec((L, D_INNER), lambda bi, c: (bi * nc + c, 0)),
                   pl.BlockSpec((1, H_D, P_D, N_D), lambda bi, c: (bi, 0, 0, 0))],
        out_shape=[jax.ShapeDtypeStruct((b * t, D_INNER), BF16), jax.ShapeDtypeStruct((b, H_D, P_D, N_D), F32)],
        scratch_shapes=[pltpu.VMEM((H_D, P_D, N_D), F32), pltpu.VMEM((TAIL, CONV_DIM), F32),
                        pltpu.VMEM((L, D_INNER), F32)],
        compiler_params=_cp("parallel", "arbitrary"),
        name="ssd_mix",
    )(proj, proj, proj, conv_buf, state0, *consts, tri)
    return y, hf


SC_L = 16
SC_TB = 16
GELU_C = math.sqrt(2.0 / math.pi)


def _sc_gelu(a):
    y = GELU_C * (a + 0.044715 * (a * a * a))
    return 0.5 * a * (2.0 - 2.0 / (jnp.exp(2.0 * y) + 1.0))


def peer_experts(hn, idx, gate, u_tab, v_tab, x):
    n = hn.shape[0]
    info = plsc.get_sparse_core_info()
    nc, ns = info.num_cores, info.num_subcores
    assert info.num_lanes == SC_L and PEER_TOPK == SC_L
    nw = nc * ns
    per_w = n // nw
    assert n % (nw * SC_TB) == 0
    n_blk = per_w // SC_TB
    n_chunk = SC_TB * PEER_HEADS
    n_vec = D_MODEL // SC_L
    mesh = plsc.VectorSubcoreMesh(core_axis_name="c", subcore_axis_name="s")

    @functools.partial(
        pl.kernel, mesh=mesh, out_type=jax.ShapeDtypeStruct((n, D_MODEL), F32),
        scratch_types=[pltpu.VMEM((SC_TB, D_MODEL), F32), pltpu.VMEM((SC_TB, D_MODEL), F32),
                       pltpu.VMEM((SC_TB, PEER_HEADS, SC_L), I32), pltpu.VMEM((SC_TB, PEER_HEADS * SC_L), F32),
                       pltpu.VMEM((2, SC_L, D_MODEL), F32), pltpu.VMEM((2, SC_L, D_MODEL), F32),
                       pltpu.SemaphoreType.DMA((2,)), pltpu.SemaphoreType.DMA((2,))],
        compiler_params=pltpu.CompilerParams(needs_layout_passes=False),
        name="peer_experts")
    def run(hn_hbm, idx_hbm, gate_hbm, u_hbm, v_hbm, x_hbm, out_hbm, hnv, outv, idxv, gv, ubuf, vbuf, usem, vsem):
        wid = lax.axis_index("s") * nc + lax.axis_index("c")
        lanes = lax.iota(I32, SC_L)

        def copies(q, slot):
            tl, hd = q // PEER_HEADS, q % PEER_HEADS
            rows = idxv.at[tl, hd]
            return (pltpu.make_async_copy(u_hbm.at[rows], ubuf.at[slot], usem.at[slot]),
                    pltpu.make_async_copy(v_hbm.at[rows], vbuf.at[slot], vsem.at[slot]))

        def start(q, slot):
            for cp in copies(q, slot):
                cp.start()

        def compute(q, slot):
            tl, hd = q // PEER_HEADS, q % PEER_HEADS
            cu, cv = copies(q, slot)
            cu.wait()

            def ubody(c, accs):
                off = pl.multiple_of(c * SC_L, SC_L)
                xc = hnv[tl, pl.ds(off, SC_L)]
                return tuple(accs[j] + ubuf[slot, j, pl.ds(off, SC_L)] * xc for j in range(SC_L))

            zero = jnp.zeros((SC_L,), F32)
            accs = lax.fori_loop(0, n_vec, ubody, (zero,) * SC_L)
            acts = zero
            for j in range(SC_L):
                acts = jnp.where(lanes == j, jnp.sum(accs[j]), acts)
            coef = gv[tl, pl.ds(pl.multiple_of(hd * SC_L, SC_L), SC_L)] * _sc_gelu(acts)
            cj = [jnp.sum(jnp.where(lanes == j, coef, 0.0)) for j in range(SC_L)]
            cv.wait()

            def vbody(c, carry):
                off = pl.multiple_of(c * SC_L, SC_L)
                o = outv[tl, pl.ds(off, SC_L)]
                for j in range(SC_L):
                    o = o + cj[j] * vbuf[slot, j, pl.ds(off, SC_L)]
                outv[tl, pl.ds(off, SC_L)] = o
                return carry

            lax.fori_loop(0, n_vec, vbody, 0)

        def block(bi, carry):
            t0 = pl.multiple_of(wid * per_w + bi * SC_TB, SC_TB)
            pltpu.sync_copy(hn_hbm.at[pl.ds(t0, SC_TB)], hnv)
            pltpu.sync_copy(x_hbm.at[pl.ds(t0, SC_TB)], outv)
            pltpu.sync_copy(idx_hbm.at[pl.ds(t0, SC_TB)], idxv)
            pltpu.sync_copy(gate_hbm.at[pl.ds(t0, SC_TB)], gv)
            start(0, 0)

            def pair(p, c2):
                q = 2 * p
                start(q + 1, 1)
                compute(q, 0)

                @pl.when(q + 2 < n_chunk)
                def _():
                    start(q + 2, 0)

                compute(q + 1, 1)
                return c2

            lax.fori_loop(0, n_chunk // 2, pair, 0)
            pltpu.sync_copy(outv, out_hbm.at[pl.ds(t0, SC_TB)])
            return carry

        lax.fori_loop(0, n_blk, block, 0)

    return run(hn, idx.reshape(n, PEER_HEADS, SC_L), gate, u_tab, v_tab, x)


def _even_w_in(w):
    out = jnp.zeros((D_MODEL, EVEN_W), F32)
    out = out.at[:, :EVEN_KR_OFF].set(w[:, :D_CQ + D_C])
    out = out.at[:, EVEN_KR_OFF + DN_A:EVEN_KR_OFF + DN_A + DR_A].set(w[:, D_CQ + D_C:D_CQ + D_C + DR_A])
    out = out.at[:, EVEN_PR_OFF:].set(w[:, D_CQ + D_C + DR_A:])
    return out.astype(BF16)


def _even_prompt(x, b, t, pos, g_mix, P, W):
    proj = norm_matmul(x, g_mix, W['in'], after=W['after'])
    qc, kc, vc, ckv, kr = mla_prep(proj, pos, P)
    o_a = mla_prompt_attention(qc, kc, vc, b, t)
    o_b, sh, wkv = rwkv_mix(proj, jnp.zeros((b, RW_WIDTH), F32), jnp.zeros((b, H_B, HS_B, HS_B), F32),
                            b, t, P, nb=min(b, 4), tc=64, tm=256)
    x = matmul_res([o_a, o_b], W['out'], x)
    return x, lambda: (ckv.reshape(b, t, D_C), kr[:, DN_A:DN_A + DR_A].reshape(b, t, DR_A), sh, wkv)


def _even_sample(x, b, t, pos, page_table, cache_ckv, cache_kr, rw_shift, rw_wkv, g_mix, P, W):
    proj = norm_matmul(x, g_mix, W['in'], after=W['after'])
    qc, _, _, ckv, kr = mla_prep(proj, pos, P)
    qc = qc.reshape(b, t, H_A, HG)
    ckv = ckv.reshape(b, t, D_C)
    kr = kr[:, DN_A:DN_A + DR_A].reshape(b, t, DR_A)
    o_a = mla_sample_attention(page_table, cache_ckv, cache_kr, ckv, kr, qc[..., :DN_A], qc[..., DN_A:DN_A + DR_A], P)
    o_b, sh, wkv = rwkv_mix(proj, rw_shift, rw_wkv, b, t, P, nb=8, tc=t, tm=256)
    x = matmul_res([o_a, o_b], W['out'], x)
    return x, lambda: (ckv, kr, sh, wkv)


def _odd_states(proj, b, t):
    p3 = proj.reshape(b, t, ODD_W)
    if t % LANES == 0:
        k_new, v_new = sb_kv_state(proj, b, t)
    else:
        k_new, v_new = (p3[:, :, c * HC:(c + 1) * HC].reshape(b, t, H_C, HD_C) for c in (1, 2))
    return k_new, v_new, p3[:, t - (CONV_W - 1):, ODD_XBC_OFF:ODD_XBC_OFF + CONV_DIM]


def _odd_prompt(x, b, t, g_mix, P, W):
    proj = norm_matmul(x, g_mix, W['in'], after=W['after'])
    o_c = sb_prompt_attention(proj, b, t)
    y, ssm = ssd_mix(proj, jnp.zeros((b, CONV_W - 1, CONV_DIM), F32), jnp.zeros((b, H_D, P_D, N_D), F32), b, t, P)
    x = matmul_res([o_c, y], W['out'], x)
    return x, lambda: _odd_states(proj, b, t) + (ssm,)


def _odd_sample(x, b, t, page_table, cache_k, cache_v, conv_state, ssm_state, g_mix, P, W):
    proj = norm_matmul(x, g_mix, W['in'], after=W['after'])
    o_c = sb_sample_attention(page_table, cache_k, cache_v, proj, b, t)
    y, ssm = ssd_mix(proj, conv_state, ssm_state, b, t, P)
    x = matmul_res([o_c, y], W['out'], x)
    return x, lambda: _odd_states(proj, b, t) + (ssm,)


PROMPT_SPLITS = 2


def kernel(x_prompt, x_sample, mem_prompt, page_table, cache_mla_ckv, cache_mla_krope, state_rwkv_shift, state_rwkv_wkv, cache_sb_k, cache_sb_v, state_ssm_conv, state_ssm, cache_mem_k, cache_mem_v, norm_mix, norm_mem, norm_ffn, w_in_even, w_out_even, mla_g_cq, mla_w_uq, mla_g_ckv, mla_w_uk, mla_w_uv, mla_g_qn, mla_g_kn, mla_g_qr, mla_g_kr, rw_mu, rw_w0, rw_w_up, rw_a0, rw_a_up, rw_g_up, rw_k_k, rw_k_a, rw_r_k, rw_ln_g, rw_ln_b, w_in_odd, w_out_odd, ssm_conv_w, ssm_conv_b, ssm_dt_bias, ssm_a_log, ssm_d, ssm_g_norm, mem_g_src, mem_w_q, mem_w_kv, mem_g_q, mem_g_k, mem_w_o, peer_w_q, peer_sub_keys, peer_u, peer_v):
    bp, tp = x_prompt.shape[:2]
    bs, ts = x_sample.shape[:2]
    depth = norm_mix.shape[0]
    nsp = PROMPT_SPLITS if bp % PROMPT_SPLITS == 0 else 1
    bh = bp // nsp
    part = lambda a, k: a[k * bh:(k + 1) * bh]
    xps = [part(x_prompt, k).reshape(bh * tp, D_MODEL) for k in range(nsp)]
    xs = x_sample.reshape(bs * ts, D_MODEL)
    pos_p = jnp.tile(jnp.arange(tp, dtype=I32), bh)
    pos_s = jnp.tile(PAST_LEN + jnp.arange(ts, dtype=I32), bs)
    even_p, even_s, odd_p, odd_s, mem_k, mem_v = [], [], [], [], [], []
    tc_done = None
    for layer in range(depth):
        i = layer // 2
        even = layer % 2 == 0
        g_mix = norm_mix[layer]
        if even:
            P = dict(w_in=w_in_even[i], w_out=w_out_even[i], g_cq=mla_g_cq[i], w_uq=mla_w_uq[i],
                     g_ckv=mla_g_ckv[i], w_uk=mla_w_uk[i], w_uv=mla_w_uv[i], g_qn=mla_g_qn[i],
                     g_kn=mla_g_kn[i], g_qr=mla_g_qr[i], g_kr=mla_g_kr[i],
                     mu=rw_mu[i], w0=rw_w0[i], w_up=rw_w_up[i], a0=rw_a0[i], a_up=rw_a_up[i],
                     g_up=rw_g_up[i], k_k=rw_k_k[i], k_a=rw_k_a[i], r_k=rw_r_k[i],
                     ln_g=rw_ln_g[i], ln_b=rw_ln_b[i])
            w_out = P['w_out'].astype(BF16)
            W = {'in': _even_w_in(P['w_in']), 'out': [w_out[:H_A * DV_A], w_out[H_A * DV_A:]]}
        else:
            P = dict(w_in=w_in_odd[i], w_out=w_out_odd[i], conv_w=ssm_conv_w[i], conv_b=ssm_conv_b[i],
                     dt_bias=ssm_dt_bias[i], a_log=ssm_a_log[i], d=ssm_d[i], g_norm=ssm_g_norm[i])
            w_out = P['w_out'].astype(BF16)
            W = {'in': jnp.pad(P['w_in'], ((0, 0), (0, ODD_W - P['w_in'].shape[1]))).astype(BF16),
                 'out': [w_out[:HC], w_out[HC:]]}
        mem = lambda x, t, k, v, tq: memory_attend(x, t, norm_mem[layer], mem_w_q[layer], mem_g_q[layer], k, v,
                                                   mem_w_o[layer], tq=tq)

        def peer(x):
            hn, idx, gate = peer_route(x, norm_ffn[layer], peer_w_q[layer], peer_sub_keys[layer])
            return peer_experts(hn, idx, gate, peer_u[layer], peer_v[layer], x), idx

        mk, mv = memory_kv(mem_prompt.reshape(bp * N_MEM, D_MODEL), mem_g_src[layer], mem_w_kv[layer], mem_g_k[layer])
        mem_k.append(mk.reshape(bp, N_MEM, MEM_HEADS, MEM_HD))
        mem_v.append(mv.reshape(bp, N_MEM, MEM_HEADS, MEM_HD))
        mk, mv = mk.reshape(bp, N_MEM, MEM_W), mv.reshape(bp, N_MEM, MEM_W)
        parts = []
        for k in range(nsp):
            W['after'] = tc_done
            if even:
                x, sp = _even_prompt(xps[k], bh, tp, pos_p, g_mix, P, W)
            else:
                x, sp = _odd_prompt(xps[k], bh, tp, g_mix, P, W)
            parts.append(sp)
            xps[k], tc_done = peer(mem(x, tp, part(mk, k), part(mv, k), 256))
        (even_p if even else odd_p).append(parts)
        W['after'] = tc_done
        if even:
            xs, ss = _even_sample(xs, bs, ts, pos_s, page_table, cache_mla_ckv[i], cache_mla_krope[i],
                                  state_rwkv_shift[i], state_rwkv_wkv[i], g_mix, P, W)
            even_s.append(ss)
        else:
            xs, ss = _odd_sample(xs, bs, ts, page_table, cache_sb_k[i], cache_sb_v[i], state_ssm_conv[i],
                                 state_ssm[i], g_mix, P, W)
            odd_s.append(ss)
        xs, tc_done = peer(mem(xs, ts, cache_mem_k[layer].reshape(bs, N_MEM, MEM_W),
                               cache_mem_v[layer].reshape(bs, N_MEM, MEM_W), ts))
    def join(parts):
        parts = [p() for p in parts]
        return tuple(jnp.concatenate([p[j] for p in parts], axis=0) for j in range(len(parts[0])))

    even_p, odd_p = [join(p) for p in even_p], [join(p) for p in odd_p]
    even_s, odd_s = [s() for s in even_s], [s() for s in odd_s]
    stack = lambda groups, k: jnp.stack([g[k] for g in groups])
    xp = jnp.concatenate(xps, axis=0)
    return (xp.reshape(bp, tp, D_MODEL), xs.reshape(bs, ts, D_MODEL),
            stack(even_p, 0), stack(even_p, 1), stack(even_p, 2), stack(even_p, 3),
            stack(odd_p, 0), stack(odd_p, 1), stack(odd_p, 2), stack(odd_p, 3),
            jnp.stack(mem_k), jnp.stack(mem_v),
            stack(even_s, 0), stack(even_s, 1), stack(even_s, 2), stack(even_s, 3),
            stack(odd_s, 0), stack(odd_s, 1), stack(odd_s, 2), stack(odd_s, 3))
```

```python
import functools
import math

import numpy as np
import jax
import jax.numpy as jnp
from jax import lax
from jax.experimental import pallas as pl
from jax.experimental.pallas import tpu as pltpu
from jax.experimental.pallas import tpu_sc as plsc

F32 = jnp.float32
BF16 = jnp.bfloat16
I32 = jnp.int32

D_MODEL = 1024
EPS = 1e-6
PAST_LEN = 8192
PAGE_SIZE = 128
N_PAGES = PAST_LEN // PAGE_SIZE

H_A, DN_A, DR_A, DV_A, D_CQ, D_C = 8, 64, 32, 64, 384, 256
ROPE_BASE = 10000.0
MLA_SCALE = (DN_A + DR_A) ** -0.5
H_B, HS_B, D_WL, D_AL, D_GL = 8, 64, 64, 64, 128
HB = H_B * HS_B
RW_WIDTH = 3 * HB + D_WL + D_AL + D_GL
RWKV_DECAY_SCALE = 0.606531
RWKV_LN_EPS = 64e-5
H_C, HD_C = 8, 64
HC = H_C * HD_C
H_D, P_D, N_D, G_D, CONV_W = 8, 64, 128, 2, 4
D_INNER = H_D * P_D
CONV_DIM = D_INNER + 2 * G_D * N_D
SSD_CHUNK = 128
N_MEM, MEM_HEADS, MEM_HD = 256, 4, 64
MEM_W = MEM_HEADS * MEM_HD
PEER_HEADS, N_KEYS, PEER_DQ, PEER_TOPK = 8, 128, 256, 16
LANES = 128
VMEM_LIMIT = 56 * 1024 * 1024


def _cp(*sem):
    return pltpu.CompilerParams(dimension_semantics=sem, vmem_limit_bytes=VMEM_LIMIT)


def _dot(a, b):
    return jnp.dot(a, b, preferred_element_type=F32)


def _dot_nt(a, b):
    return lax.dot_general(a, b, (((1,), (1,)), ((), ())), preferred_element_type=F32)


def _dot_tn(a, b):
    return lax.dot_general(a, b, (((0,), (0,)), ((), ())), preferred_element_type=F32)


def _split2(x):
    hi = x.astype(BF16)
    lo = (x - hi.astype(F32)).astype(BF16)
    return hi, lo


def _dot_f32(x, m):
    hi, lo = _split2(x)
    return _dot(hi, m) + _dot(lo, m)


def _dot_sel(x, m):
    hi = x.astype(BF16)
    r1 = x - hi.astype(F32)
    mid = r1.astype(BF16)
    lo = (r1 - mid.astype(F32)).astype(BF16)
    return (_dot(hi, m) + _dot(mid, m)) + _dot(lo, m)


def _rms(x, g):
    return x * lax.rsqrt(jnp.mean(x * x, axis=-1, keepdims=True) + EPS) * g


def _sigmoid(x):
    return 1.0 / (1.0 + jnp.exp(-x))


def _softplus(x):
    return jnp.maximum(x, 0.0) + jnp.log(1.0 + jnp.exp(-jnp.abs(x)))


def _blockdiag_ones(n, seg):
    i = np.arange(n)
    return (i[:, None] // seg == i[None, :] // seg).astype(np.float32)


def _nm_body(x_ref, g_ref, w_ref, *rest):
    o_ref = rest[-1]
    h = _rms(x_ref[...], g_ref[...])
    o_ref[...] = _dot(h.astype(BF16), w_ref[...])


def norm_matmul(x, g, w, tm=256, after=None):
    n, k = x.shape
    m = w.shape[1]
    extra = [] if after is None else [after]
    return pl.pallas_call(
        _nm_body,
        grid=(n // tm,),
        in_specs=[pl.BlockSpec((tm, k), lambda i: (i, 0)),
                  pl.BlockSpec((1, k), lambda i: (0, 0)),
                  pl.BlockSpec((k, m), lambda i: (0, 0))] + [pl.BlockSpec(memory_space=pl.ANY) for _ in extra],
        out_specs=pl.BlockSpec((tm, m), lambda i: (i, 0)),
        out_shape=jax.ShapeDtypeStruct((n, m), F32),
        compiler_params=_cp("parallel"),
        name="norm_matmul",
    )(x, g.reshape(1, k), w, *extra)


def _mr_body(*refs, n_in):
    a_refs, w_refs, r_ref, o_ref = refs[:n_in], refs[n_in:2 * n_in], refs[2 * n_in], refs[2 * n_in + 1]
    acc = r_ref[...]
    for a_ref, w_ref in zip(a_refs, w_refs):
        acc = acc + _dot(a_ref[...].astype(BF16), w_ref[...])
    o_ref[...] = acc


def matmul_res(a_list, w_list, res, tm=256):
    n, m = res.shape
    n_in = len(a_list)
    in_specs = ([pl.BlockSpec((tm, a.shape[1]), lambda i: (i, 0)) for a in a_list]
                + [pl.BlockSpec(w.shape, lambda i: (0, 0)) for w in w_list]
                + [pl.BlockSpec((tm, m), lambda i: (i, 0))])
    return pl.pallas_call(
        functools.partial(_mr_body, n_in=n_in),
        grid=(n // tm,),
        in_specs=in_specs,
        out_specs=pl.BlockSpec((tm, m), lambda i: (i, 0)),
        out_shape=jax.ShapeDtypeStruct((n, m), F32),
        compiler_params=_cp("parallel"),
        name="matmul_res",
    )(*a_list, *w_list, res)


def _memkv_body(m_ref, g_ref, w_ref, gk_ref, e_ref, k_ref, v_ref):
    h = _rms(m_ref[...], g_ref[...])
    kv = _dot(h.astype(BF16), w_ref[...])
    k = kv[:, :MEM_W]
    ms = _dot_f32(k * k, e_ref[...]) * (1.0 / MEM_HD)
    k_ref[...] = k * lax.rsqrt(ms + EPS) * gk_ref[...]
    v_ref[...] = kv[:, MEM_W:]


def memory_kv(mem2d, g_src, w_kv, g_k, tm=256):
    n = mem2d.shape[0]
    e = jnp.asarray(_blockdiag_ones(MEM_W, MEM_HD), BF16)
    gk = jnp.tile(g_k, MEM_HEADS).reshape(1, MEM_W)
    return pl.pallas_call(
        _memkv_body,
        grid=(n // tm,),
        in_specs=[pl.BlockSpec((tm, D_MODEL), lambda i: (i, 0)),
                  pl.BlockSpec((1, D_MODEL), lambda i: (0, 0)),
                  pl.BlockSpec((D_MODEL, 2 * MEM_W), lambda i: (0, 0)),
                  pl.BlockSpec((1, MEM_W), lambda i: (0, 0)),
                  pl.BlockSpec((MEM_W, MEM_W), lambda i: (0, 0))],
        out_specs=[pl.BlockSpec((tm, MEM_W), lambda i: (i, 0))] * 2,
        out_shape=[jax.ShapeDtypeStruct((n, MEM_W), F32)] * 2,
        compiler_params=_cp("parallel"),
        name="memory_kv",
    )(mem2d, g_src.reshape(1, -1), w_kv.astype(BF16), gk, e)


def _memattn_body(x_ref, g_ref, wq_ref, gq_ref, e_ref, k_ref, v_ref, wo_ref, o_ref, att_ref):
    x = x_ref[...]
    h = _rms(x, g_ref[...])
    q = _dot(h.astype(BF16), wq_ref[...])
    ms = _dot_f32(q * q, e_ref[...]) * (1.0 / MEM_HD)
    q = (q * lax.rsqrt(ms + EPS) * gq_ref[...]).astype(BF16)
    k = k_ref[0].astype(BF16)
    v = v_ref[0].astype(BF16)
    for hd in range(MEM_HEADS):
        sl = slice(hd * MEM_HD, (hd + 1) * MEM_HD)
        s = _dot_nt(q[:, sl], k[:, sl]) * (MEM_HD ** -0.5)
        p = jnp.exp(s - jnp.max(s, axis=-1, keepdims=True))
        p = p / jnp.sum(p, axis=-1, keepdims=True)
        att_ref[:, sl] = _dot(p.astype(BF16), v[:, sl])
    o_ref[...] = x + _dot(att_ref[...].astype(BF16), wo_ref[...])


def memory_attend(x, seq_t, g_norm, w_q, g_q, k, v, w_o, tq):
    n = x.shape[0]
    per_b = seq_t // tq
    e = jnp.asarray(_blockdiag_ones(MEM_W, MEM_HD), BF16)
    gq = jnp.tile(g_q, MEM_HEADS).reshape(1, MEM_W)
    return pl.pallas_call(
        _memattn_body,
        grid=(n // tq,),
        in_specs=[pl.BlockSpec((tq, D_MODEL), lambda i: (i, 0)),
                  pl.BlockSpec((1, D_MODEL), lambda i: (0, 0)),
                  pl.BlockSpec((D_MODEL, MEM_W), lambda i: (0, 0)),
                  pl.BlockSpec((1, MEM_W), lambda i: (0, 0)),
                  pl.BlockSpec((MEM_W, MEM_W), lambda i: (0, 0)),
                  pl.BlockSpec((1, N_MEM, MEM_W), lambda i: (i // per_b, 0, 0)),
                  pl.BlockSpec((1, N_MEM, MEM_W), lambda i: (i // per_b, 0, 0)),
                  pl.BlockSpec((MEM_W, D_MODEL), lambda i: (0, 0))],
        out_specs=pl.BlockSpec((tq, D_MODEL), lambda i: (i, 0)),
        out_shape=jax.ShapeDtypeStruct((n, D_MODEL), F32),
        scratch_shapes=[pltpu.VMEM((tq, MEM_W), F32)],
        compiler_params=_cp("parallel"),
        name="memory_attend",
    )(x, g_norm.reshape(1, -1), w_q.astype(BF16), gq, e, k, v, w_o.astype(BF16))


def _topk_rows(s, k, payload=None, order=None):
    rows, cols = s.shape
    ridx = lax.broadcasted_iota(I32, (rows, cols), 0) if order is None else order
    orow = lax.broadcasted_iota(I32, (k, cols), 0)

    def body(r, carry):
        s, acc_v, acc_i = carry
        m = jnp.max(s, axis=0, keepdims=True)
        first = jnp.min(jnp.where(s == m, ridx, jnp.iinfo(jnp.int32).max), axis=0, keepdims=True)
        hit = ridx == first
        if payload is None:
            pick = first.astype(F32)
        else:
            pick = jnp.max(jnp.where(hit, payload, -1.0), axis=0, keepdims=True)
        here = orow == r
        return jnp.where(hit, -jnp.inf, s), jnp.where(here, m, acc_v), jnp.where(here, pick, acc_i)

    z = jnp.zeros((k, cols), F32)
    _, acc_v, acc_i = lax.fori_loop(0, k, body, (s, z, z))
    return acc_v, acc_i


def _route_body(x_ref, g_ref, wq_ref, keys_ref, hn_ref, idx_ref, gate_ref):
    h = _rms(x_ref[...], g_ref[...])
    hn_ref[...] = h
    q = _dot(h.astype(BF16), wq_ref[...])
    tops = []
    for c in range(2):
        s = _dot_nt(keys_ref[0, c], q[:, c * LANES:(c + 1) * LANES].astype(BF16))
        tops.append(_topk_rows(s, PEER_TOPK))
    (s0, i0), (s1, i1) = tops
    tm = s0.shape[1]
    ss, ii, ff = [], [], []

    def rows_a(a, nb):
        ss.append(s0[a:a + 1, :] + s1[:nb])
        ii.append(i0[a:a + 1, :] * float(N_KEYS) + i1[:nb])
        ff.append(a * PEER_TOPK + lax.broadcasted_iota(I32, (nb, tm), 0))

    def rows_b(b, na, a_min):
        a_idx = lax.broadcasted_iota(I32, (na, tm), 0)
        ss.append(jnp.where(a_idx >= a_min, s0[:na] + s1[b:b + 1, :], -jnp.inf))
        ii.append(i0[:na] * float(N_KEYS) + i1[b:b + 1, :])
        ff.append(a_idx * PEER_TOPK + b)

    half = PEER_TOPK // 2
    plan_a = [(0, PEER_TOPK)] + [(a, half) for a in range(1, 4)]
    plan_b = [(0, PEER_TOPK, 4)] + [(b, half, 4) for b in range(1, 3)]
    covered = [(a, b) for a, nb in plan_a for b in range(nb)] + [(a, b) for b, na, lo in plan_b for a in range(lo, na)]
    needed = {(a, b) for a in range(PEER_TOPK) for b in range(PEER_TOPK) if (a + 1) * (b + 1) <= PEER_TOPK}
    assert len(set(covered)) == len(covered) and needed <= set(covered)
    for a, nb in plan_a:
        rows_a(a, nb)
    for b, na, lo in plan_b:
        rows_b(b, na, lo)
    cand_s, cand_i, cand_f = (jnp.concatenate(p, axis=0) for p in (ss, ii, ff))
    best_s, best_i = _topk_rows(cand_s, PEER_TOPK, payload=cand_i, order=cand_f)
    e = jnp.exp(best_s - best_s[0:1, :])
    idx_ref[0] = best_i.astype(I32)
    gate_ref[0] = e / jnp.sum(e, axis=0, keepdims=True)


def peer_route(x, g, w_q, sub_keys, tm=128):
    n = x.shape[0]
    keys = sub_keys.astype(BF16)
    hn, idx, gate = pl.pallas_call(
        _route_body,
        grid=(n // tm, PEER_HEADS),
        in_specs=[pl.BlockSpec((tm, D_MODEL), lambda i, hd: (i, 0)),
                  pl.BlockSpec((1, D_MODEL), lambda i, hd: (0, 0)),
                  pl.BlockSpec((D_MODEL, PEER_DQ), lambda i, hd: (0, hd)),
                  pl.BlockSpec((1, 2, N_KEYS, PEER_DQ // 2), lambda i, hd: (hd, 0, 0, 0))],
        out_specs=[pl.BlockSpec((tm, D_MODEL), lambda i, hd: (i, 0)),
                   pl.BlockSpec((1, PEER_TOPK, tm), lambda i, hd: (hd, 0, i)),
                   pl.BlockSpec((1, PEER_TOPK, tm), lambda i, hd: (hd, 0, i))],
        out_shape=[jax.ShapeDtypeStruct((n, D_MODEL), F32),
                   jax.ShapeDtypeStruct((PEER_HEADS, PEER_TOPK, n), I32),
                   jax.ShapeDtypeStruct((PEER_HEADS, PEER_TOPK, n), F32)],
        compiler_params=_cp("parallel", "arbitrary"),
        name="peer_route",
    )(x, g.reshape(1, -1), w_q.astype(BF16), keys)
    idx = jnp.transpose(idx, (2, 0, 1)).reshape(n, PEER_HEADS * PEER_TOPK)
    gate = jnp.transpose(gate, (2, 0, 1)).reshape(n, PEER_HEADS * PEER_TOPK)
    return hn, idx, gate


HG = LANES
QW = H_A * HG
EVEN_CKV_OFF = D_CQ
EVEN_KR_OFF = D_CQ + D_C
EVEN_PR_OFF = EVEN_KR_OFF + HG
EVEN_W = EVEN_PR_OFF + RW_WIDTH


def _mla_prep_body(p_ref, cs_ref, sn_ref, gcq_ref, wuq_ref, gq_ref, gckv_ref, gkr_ref, wuk_ref, gk_ref, wuv_ref,
                   eq_ref, ek_ref, perm_ref, q_out, k_out, v_out, ckv_out, kr_out):
    cs, sn = cs_ref[...], sn_ref[...]

    def rope(x):
        return x * cs + _dot_sel(x, perm_ref[...]) * sn

    q = _dot(_rms(p_ref[:, :D_CQ], gcq_ref[...]).astype(BF16), wuq_ref[...])
    ckv = _rms(p_ref[:, EVEN_CKV_OFF:EVEN_CKV_OFF + D_C], gckv_ref[...])
    ckv_out[...] = ckv
    krr = p_ref[:, EVEN_KR_OFF:EVEN_KR_OFF + HG]
    ms = jnp.sum(krr * krr, axis=-1, keepdims=True) * (1.0 / DR_A)
    kr = rope(krr * lax.rsqrt(ms + EPS) * gkr_ref[...])
    kr_out[...] = kr
    ckv_b = ckv.astype(BF16)
    kraw = _dot(ckv_b, wuk_ref[...])
    v_out[...] = _dot(ckv_b, wuv_ref[...]).astype(BF16)
    for h in range(H_A):
        sl = slice(h * HG, (h + 1) * HG)
        qh = q[:, sl]
        qh = qh * lax.rsqrt(_dot_f32(qh * qh, eq_ref[...]) + EPS) * gq_ref[:, sl]
        q_out[:, sl] = rope(qh).astype(BF16)
        kh = kraw[:, sl]
        kh = kh * lax.rsqrt(_dot_f32(kh * kh, ek_ref[...]) + EPS) * gk_ref[:, sl]
        k_out[:, sl] = (kh + kr).astype(BF16)


def _head_groups(w, width, off=0):
    k = w.shape[0]
    out = jnp.zeros((k, H_A, HG), w.dtype)
    out = out.at[:, :, off:off + width].set(w.reshape(k, H_A, width))
    return out.reshape(k, H_A * HG)


def mla_prep(proj, pos, P, tm=256):
    n = proj.shape[0]
    half = DR_A // 2
    freqs = ROPE_BASE ** (-jnp.arange(half, dtype=F32) / half)
    ang = pos.astype(F32)[:, None] * freqs[None, :]
    cos, sin = jnp.cos(ang), jnp.sin(ang)
    one, zero = jnp.ones((n, DN_A), F32), jnp.zeros((n, DN_A), F32)
    pad = jnp.zeros((n, HG - DN_A - DR_A), F32)
    cs = jnp.concatenate([one, cos, cos, pad], axis=1)
    sn = jnp.concatenate([zero, -sin, sin, pad], axis=1)
    wuq = P['w_uq'].reshape(D_CQ, H_A, DN_A + DR_A)
    wuq = jnp.pad(wuq, ((0, 0), (0, 0), (0, HG - DN_A - DR_A))).reshape(D_CQ, QW).astype(BF16)
    gq = jnp.tile(jnp.concatenate([P['g_qn'], P['g_qr'], jnp.zeros((HG - DN_A - DR_A,), F32)]), H_A).reshape(1, QW)
    gkr = jnp.zeros((1, HG), F32).at[0, DN_A:DN_A + DR_A].set(P['g_kr'])
    wuk = _head_groups(P['w_uk'], DN_A).astype(BF16)
    wuv = _head_groups(P['w_uv'], DV_A).astype(BF16)
    gk = _head_groups(jnp.tile(P['g_kn'], H_A).reshape(1, -1), DN_A)
    eq = np.zeros((HG, HG), np.float32)
    eq[:DN_A, :DN_A] = 1.0 / DN_A
    eq[DN_A:DN_A + DR_A, DN_A:DN_A + DR_A] = 1.0 / DR_A
    ek = np.zeros((HG, HG), np.float32)
    ek[:DN_A, :DN_A] = 1.0 / DN_A
    perm = np.zeros((HG, HG), np.float32)
    j = np.arange(half)
    perm[DN_A + half + j, DN_A + j] = 1.0
    perm[DN_A + j, DN_A + half + j] = 1.0
    row = lambda a: pl.BlockSpec(a.shape, lambda i: (0, 0))
    consts = [P['g_cq'].reshape(1, -1), wuq, gq, P['g_ckv'].reshape(1, -1), gkr, wuk, gk, wuv,
              jnp.asarray(eq, BF16), jnp.asarray(ek, BF16), jnp.asarray(perm, BF16)]
    return pl.pallas_call(
        _mla_prep_body,
        grid=(n // tm,),
        in_specs=[pl.BlockSpec((tm, EVEN_PR_OFF), lambda i: (i, 0)),
                  pl.BlockSpec((tm, HG), lambda i: (i, 0)),
                  pl.BlockSpec((tm, HG), lambda i: (i, 0))] + [row(c) for c in consts],
        out_specs=[pl.BlockSpec((tm, QW), lambda i: (i, 0))] * 3
        + [pl.BlockSpec((tm, D_C), lambda i: (i, 0)), pl.BlockSpec((tm, HG), lambda i: (i, 0))],
        out_shape=[jax.ShapeDtypeStruct((n, QW), BF16)] * 3
        + [jax.ShapeDtypeStruct((n, D_C), F32), jax.ShapeDtypeStruct((n, HG), F32)],
        compiler_params=_cp("parallel"),
        name="mla_prep",
    )(proj, cs, sn, *consts)


def _mla_prompt_body(q_ref, k_ref, v_ref, o_ref, *, tq):
    qi = pl.program_id(1)
    row = lax.broadcasted_iota(I32, (tq, tq), 0)
    col = lax.broadcasted_iota(I32, (tq, tq), 1)
    for h in range(H_A):
        sl = slice(h * HG, (h + 1) * HG)
        q = q_ref[:, sl]

        def body(kb, carry):
            m, l, acc = carry
            off = pl.multiple_of(kb * tq, tq)
            s = _dot_nt(q, k_ref[pl.ds(off, tq), sl]) * MLA_SCALE
            s = jnp.where(col + kb * tq <= row + qi * tq, s, -jnp.inf)
            m_new = jnp.maximum(m, jnp.max(s, axis=-1, keepdims=True))
            alpha = jnp.exp(m - m_new)
            p = jnp.exp(s - m_new)
            l = l * alpha + jnp.sum(p, axis=-1, keepdims=True)
            acc = acc * alpha + _dot(p.astype(BF16), v_ref[pl.ds(off, tq), sl])
            return m_new, l, acc

        init = (jnp.full((tq, 1), -jnp.inf, F32), jnp.zeros((tq, 1), F32), jnp.zeros((tq, HG), F32))
        _, l, acc = lax.fori_loop(0, qi + 1, body, init)
        o_ref[:, h * DV_A:(h + 1) * DV_A] = (acc / l)[:, :DV_A].astype(BF16)


def mla_prompt_attention(qc, kc, vc, b, t, tq=256):
    nq = t // tq
    return pl.pallas_call(
        functools.partial(_mla_prompt_body, tq=tq),
        grid=(b, nq),
        in_specs=[pl.BlockSpec((tq, QW), lambda bi, qi: (bi * nq + qi, 0)),
                  pl.BlockSpec((t, QW), lambda bi, qi: (bi, 0)),
                  pl.BlockSpec((t, QW), lambda bi, qi: (bi, 0))],
        out_specs=pl.BlockSpec((tq, H_A * DV_A), lambda bi, qi: (bi * nq + qi, 0)),
        out_shape=jax.ShapeDtypeStruct((b * t, H_A * DV_A), BF16),
        compiler_params=_cp("parallel", "arbitrary"),
        name="mla_prompt_attention",
    )(qc, kc, vc)


PAGES_PER_STEP = 8
KV_STEPS = N_PAGES // PAGES_PER_STEP


def _mla_sample_body(pt_ref, *refs, t_new):
    npg = PAGES_PER_STEP
    ckv_pages, kr_pages = refs[:npg], refs[npg:2 * npg]
    (ckv_new, kr_new, qn_ref, qr_ref, wuk_ref, e_ref, wuv_ref, hm_ref, o_ref,
     m_scr, l_scr, acc_scr) = refs[2 * npg:]
    j = pl.program_id(1)

    def process(ckv, kr, mask):
        ckv_b = ckv.astype(BF16)
        kraw = _dot(ckv_b, wuk_ref[...])
        ms = _dot_f32(kraw * kraw, e_ref[...]) * (1.0 / DN_A)
        s = (_dot(kraw.astype(BF16), qn_ref[0]) * lax.rsqrt(ms + EPS)
             + _dot_tn(kr.astype(BF16), qr_ref[0])) * MLA_SCALE
        if mask is not None:
            s = jnp.where(mask, s, -jnp.inf)
        m = m_scr[...]
        m_new = jnp.maximum(m, jnp.max(s, axis=0, keepdims=True))
        alpha = jnp.exp(m - m_new)
        p = jnp.exp(s - m_new)
        l_scr[...] = l_scr[...] * alpha + jnp.sum(p, axis=0, keepdims=True)
        acc_scr[...] = acc_scr[...] * alpha + _dot_tn(ckv_b, p.astype(BF16))
        m_scr[...] = m_new

    @pl.when(j == 0)
    def _():
        m_scr[...] = jnp.full(m_scr.shape, -jnp.inf, F32)
        l_scr[...] = jnp.zeros(l_scr.shape, F32)
        acc_scr[...] = jnp.zeros(acc_scr.shape, F32)
        nk, nq = PAGE_SIZE, H_A * t_new
        key = lax.broadcasted_iota(I32, (nk, nq), 0)
        qt = lax.broadcasted_iota(I32, (nk, nq), 1) % t_new
        process(ckv_new[0], kr_new[0], key <= qt)

    process(jnp.concatenate([r[0] for r in ckv_pages], axis=0), jnp.concatenate([r[0] for r in kr_pages], axis=1), None)

    @pl.when(j == KV_STEPS - 1)
    def _():
        o_lat =(acc_scr[...] / l_scr[...]).astype(BF16)
        full = _dot_tn(o_lat, wuv_ref[...]) * hm_ref[...]
        out = full[0:t_new]
        for h in range(1, H_A):
            out = out + full[h * t_new:(h + 1) * t_new]
        o_ref[0] = out.astype(BF16)


def mla_sample_attention(page_table, cache_ckv, cache_kr, ckv_new, kr_new, qn, qr, P):
    b, t_new = qn.shape[:2]
    nq = H_A * t_new
    eye = jnp.eye(H_A, dtype=BF16)
    qn_g = (qn.astype(F32) * P['g_kn']).astype(BF16)
    qn_bd = jnp.einsum('bthd,hg->bhdgt', qn_g, eye).reshape(b, H_A * DN_A, nq)
    qr_m = jnp.transpose(qr, (0, 3, 2, 1)).reshape(b, DR_A, nq)
    pad = lambda a: jnp.pad(a, ((0, 0), (0, PAGE_SIZE - t_new), (0, 0)))
    keys_last = lambda a: jnp.swapaxes(a, 1, 2)
    cache_kr = keys_last(cache_kr)
    hm = (np.arange(nq)[:, None] // t_new == np.arange(H_A * DV_A)[None, :] // DV_A).astype(np.float32)
    e_head = (np.arange(H_A * DN_A)[:, None] // DN_A == np.arange(nq)[None, :] // t_new).astype(np.float32)
    npg = PAGES_PER_STEP
    page = lambda p, s: pl.BlockSpec((1,) + s, lambda bi, j, pt: (pt[bi * N_PAGES + j * npg + p], 0, 0))
    per_b = lambda s: pl.BlockSpec((1,) + s, lambda bi, j, pt: (bi, 0, 0))
    const = lambda a: pl.BlockSpec(a.shape, lambda bi, j, pt: (0, 0))
    consts = [P['w_uk'].astype(BF16), jnp.asarray(e_head, BF16), P['w_uv'].astype(BF16), jnp.asarray(hm)]
    grid_spec = pltpu.PrefetchScalarGridSpec(
        num_scalar_prefetch=1,
        grid=(b, KV_STEPS),
        in_specs=[page(p, (PAGE_SIZE, D_C)) for p in range(npg)] + [page(p, (DR_A, PAGE_SIZE)) for p in range(npg)]
        + [per_b((PAGE_SIZE, D_C)), per_b((DR_A, PAGE_SIZE)), per_b((H_A * DN_A, nq)), per_b((DR_A, nq))]
        + [const(c) for c in consts],
        out_specs=pl.BlockSpec((1, t_new, H_A * DV_A), lambda bi, j, pt: (bi, 0, 0)),
        scratch_shapes=[pltpu.VMEM((1, nq), F32), pltpu.VMEM((1, nq), F32), pltpu.VMEM((D_C, nq), F32)],
    )
    out = pl.pallas_call(
        functools.partial(_mla_sample_body, t_new=t_new),
        grid_spec=grid_spec,
        out_shape=jax.ShapeDtypeStruct((b, t_new, H_A * DV_A), BF16),
        compiler_params=_cp("parallel", "arbitrary"),
        name="mla_sample_attention",
    )(page_table.reshape(-1), *([cache_ckv] * npg), *([cache_kr] * npg), pad(ckv_new), keys_last(pad(kr_new)),
      qn_bd, qr_m, *consts)
    return out.reshape(b * t_new, H_A * DV_A)


RW_LORA_OFF = 3 * HB


def _rwkv_prep_body(pr_ref, sh_ref, mu_ref, w0_ref, wup_ref, a0_ref, aup_ref, gup_ref, kk_ref, ka_ref, rk_ref, e_ref,
                    nkk_o, wr_o, w_o, kka_o, k2_o, v_o, c1_o, c2_o, g_o, bonus_o):
    pr = pr_ref[...]
    xs = pr + mu_ref[...] * (sh_ref[...] - pr)
    r, k, v = xs[:, :HB], xs[:, HB:2 * HB], xs[:, 2 * HB:3 * HB]
    xwa = xs[:, RW_LORA_OFF:RW_LORA_OFF + D_WL + D_AL]
    xg = xs[:, RW_LORA_OFF + D_WL + D_AL:]
    w = jnp.exp(-RWKV_DECAY_SCALE * _sigmoid(w0_ref[...] + _dot(jnp.tanh(xwa).astype(BF16), wup_ref[...])))
    a = _sigmoid(a0_ref[...] + _dot(xwa.astype(BF16), aup_ref[...]))
    g_o[...] = _dot(_sigmoid(xg).astype(BF16), gup_ref[...])
    kk = k * kk_ref[...]
    kk = kk * lax.rsqrt(jnp.maximum(_dot_f32(kk * kk, e_ref[...]), 1e-12))
    k2 = k * (1.0 + (a - 1.0) * ka_ref[...])
    kka = kk * a
    nkk_o[...] = -kk
    wr_o[...] = w * r
    w_o[...] = w
    kka_o[...] = kka
    k2_o[...] = k2
    v_o[...] = v
    c1_o[...] = _dot_f32(kka * r, e_ref[...])
    c2_o[...] = _dot_f32(k2 * r, e_ref[...])
    bonus_o[...] = _dot_f32(r * k2 * rk_ref[...], e_ref[...]) * v


def rwkv_prep(proj, shifted, P, tm=256):
    n = proj.shape[0]
    nblk = EVEN_PR_OFF // RW_WIDTH
    assert EVEN_PR_OFF % LANES == 0
    zw = jnp.zeros((D_WL, HB), F32)
    consts = [P['mu'].reshape(1, -1), P['w0'].reshape(1, -1),
              jnp.concatenate([P['w_up'], zw]).astype(BF16), P['a0'].reshape(1, -1),
              jnp.concatenate([zw, P['a_up']]).astype(BF16), P['g_up'].astype(BF16),
              P['k_k'].reshape(1, -1), P['k_a'].reshape(1, -1), P['r_k'].reshape(1, -1),
              jnp.asarray(_blockdiag_ones(HB, HS_B), BF16)]
    pr = lax.slice_in_dim(proj, EVEN_PR_OFF, EVEN_W, axis=1)
    return pl.pallas_call(
        _rwkv_prep_body,
        grid=(n // tm,),
        in_specs=[pl.BlockSpec((tm, RW_WIDTH), lambda i: (i, 0)), pl.BlockSpec((tm, RW_WIDTH), lambda i: (i, 0))]
        + [pl.BlockSpec(c.shape, lambda i: (0, 0)) for c in consts],
        out_specs=[pl.BlockSpec((tm, HB), lambda i: (i, 0))] * 10,
        out_shape=[jax.ShapeDtypeStruct((n, HB), F32)] * 10,
        compiler_params=_cp("parallel"),
        name="rwkv_prep",
    )(pr, shifted, *consts)


def _rwkv_scan_body(nkk_r, wr_r, w_r, kka_r, k2_r, v_r, c1_r, c2_r, s0_ref, e_ref, d_ref, o_ref, sf_ref, s_scr,
                    *, nb, tc):
    c = pl.program_id(1)

    @pl.when(c == 0)
    def _():
        s_scr[...] = s0_ref[...]

    dmask = d_ref[...]
    half = HB // 2

    def step(t, carry):
        for b in range(nb):
            row = lambda ref: ref[b, pl.ds(t, 1), :]
            s = s_scr[b]
            stacked = jnp.concatenate([s * row(nkk_r), s * row(wr_r), dmask * row(v_r)], axis=0)
            hi, lo = _split2(stacked)
            seg = jnp.concatenate(
                [_dot(hi[:, i * half:(i + 1) * half], e_ref[...]) + _dot(lo[:, i * half:(i + 1) * half], e_ref[...])
                 for i in range(2)], axis=1)
            sa, t2, vb = seg[:HS_B], seg[HS_B:2 * HS_B], seg[2 * HS_B:]
            s_scr[b] = s * row(w_r) + sa * row(kka_r) + vb * row(k2_r)
            ob = t2 + sa * row(c1_r) + vb * row(c2_r)
            o_ref[b, pl.ds(t, 1), :] = jnp.sum(ob * dmask, axis=0, keepdims=True)
        return carry

    lax.fori_loop(0, tc, step, 0)

    @pl.when(c == pl.num_programs(1) - 1)
    def _():
        sf_ref[...] = s_scr[...]


def rwkv_scan(seqs, state0, b, t, nb, tc):
    s0 = jnp.transpose(state0, (0, 2, 1, 3)).reshape(b, HS_B, HB)
    e = jnp.asarray(_blockdiag_ones(HB // 2, HS_B), BF16)
    dmask = jnp.asarray((np.arange(HS_B)[:, None] == (np.arange(HB)[None, :] % HS_B)).astype(np.float32))
    seq_spec = pl.BlockSpec((nb, tc, HB), lambda i, c: (i, c, 0))
    st_spec = pl.BlockSpec((nb, HS_B, HB), lambda i, c: (i, 0, 0))
    o, sf = pl.pallas_call(
        functools.partial(_rwkv_scan_body, nb=nb, tc=tc),
        grid=(b // nb, t // tc),
        in_specs=[seq_spec] * 8 + [st_spec, pl.BlockSpec(e.shape, lambda i, c: (0, 0)),
                                   pl.BlockSpec(dmask.shape, lambda i, c: (0, 0))],
        out_specs=[seq_spec, st_spec],
        out_shape=[jax.ShapeDtypeStruct((b, t, HB), F32), jax.ShapeDtypeStruct((b, HS_B, HB), F32)],
        scratch_shapes=[pltpu.VMEM((nb, HS_B, HB), F32)],
        compiler_params=_cp("parallel", "arbitrary"),
        name="rwkv_scan",
    )(*[a.reshape(b, t, HB) for a in seqs], s0, e, dmask)
    sf = jnp.transpose(sf.reshape(b, HS_B, H_B, HS_B), (0, 2, 1, 3))
    return o.reshape(b * t, HB), sf


def _rwkv_post_body(o_ref, g_ref, bonus_ref, lng_ref, lnb_ref, e_ref, out_ref):
    o = o_ref[...]
    mu = _dot_f32(o, e_ref[...]) * (1.0 / HS_B)
    d = o - mu
    var = _dot_f32(d * d, e_ref[...]) * (1.0 / HS_B)
    y = d * lax.rsqrt(var + RWKV_LN_EPS) * lng_ref[...] + lnb_ref[...]
    out_ref[...] = ((y + bonus_ref[...]) * g_ref[...]).astype(BF16)


def rwkv_post(o, g, bonus, P, tm=256):
    n = o.shape[0]
    e = jnp.asarray(_blockdiag_ones(HB, HS_B), BF16)
    blk = pl.BlockSpec((tm, HB), lambda i: (i, 0))
    row = pl.BlockSpec((1, HB), lambda i: (0, 0))
    return pl.pallas_call(
        _rwkv_post_body,
        grid=(n // tm,),
        in_specs=[blk, blk, blk, row, row, pl.BlockSpec(e.shape, lambda i: (0, 0))],
        out_specs=blk,
        out_shape=jax.ShapeDtypeStruct((n, HB), BF16),
        compiler_params=_cp("parallel"),
        name="rwkv_post",
    )(o, g, bonus, P['ln_g'].reshape(1, -1), P['ln_b'].reshape(1, -1), e)


def rwkv_mix(proj, prev, state0, b, t, P, nb, tc, tm):
    pr = lax.slice_in_dim(proj, EVEN_PR_OFF, EVEN_W, axis=1).reshape(b, t, RW_WIDTH)
    shifted = jnp.concatenate([prev[:, None, :], pr[:, :-1, :]], axis=1).reshape(b * t, RW_WIDTH)
    outs = rwkv_prep(proj, shifted, P, tm)
    o, s_new = rwkv_scan(outs[:8], state0, b, t, nb, tc)
    return rwkv_post(o, outs[8], outs[9], P, tm), pr[:, -1, :], s_new


ODD_Z_OFF = 3 * HC
ODD_XBC_OFF = ODD_Z_OFF + D_INNER
ODD_DT_OFF = ODD_XBC_OFF + CONV_DIM
ODD_W = ODD_DT_OFF + LANES
SB_SCALE = HD_C ** -0.5


def _sb_weights(z, mask, suffix_of, run):
    lneg = -_softplus(z)
    if mask is not None:
        lneg = jnp.where(mask, lneg, 0.0)
    w = jnp.exp(z + lneg + suffix_of(lneg) + run)
    if mask is not None:
        w = jnp.where(mask, w, 0.0)
    return w, lneg


def _sb_prompt_body(q_ref, k_ref, v_ref, u_ref, o_ref, *, tq):
    qi = pl.program_id(1)
    row = lax.broadcasted_iota(I32, (tq, tq), 0)
    col = lax.broadcasted_iota(I32, (tq, tq), 1)
    for h in range(H_C):
        sl = slice(h * HD_C, (h + 1) * HD_C)
        q = q_ref[:, sl].astype(BF16)

        def body(i, carry):
            run, acc = carry
            kb = qi - i
            off = pl.multiple_of(kb * tq, tq)
            z = _dot_nt(q, k_ref[pl.ds(off, tq), sl].astype(BF16)) * SB_SCALE
            mask = col + kb * tq < row + qi * tq
            w, lneg = _sb_weights(z, mask, lambda l: _dot_f32(l, u_ref[...]), run)
            acc = acc + _dot(w.astype(BF16), v_ref[pl.ds(off, tq), sl].astype(BF16))
            return run + jnp.sum(lneg, axis=-1, keepdims=True), acc

        _, acc = lax.fori_loop(0, qi + 1, body, (jnp.zeros((tq, 1), F32), jnp.zeros((tq, HD_C), F32)))
        o_ref[:, sl] = acc.astype(BF16)


def sb_prompt_attention(proj, b, t, tq=256):
    nq = t // tq
    u = jnp.asarray(np.tril(np.ones((tq, tq), np.float32), -1), BF16)
    return pl.pallas_call(
        functools.partial(_sb_prompt_body, tq=tq),
        grid=(b, nq),
        in_specs=[pl.BlockSpec((tq, HC), lambda bi, qi: (bi * nq + qi, 0)),
                  pl.BlockSpec((t, HC), lambda bi, qi: (bi, 1)),
                  pl.BlockSpec((t, HC), lambda bi, qi: (bi, 2)),
                  pl.BlockSpec(u.shape, lambda bi, qi: (0, 0))],
        out_specs=pl.BlockSpec((tq, HC), lambda bi, qi: (bi * nq + qi, 0)),
        out_shape=jax.ShapeDtypeStruct((b * t, HC), BF16),
        compiler_params=_cp("parallel", "arbitrary"),
        name="sb_prompt_attention",
    )(proj, proj, proj, u)


def _kv_heads_body(k_ref, v_ref, eye_ref, kt_ref, vt_ref):
    for src, dst in ((k_ref, kt_ref), (v_ref, vt_ref)):
        x = src[...]
        hi = x.astype(BF16)
        r1 = x - hi.astype(F32)
        mid = r1.astype(BF16)
        lo = (r1 - mid.astype(F32)).astype(BF16)
        for h in range(H_C):
            sl = slice(h * HD_C, (h + 1) * HD_C)
            eye = eye_ref[...]
            dst[0, h] = (_dot_nt(eye, hi[:, sl]) + _dot_nt(eye, mid[:, sl])) + _dot_nt(eye, lo[:, sl])


def sb_kv_state(proj, b, t, tq=512):
    tq = min(tq, t)
    assert t % tq == 0
    nq = t // tq
    eye = jnp.eye(HD_C, dtype=BF16)
    out = pl.BlockSpec((1, H_C, HD_C, tq), lambda bi, qi: (bi, 0, 0, qi))
    kt, vt = pl.pallas_call(
        _kv_heads_body,
        grid=(b, nq),
        in_specs=[pl.BlockSpec((tq, HC), lambda bi, qi: (bi * nq + qi, 1)),
                  pl.BlockSpec((tq, HC), lambda bi, qi: (bi * nq + qi, 2)),
                  pl.BlockSpec(eye.shape, lambda bi, qi: (0, 0))],
        out_specs=[out, out],
        out_shape=[jax.ShapeDtypeStruct((b, H_C, HD_C, t), F32)] * 2,
        compiler_params=_cp("parallel", "parallel"),
        name="sb_kv_state",
    )(proj, proj, eye)
    return jnp.transpose(kt, (0, 3, 1, 2)), jnp.transpose(vt, (0, 3, 1, 2))


def _sb_sample_body(pt_ref, *refs, t_new):
    npg = PAGES_PER_STEP
    k_pages, v_pages = refs[:npg], refs[npg:2 * npg]
    k_new, v_new, q_ref, u_ref, o_ref, q_scr, run_scr, acc_scr = refs[2 * npg:]
    j = pl.program_id(1)
    nq = H_C * t_new
    row_head = lax.broadcasted_iota(I32, (nq, PAGE_SIZE), 0) // t_new
    hsl = lambda h: slice(h * HD_C, (h + 1) * HD_C)

    def process(k_of, v_of, mask, dot_k, dot_v):
        z = dot_k(q_scr[0], k_of(0))
        for h in range(1, H_C):
            z = z + dot_k(q_scr[h], k_of(h))
        w, lneg = _sb_weights(z * SB_SCALE, mask, lambda l: _dot_f32(l, u_ref[...]), run_scr[...])
        acc = acc_scr[...]
        for h in range(H_C):
            acc = acc + dot_v(jnp.where(row_head == h, w, 0.0).astype(BF16), v_of(h))
        acc_scr[...] = acc
        run_scr[...] += jnp.sum(lneg, axis=1, keepdims=True)

    @pl.when(j == 0)
    def _():
        run_scr[...] = jnp.zeros(run_scr.shape, F32)
        acc_scr[...] = jnp.zeros(acc_scr.shape, F32)
        q = q_ref[0]
        for h in range(H_C):
            rows = [jnp.zeros((t_new, HD_C), F32)] * H_C
            rows[h] = q[:, hsl(h)]
            q_scr[h] = jnp.concatenate(rows, axis=0).astype(BF16)
        fill = jnp.zeros((PAGE_SIZE - t_new, HC), F32)
        k_pad = jnp.concatenate([k_new[0], fill], axis=0).astype(BF16)
        v_pad = jnp.concatenate([v_new[0], fill], axis=0).astype(BF16)
        key = lax.broadcasted_iota(I32, (nq, PAGE_SIZE), 1)
        qt = lax.broadcasted_iota(I32, (nq, PAGE_SIZE), 0) % t_new
        process(lambda h: k_pad[:, hsl(h)], lambda h: v_pad[:, hsl(h)], key < qt, _dot_nt, _dot)

    for p in range(npg):
        process(lambda h: k_pages[p][0, h].astype(BF16), lambda h: v_pages[p][0, h].astype(BF16), None, _dot, _dot_nt)

    @pl.when(j == KV_STEPS - 1)
    def _():
        acc = acc_scr[...]
        for h in range(H_C):
            o_ref[0, :, h * HD_C:(h + 1) * HD_C] = acc[h * t_new:(h + 1) * t_new].astype(BF16)


def sb_sample_attention(page_table, cache_k, cache_v, proj, b, t_new):
    proj = proj.reshape(b, t_new, ODD_W)
    nq = H_C * t_new
    keys_last = lambda a: jnp.transpose(a, (0, 2, 3, 1))
    cache_k, cache_v = keys_last(cache_k), keys_last(cache_v)
    u = jnp.asarray(np.tril(np.ones((PAGE_SIZE, PAGE_SIZE), np.float32), -1), BF16)
    npg = PAGES_PER_STEP
    page = lambda p: pl.BlockSpec(
        (1, H_C, HD_C, PAGE_SIZE), lambda bi, j, pt: (pt[bi * N_PAGES + (N_PAGES - 1 - (j * npg + p))], 0, 0, 0))
    col = lambda c: pl.BlockSpec((1, t_new, HC), lambda bi, j, pt: (bi, 0, c))
    grid_spec = pltpu.PrefetchScalarGridSpec(
        num_scalar_prefetch=1,
        grid=(b, KV_STEPS),
        in_specs=[page(p) for p in range(npg)] * 2
        + [col(1), col(2), col(0), pl.BlockSpec(u.shape, lambda bi, j, pt: (0, 0))],
        out_specs=col(0),
        scratch_shapes=[pltpu.VMEM((H_C, nq, HD_C), BF16), pltpu.VMEM((nq, 1), F32), pltpu.VMEM((nq, HD_C), F32)],
    )
    out = pl.pallas_call(
        functools.partial(_sb_sample_body, t_new=t_new),
        grid_spec=grid_spec,
        out_shape=jax.ShapeDtypeStruct((b, t_new, HC), BF16),
        compiler_params=_cp("parallel", "arbitrary"),
        name="sb_sample_attention",
    )(page_table.reshape(-1), *([cache_k] * npg), *([cache_v] * npg), proj, proj, proj, u)
    return out.reshape(b * t_new, HC)


TAIL = 8


def _ssd_body(z_ref, xbc_ref, dt_ref, cbuf_ref, h0_ref, cw_ref, cb_ref, dtb_ref, alog_ref, drow_ref, gn_ref, tri_ref,
              y_ref, hf_ref, h_scr, tail_scr, y_scr, *, L):
    c = pl.program_id(1)
    mm = (lambda x: x.astype(BF16)) if L >= 16 else (lambda x: x.astype(BF16).astype(F32))

    @pl.when(c == 0)
    def _():
        h_scr[...] = h0_ref[0]
        tail_scr[...] = jnp.zeros(tail_scr.shape, F32)
        tail_scr[TAIL - (CONV_W - 1):, :] = cbuf_ref[0]

    xbc = xbc_ref[...]
    ext = jnp.concatenate([tail_scr[...], xbc], axis=0)
    conv = cb_ref[...]
    for i in range(CONV_W):
        s = TAIL - (CONV_W - 1) + i
        conv = conv + cw_ref[i:i + 1, :] * ext[s:s + L]
    tail_scr[...] = xbc[L - TAIL:]
    xc = conv * _sigmoid(conv)
    dt = _softplus(dt_ref[...] + dtb_ref[...])
    a = -jnp.exp(alog_ref[...])
    lane = lax.broadcasted_iota(I32, (1, LANES), 1)
    dta = jnp.where(lane < H_D, dt * a, 0.0)
    hi = dta.astype(BF16)
    r1 = dta - hi.astype(F32)
    mid = r1.astype(BF16)
    lo = (r1 - mid.astype(F32)).astype(BF16)
    tri = tri_ref[...]
    acs = (_dot(tri, mm(hi)) + _dot(tri, mm(mid))) + _dot(tri, mm(lo))
    hi, mid, lo = acs.astype(BF16), None, None
    r1 = acs - hi.astype(F32)
    mid = r1.astype(BF16)
    lo = (r1 - mid.astype(F32)).astype(BF16)
    row = lax.broadcasted_iota(I32, (L, L), 0)
    col = lax.broadcasted_iota(I32, (L, L), 1)
    lane_l = lax.broadcasted_iota(I32, (L, LANES), 1)
    acs_last = acs[L - 1:L, :]
    cbs = []
    for g in range(G_D):
        bg = xc[:, D_INNER + g * N_D:D_INNER + (g + 1) * N_D]
        cg = xc[:, D_INNER + G_D * N_D + g * N_D:D_INNER + G_D * N_D + (g + 1) * N_D]
        cbs.append((bg, cg, _dot_nt(mm(cg), mm(bg))))
    for h in range(H_D):
        bg, cg, cb = cbs[h // (H_D // G_D)]
        sel = mm((lane_l == h).astype(F32))
        acs_row = (_dot_nt(sel, mm(hi)) + _dot_nt(sel, mm(mid))) + _dot_nt(sel, mm(lo))
        acs_col = acs[:, h:h + 1]
        lmat = jnp.exp(jnp.where(col <= row, acs_col - acs_row, -jnp.inf))
        xh = xc[:, h * P_D:(h + 1) * P_D]
        xdt = xh * dt[:, h:h + 1]
        hprev = h_scr[h]
        y = _dot(mm(cb * lmat), mm(xdt)) + _dot_nt(mm(cg * jnp.exp(acs_col)), mm(hprev))
        y_scr[:, h * P_D:(h + 1) * P_D] = y + drow_ref[:, h * P_D:(h + 1) * P_D] * xh
        last = acs_last[:, h:h + 1]
        bw = bg * jnp.exp(last - acs_col)
        if L >= 16:
            st = _dot_tn(mm(xdt), mm(bw))
        else:
            xr, br = xdt - mm(xdt), bw - mm(bw)
            st = (_dot_tn(mm(xdt), mm(bw)) + _dot_tn(mm(xdt), mm(br))) + _dot_tn(mm(xr), mm(bw))
        h_scr[h] = hprev * jnp.exp(last) + st
    z = z_ref[...]
    y_ref[...] = _rms(y_scr[...] * (z * _sigmoid(z)), gn_ref[...]).astype(BF16)

    @pl.when(c == pl.num_programs(1) - 1)
    def _():
        hf_ref[0] = h_scr[...]


def ssd_mix(proj, conv_buf, state0, b, t, P):
    L = SSD_CHUNK if t % SSD_CHUNK == 0 else t
    assert L % TAIL == 0
    nc = t // L
    pad_h = lambda v: jnp.pad(v.reshape(1, -1), ((0, 0), (0, LANES - H_D)))
    consts = [P['conv_w'], P['conv_b'].reshape(1, -1), pad_h(P['dt_bias']), pad_h(P['a_log']),
              jnp.repeat(P['d'], P_D).reshape(1, -1), P['g_norm'].reshape(1, -1)]
    tri_np = np.tril(np.ones((L, L), np.float32))
    tri = jnp.asarray(tri_np, BF16 if L >= 16 else F32)
    zc, xc, dc = ODD_Z_OFF // D_INNER, ODD_XBC_OFF // CONV_DIM, ODD_DT_OFF // LANES
    assert ODD_Z_OFF % D_INNER == 0 and ODD_XBC_OFF % CONV_DIM == 0 and ODD_DT_OFF % LANES == 0
    y, hf = pl.pallas_call(
        functools.partial(_ssd_body, L=L),
        grid=(b, nc),
        in_specs=[pl.BlockSpec((L, D_INNER), lambda bi, c: (bi * nc + c, zc)),
                  pl.BlockSpec((L, CONV_DIM), lambda bi, c: (bi * nc + c, xc)),
                  pl.BlockSpec((L, LANES), lambda bi, c: (bi * nc + c, dc)),
                  pl.BlockSpec((1, CONV_W - 1, CONV_DIM), lambda bi, c: (bi, 0, 0)),
                  pl.BlockSpec((1, H_D, P_D, N_D), lambda bi, c: (bi, 0, 0, 0))]
        + [pl.BlockSpec(cst.shape, lambda bi, c: (0, 0)) for cst in consts]
        + [pl.BlockSpec(tri.shape, lambda bi, c: (0, 0))],
        out_specs=[pl.BlockSpec((L, D_INNER), lambda bi, c: (bi * nc + c, 0)),
                   pl.BlockSpec((1, H_D, P_D, N_D), lambda bi, c: (bi, 0, 0, 0))],
        out_shape=[jax.ShapeDtypeStruct((b * t, D_INNER), BF16), jax.ShapeDtypeStruct((b, H_D, P_D, N_D), F32)],
        scratch_shapes=[pltpu.VMEM((H_D, P_D, N_D), F32), pltpu.VMEM((TAIL, CONV_DIM), F32),
                        pltpu.VMEM((L, D_INNER), F32)],
        compiler_params=_cp("parallel", "arbitrary"),
        name="ssd_mix",
    )(proj, proj, proj, conv_buf, state0, *consts, tri)
    return y, hf


SC_L = 16
SC_TB = 16
GELU_C = math.sqrt(2.0 / math.pi)


def _sc_gelu(a):
    y = GELU_C * (a + 0.044715 * (a * a * a))
    return 0.5 * a * (2.0 - 2.0 / (jnp.exp(2.0 * y) + 1.0))


def peer_experts(hn, idx, gate, u_tab, v_tab, x):
    n = hn.shape[0]
    info = plsc.get_sparse_core_info()
    nc, ns = info.num_cores, info.num_subcores
    assert info.num_lanes == SC_L and PEER_TOPK == SC_L
    nw = nc * ns
    per_w = n // nw
    assert n % (nw * SC_TB) == 0
    n_blk = per_w // SC_TB
    n_chunk = SC_TB * PEER_HEADS
    n_vec = D_MODEL // SC_L
    mesh = plsc.VectorSubcoreMesh(core_axis_name="c", subcore_axis_name="s")

    @functools.partial(
        pl.kernel, mesh=mesh, out_type=jax.ShapeDtypeStruct((n, D_MODEL), F32),
        scratch_types=[pltpu.VMEM((SC_TB, D_MODEL), F32), pltpu.VMEM((SC_TB, D_MODEL), F32),
                       pltpu.VMEM((SC_TB, PEER_HEADS, SC_L), I32), pltpu.VMEM((SC_TB, PEER_HEADS * SC_L), F32),
                       pltpu.VMEM((2, SC_L, D_MODEL), F32), pltpu.VMEM((2, SC_L, D_MODEL), F32),
                       pltpu.SemaphoreType.DMA((2,)), pltpu.SemaphoreType.DMA((2,))],
        compiler_params=pltpu.CompilerParams(needs_layout_passes=False),
        name="peer_experts")
    def run(hn_hbm, idx_hbm, gate_hbm, u_hbm, v_hbm, x_hbm, out_hbm, hnv, outv, idxv, gv, ubuf, vbuf, usem, vsem):
        wid = lax.axis_index("s") * nc + lax.axis_index("c")
        lanes = lax.iota(I32, SC_L)

        def copies(q, slot):
            tl, hd = q // PEER_HEADS, q % PEER_HEADS
            rows = idxv.at[tl, hd]
            return (pltpu.make_async_copy(u_hbm.at[rows], ubuf.at[slot], usem.at[slot]),
                    pltpu.make_async_copy(v_hbm.at[rows], vbuf.at[slot], vsem.at[slot]))

        def start(q, slot):
            for cp in copies(q, slot):
                cp.start()

        def compute(q, slot):
            tl, hd = q // PEER_HEADS, q % PEER_HEADS
            cu, cv = copies(q, slot)
            cu.wait()

            def ubody(c, accs):
                off = pl.multiple_of(c * SC_L, SC_L)
                xc = hnv[tl, pl.ds(off, SC_L)]
                return tuple(accs[j] + ubuf[slot, j, pl.ds(off, SC_L)] * xc for j in range(SC_L))

            zero = jnp.zeros((SC_L,), F32)
            accs = lax.fori_loop(0, n_vec, ubody, (zero,) * SC_L)
            acts = zero
            for j in range(SC_L):
                acts = jnp.where(lanes == j, jnp.sum(accs[j]), acts)
            coef = gv[tl, pl.ds(pl.multiple_of(hd * SC_L, SC_L), SC_L)] * _sc_gelu(acts)
            cj = [jnp.sum(jnp.where(lanes == j, coef, 0.0)) for j in range(SC_L)]
            cv.wait()

            def vbody(c, carry):
                off = pl.multiple_of(c * SC_L, SC_L)
                o = outv[tl, pl.ds(off, SC_L)]
                for j in range(SC_L):
                    o = o + cj[j] * vbuf[slot, j, pl.ds(off, SC_L)]
                outv[tl, pl.ds(off, SC_L)] = o
                return carry

            lax.fori_loop(0, n_vec, vbody, 0)

        def block(bi, carry):
            t0 = pl.multiple_of(wid * per_w + bi * SC_TB, SC_TB)
            pltpu.sync_copy(hn_hbm.at[pl.ds(t0, SC_TB)], hnv)
            pltpu.sync_copy(x_hbm.at[pl.ds(t0, SC_TB)], outv)
            pltpu.sync_copy(idx_hbm.at[pl.ds(t0, SC_TB)], idxv)
            pltpu.sync_copy(gate_hbm.at[pl.ds(t0, SC_TB)], gv)
            start(0, 0)

            def pair(p, c2):
                q = 2 * p
                start(q + 1, 1)
                compute(q, 0)

                @pl.when(q + 2 < n_chunk)
                def _():
                    start(q + 2, 0)

                compute(q + 1, 1)
                return c2

            lax.fori_loop(0, n_chunk // 2, pair, 0)
            pltpu.sync_copy(outv, out_hbm.at[pl.ds(t0, SC_TB)])
            return carry

        lax.fori_loop(0, n_blk, block, 0)

    return run(hn, idx.reshape(n, PEER_HEADS, SC_L), gate, u_tab, v_tab, x)


def _even_w_in(w):
    out = jnp.zeros((D_MODEL, EVEN_W), F32)
    out = out.at[:, :EVEN_KR_OFF].set(w[:, :D_CQ + D_C])
    out = out.at[:, EVEN_KR_OFF + DN_A:EVEN_KR_OFF + DN_A + DR_A].set(w[:, D_CQ + D_C:D_CQ + D_C + DR_A])
    out = out.at[:, EVEN_PR_OFF:].set(w[:, D_CQ + D_C + DR_A:])
    return out.astype(BF16)


def _even_prompt(x, b, t, pos, g_mix, P, W):
    proj = norm_matmul(x, g_mix, W['in'], after=W['after'])
    qc, kc, vc, ckv, kr = mla_prep(proj, pos, P)
    o_a = mla_prompt_attention(qc, kc, vc, b, t)
    o_b, sh, wkv = rwkv_mix(proj, jnp.zeros((b, RW_WIDTH), F32), jnp.zeros((b, H_B, HS_B, HS_B), F32),
                            b, t, P, nb=min(b, 4), tc=64, tm=256)
    x = matmul_res([o_a, o_b], W['out'], x)
    return x, lambda: (ckv.reshape(b, t, D_C), kr[:, DN_A:DN_A + DR_A].reshape(b, t, DR_A), sh, wkv)


def _even_sample(x, b, t, pos, page_table, cache_ckv, cache_kr, rw_shift, rw_wkv, g_mix, P, W):
    proj = norm_matmul(x, g_mix, W['in'], after=W['after'])
    qc, _, _, ckv, kr = mla_prep(proj, pos, P)
    qc = qc.reshape(b, t, H_A, HG)
    ckv = ckv.reshape(b, t, D_C)
    kr = kr[:, DN_A:DN_A + DR_A].reshape(b, t, DR_A)
    o_a = mla_sample_attention(page_table, cache_ckv, cache_kr, ckv, kr, qc[..., :DN_A], qc[..., DN_A:DN_A + DR_A], P)
    o_b, sh, wkv = rwkv_mix(proj, rw_shift, rw_wkv, b, t, P, nb=8, tc=t, tm=256)
    x = matmul_res([o_a, o_b], W['out'], x)
    return x, lambda: (ckv, kr, sh, wkv)


def _odd_states(proj, b, t):
    p3 = proj.reshape(b, t, ODD_W)
    if t % LANES == 0:
        k_new, v_new = sb_kv_state(proj, b, t)
    else:
        k_new, v_new = (p3[:, :, c * HC:(c + 1) * HC].reshape(b, t, H_C, HD_C) for c in (1, 2))
    return k_new, v_new, p3[:, t - (CONV_W - 1):, ODD_XBC_OFF:ODD_XBC_OFF + CONV_DIM]


def _odd_prompt(x, b, t, g_mix, P, W):
    proj = norm_matmul(x, g_mix, W['in'], after=W['after'])
    o_c = sb_prompt_attention(proj, b, t)
    y, ssm = ssd_mix(proj, jnp.zeros((b, CONV_W - 1, CONV_DIM), F32), jnp.zeros((b, H_D, P_D, N_D), F32), b, t, P)
    x = matmul_res([o_c, y], W['out'], x)
    return x, lambda: _odd_states(proj, b, t) + (ssm,)


def _odd_sample(x, b, t, page_table, cache_k, cache_v, conv_state, ssm_state, g_mix, P, W):
    proj = norm_matmul(x, g_mix, W['in'], after=W['after'])
    o_c = sb_sample_attention(page_table, cache_k, cache_v, proj, b, t)
    y, ssm = ssd_mix(proj, conv_state, ssm_state, b, t, P)
    x = matmul_res([o_c, y], W['out'], x)
    return x, lambda: _odd_states(proj, b, t) + (ssm,)


PROMPT_SPLITS = 4


def kernel(x_prompt, x_sample, mem_prompt, page_table, cache_mla_ckv, cache_mla_krope, state_rwkv_shift, state_rwkv_wkv, cache_sb_k, cache_sb_v, state_ssm_conv, state_ssm, cache_mem_k, cache_mem_v, norm_mix, norm_mem, norm_ffn, w_in_even, w_out_even, mla_g_cq, mla_w_uq, mla_g_ckv, mla_w_uk, mla_w_uv, mla_g_qn, mla_g_kn, mla_g_qr, mla_g_kr, rw_mu, rw_w0, rw_w_up, rw_a0, rw_a_up, rw_g_up, rw_k_k, rw_k_a, rw_r_k, rw_ln_g, rw_ln_b, w_in_odd, w_out_odd, ssm_conv_w, ssm_conv_b, ssm_dt_bias, ssm_a_log, ssm_d, ssm_g_norm, mem_g_src, mem_w_q, mem_w_kv, mem_g_q, mem_g_k, mem_w_o, peer_w_q, peer_sub_keys, peer_u, peer_v):
    bp, tp = x_prompt.shape[:2]
    bs, ts = x_sample.shape[:2]
    depth = norm_mix.shape[0]
    nsp = PROMPT_SPLITS if bp % PROMPT_SPLITS == 0 else 1
    bh = bp // nsp
    part = lambda a, k: a[k * bh:(k + 1) * bh]
    xps = [part(x_prompt, k).reshape(bh * tp, D_MODEL) for k in range(nsp)]
    xs = x_sample.reshape(bs * ts, D_MODEL)
    pos_p = jnp.tile(jnp.arange(tp, dtype=I32), bh)
    pos_s = jnp.tile(PAST_LEN + jnp.arange(ts, dtype=I32), bs)
    even_p, even_s, odd_p, odd_s, mem_k, mem_v = [], [], [], [], [], []
    tc_done = None
    for layer in range(depth):
        i = layer // 2
        even = layer % 2 == 0
        g_mix = norm_mix[layer]
        if even:
            P = dict(w_in=w_in_even[i], w_out=w_out_even[i], g_cq=mla_g_cq[i], w_uq=mla_w_uq[i],
                     g_ckv=mla_g_ckv[i], w_uk=mla_w_uk[i], w_uv=mla_w_uv[i], g_qn=mla_g_qn[i],
                     g_kn=mla_g_kn[i], g_qr=mla_g_qr[i], g_kr=mla_g_kr[i],
                     mu=rw_mu[i], w0=rw_w0[i], w_up=rw_w_up[i], a0=rw_a0[i], a_up=rw_a_up[i],
                     g_up=rw_g_up[i], k_k=rw_k_k[i], k_a=rw_k_a[i], r_k=rw_r_k[i],
                     ln_g=rw_ln_g[i], ln_b=rw_ln_b[i])
            w_out = P['w_out'].astype(BF16)
            W = {'in': _even_w_in(P['w_in']), 'out': [w_out[:H_A * DV_A], w_out[H_A * DV_A:]]}
        else:
            P = dict(w_in=w_in_odd[i], w_out=w_out_odd[i], conv_w=ssm_conv_w[i], conv_b=ssm_conv_b[i],
                     dt_bias=ssm_dt_bias[i], a_log=ssm_a_log[i], d=ssm_d[i], g_norm=ssm_g_norm[i])
            w_out = P['w_out'].astype(BF16)
            W = {'in': jnp.pad(P['w_in'], ((0, 0), (0, ODD_W - P['w_in'].shape[1]))).astype(BF16),
                 'out': [w_out[:HC], w_out[HC:]]}
        mem = lambda x, t, k, v, tq: memory_attend(x, t, norm_mem[layer], mem_w_q[layer], mem_g_q[layer], k, v,
                                                   mem_w_o[layer], tq=tq)

        def peer(x):
            hn, idx, gate = peer_route(x, norm_ffn[layer], peer_w_q[layer], peer_sub_keys[layer])
            return peer_experts(hn, idx, gate, peer_u[layer], peer_v[layer], x), idx

        mk, mv = memory_kv(mem_prompt.reshape(bp * N_MEM, D_MODEL), mem_g_src[layer], mem_w_kv[layer], mem_g_k[layer])
        mem_k.append(mk.reshape(bp, N_MEM, MEM_HEADS, MEM_HD))
        mem_v.append(mv.reshape(bp, N_MEM, MEM_HEADS, MEM_HD))
        mk, mv = mk.reshape(bp, N_MEM, MEM_W), mv.reshape(bp, N_MEM, MEM_W)
        parts = []
        for k in range(nsp):
            W['after'] = tc_done
            if even:
                x, sp = _even_prompt(xps[k], bh, tp, pos_p, g_mix, P, W)
            else:
                x, sp = _odd_prompt(xps[k], bh, tp, g_mix, P, W)
            parts.append(sp)
            xps[k], tc_done = peer(mem(x, tp, part(mk, k), part(mv, k), 256))
        (even_p if even else odd_p).append(parts)
        W['after'] = tc_done
        if even:
            xs, ss = _even_sample(xs, bs, ts, pos_s, page_table, cache_mla_ckv[i], cache_mla_krope[i],
                                  state_rwkv_shift[i], state_rwkv_wkv[i], g_mix, P, W)
            even_s.append(ss)
        else:
            xs, ss = _odd_sample(xs, bs, ts, page_table, cache_sb_k[i], cache_sb_v[i], state_ssm_conv[i],
                                 state_ssm[i], g_mix, P, W)
            odd_s.append(ss)
        xs, tc_done = peer(mem(xs, ts, cache_mem_k[layer].reshape(bs, N_MEM, MEM_W),
                               cache_mem_v[layer].reshape(bs, N_MEM, MEM_W), ts))
    def join(parts):
        parts = [p() for p in parts]
        return tuple(jnp.concatenate([p[j] for p in parts], axis=0) for j in range(len(parts[0])))

    even_p, odd_p = [join(p) for p in even_p], [join(p) for p in odd_p]
    even_s, odd_s = [s() for s in even_s], [s() for s in odd_s]
    stack = lambda groups, k: jnp.stack([g[k] for g in groups])
    xp = jnp.concatenate(xps, axis=0)
    return (xp.reshape(bp, tp, D_MODEL), xs.reshape(bs, ts, D_MODEL),
            stack(even_p, 0), stack(even_p, 1), stack(even_p, 2), stack(even_p, 3),
            stack(odd_p, 0), stack(odd_p, 1), stack(odd_p, 2), stack(odd_p, 3),
            jnp.stack(mem_k), jnp.stack(mem_v),
            stack(even_s, 0), stack(even_s, 1), stack(even_s, 2), stack(even_s, 3),
            stack(odd_s, 0), stack(odd_s, 1), stack(odd_s, 2), stack(odd_s, 3))
```

```python
import functools
import math

import numpy as np
import jax
import jax.numpy as jnp
from jax import lax
from jax.experimental import pallas as pl
from jax.experimental.pallas import tpu as pltpu
from jax.experimental.pallas import tpu_sc as plsc

F32 = jnp.float32
BF16 = jnp.bfloat16
I32 = jnp.int32

D_MODEL = 1024
EPS = 1e-6
PAST_LEN = 8192
PAGE_SIZE = 128
N_PAGES = PAST_LEN // PAGE_SIZE

H_A, DN_A, DR_A, DV_A, D_CQ, D_C = 8, 64, 32, 64, 384, 256
ROPE_BASE = 10000.0
MLA_SCALE = (DN_A + DR_A) ** -0.5
H_B, HS_B, D_WL, D_AL, D_GL = 8, 64, 64, 64, 128
HB = H_B * HS_B
RW_WIDTH = 3 * HB + D_WL + D_AL + D_GL
RWKV_DECAY_SCALE = 0.606531
RWKV_LN_EPS = 64e-5
H_C, HD_C = 8, 64
HC = H_C * HD_C
H_D, P_D, N_D, G_D, CONV_W = 8, 64, 128, 2, 4
D_INNER = H_D * P_D
CONV_DIM = D_INNER + 2 * G_D * N_D
SSD_CHUNK = 128
N_MEM, MEM_HEADS, MEM_HD = 256, 4, 64
MEM_W = MEM_HEADS * MEM_HD
PEER_HEADS, N_KEYS, PEER_DQ, PEER_TOPK = 8, 128, 256, 16
LANES = 128
VMEM_LIMIT = 56 * 1024 * 1024


def _cp(*sem):
    return pltpu.CompilerParams(dimension_semantics=sem, vmem_limit_bytes=VMEM_LIMIT)


def _dot(a, b):
    return jnp.dot(a, b, preferred_element_type=F32)


def _dot_nt(a, b):
    return lax.dot_general(a, b, (((1,), (1,)), ((), ())), preferred_element_type=F32)


def _dot_tn(a, b):
    return lax.dot_general(a, b, (((0,), (0,)), ((), ())), preferred_element_type=F32)


def _split2(x):
    hi = x.astype(BF16)
    lo = (x - hi.astype(F32)).astype(BF16)
    return hi, lo


def _dot_f32(x, m):
    hi, lo = _split2(x)
    return _dot(hi, m) + _dot(lo, m)


def _dot_sel(x, m):
    hi = x.astype(BF16)
    r1 = x - hi.astype(F32)
    mid = r1.astype(BF16)
    lo = (r1 - mid.astype(F32)).astype(BF16)
    return (_dot(hi, m) + _dot(mid, m)) + _dot(lo, m)


def _rms(x, g):
    return x * lax.rsqrt(jnp.mean(x * x, axis=-1, keepdims=True) + EPS) * g


def _sigmoid(x):
    return 1.0 / (1.0 + jnp.exp(-x))


def _softplus(x):
    return jnp.maximum(x, 0.0) + jnp.log(1.0 + jnp.exp(-jnp.abs(x)))


def _blockdiag_ones(n, seg):
    i = np.arange(n)
    return (i[:, None] // seg == i[None, :] // seg).astype(np.float32)


def _nm_body(x_ref, g_ref, w_ref, *rest):
    o_ref = rest[-1]
    h = _rms(x_ref[...], g_ref[...])
    o_ref[...] = _dot(h.astype(BF16), w_ref[...])


def norm_matmul(x, g, w, tm=256, after=None):
    n, k = x.shape
    m = w.shape[1]
    extra = [] if after is None else [after]
    return pl.pallas_call(
        _nm_body,
        grid=(n // tm,),
        in_specs=[pl.BlockSpec((tm, k), lambda i: (i, 0)),
                  pl.BlockSpec((1, k), lambda i: (0, 0)),
                  pl.BlockSpec((k, m), lambda i: (0, 0))] + [pl.BlockSpec(memory_space=pl.ANY) for _ in extra],
        out_specs=pl.BlockSpec((tm, m), lambda i: (i, 0)),
        out_shape=jax.ShapeDtypeStruct((n, m), F32),
        compiler_params=_cp("parallel"),
        name="norm_matmul",
    )(x, g.reshape(1, k), w, *extra)


def _mr_body(*refs, n_in):
    a_refs, w_refs, r_ref, o_ref = refs[:n_in], refs[n_in:2 * n_in], refs[2 * n_in], refs[2 * n_in + 1]
    acc = r_ref[...]
    for a_ref, w_ref in zip(a_refs, w_refs):
        acc = acc + _dot(a_ref[...].astype(BF16), w_ref[...])
    o_ref[...] = acc


def matmul_res(a_list, w_list, res, tm=256):
    n, m = res.shape
    n_in = len(a_list)
    in_specs = ([pl.BlockSpec((tm, a.shape[1]), lambda i: (i, 0)) for a in a_list]
                + [pl.BlockSpec(w.shape, lambda i: (0, 0)) for w in w_list]
                + [pl.BlockSpec((tm, m), lambda i: (i, 0))])
    return pl.pallas_call(
        functools.partial(_mr_body, n_in=n_in),
        grid=(n // tm,),
        in_specs=in_specs,
        out_specs=pl.BlockSpec((tm, m), lambda i: (i, 0)),
        out_shape=jax.ShapeDtypeStruct((n, m), F32),
        compiler_params=_cp("parallel"),
        name="matmul_res",
    )(*a_list, *w_list, res)


def _memkv_body(m_ref, g_ref, w_ref, gk_ref, e_ref, k_ref, v_ref):
    h = _rms(m_ref[...], g_ref[...])
    kv = _dot(h.astype(BF16), w_ref[...])
    k = kv[:, :MEM_W]
    ms = _dot_f32(k * k, e_ref[...]) * (1.0 / MEM_HD)
    k_ref[...] = k * lax.rsqrt(ms + EPS) * gk_ref[...]
    v_ref[...] = kv[:, MEM_W:]


def memory_kv(mem2d, g_src, w_kv, g_k, tm=256):
    n = mem2d.shape[0]
    e = jnp.asarray(_blockdiag_ones(MEM_W, MEM_HD), BF16)
    gk = jnp.tile(g_k, MEM_HEADS).reshape(1, MEM_W)
    return pl.pallas_call(
        _memkv_body,
        grid=(n // tm,),
        in_specs=[pl.BlockSpec((tm, D_MODEL), lambda i: (i, 0)),
                  pl.BlockSpec((1, D_MODEL), lambda i: (0, 0)),
                  pl.BlockSpec((D_MODEL, 2 * MEM_W), lambda i: (0, 0)),
                  pl.BlockSpec((1, MEM_W), lambda i: (0, 0)),
                  pl.BlockSpec((MEM_W, MEM_W), lambda i: (0, 0))],
        out_specs=[pl.BlockSpec((tm, MEM_W), lambda i: (i, 0))] * 2,
        out_shape=[jax.ShapeDtypeStruct((n, MEM_W), F32)] * 2,
        compiler_params=_cp("parallel"),
        name="memory_kv",
    )(mem2d, g_src.reshape(1, -1), w_kv.astype(BF16), gk, e)


def _memattn_body(x_ref, g_ref, wq_ref, gq_ref, e_ref, k_ref, v_ref, wo_ref, o_ref, att_ref):
    x = x_ref[...]
    h = _rms(x, g_ref[...])
    q = _dot(h.astype(BF16), wq_ref[...])
    ms = _dot_f32(q * q, e_ref[...]) * (1.0 / MEM_HD)
    q = (q * lax.rsqrt(ms + EPS) * gq_ref[...]).astype(BF16)
    k = k_ref[0].astype(BF16)
    v = v_ref[0].astype(BF16)
    for hd in range(MEM_HEADS):
        sl = slice(hd * MEM_HD, (hd + 1) * MEM_HD)
        s = _dot_nt(q[:, sl], k[:, sl]) * (MEM_HD ** -0.5)
        p = jnp.exp(s - jnp.max(s, axis=-1, keepdims=True))
        p = p / jnp.sum(p, axis=-1, keepdims=True)
        att_ref[:, sl] = _dot(p.astype(BF16), v[:, sl])
    o_ref[...] = x + _dot(att_ref[...].astype(BF16), wo_ref[...])


def memory_attend(x, seq_t, g_norm, w_q, g_q, k, v, w_o, tq):
    n = x.shape[0]
    per_b = seq_t // tq
    e = jnp.asarray(_blockdiag_ones(MEM_W, MEM_HD), BF16)
    gq = jnp.tile(g_q, MEM_HEADS).reshape(1, MEM_W)
    return pl.pallas_call(
        _memattn_body,
        grid=(n // tq,),
        in_specs=[pl.BlockSpec((tq, D_MODEL), lambda i: (i, 0)),
                  pl.BlockSpec((1, D_MODEL), lambda i: (0, 0)),
                  pl.BlockSpec((D_MODEL, MEM_W), lambda i: (0, 0)),
                  pl.BlockSpec((1, MEM_W), lambda i: (0, 0)),
                  pl.BlockSpec((MEM_W, MEM_W), lambda i: (0, 0)),
                  pl.BlockSpec((1, N_MEM, MEM_W), lambda i: (i // per_b, 0, 0)),
                  pl.BlockSpec((1, N_MEM, MEM_W), lambda i: (i // per_b, 0, 0)),
                  pl.BlockSpec((MEM_W, D_MODEL), lambda i: (0, 0))],
        out_specs=pl.BlockSpec((tq, D_MODEL), lambda i: (i, 0)),
        out_shape=jax.ShapeDtypeStruct((n, D_MODEL), F32),
        scratch_shapes=[pltpu.VMEM((tq, MEM_W), F32)],
        compiler_params=_cp("parallel"),
        name="memory_attend",
    )(x, g_norm.reshape(1, -1), w_q.astype(BF16), gq, e, k, v, w_o.astype(BF16))


def _topk_rows(s, k, payload=None, order=None):
    rows, cols = s.shape
    ridx = lax.broadcasted_iota(I32, (rows, cols), 0) if order is None else order
    orow = lax.broadcasted_iota(I32, (k, cols), 0)

    def body(r, carry):
        s, acc_v, acc_i = carry
        m = jnp.max(s, axis=0, keepdims=True)
        first = jnp.min(jnp.where(s == m, ridx, jnp.iinfo(jnp.int32).max), axis=0, keepdims=True)
        hit = ridx == first
        if payload is None:
            pick = first.astype(F32)
        else:
            pick = jnp.max(jnp.where(hit, payload, -1.0), axis=0, keepdims=True)
        here = orow == r
        return jnp.where(hit, -jnp.inf, s), jnp.where(here, m, acc_v), jnp.where(here, pick, acc_i)

    z = jnp.zeros((k, cols), F32)
    _, acc_v, acc_i = lax.fori_loop(0, k, body, (s, z, z))
    return acc_v, acc_i


def _route_body(x_ref, g_ref, wq_ref, keys_ref, hn_ref, idx_ref, gate_ref):
    h = _rms(x_ref[...], g_ref[...])
    hn_ref[...] = h
    q = _dot(h.astype(BF16), wq_ref[...])
    tops = []
    for c in range(2):
        s = _dot_nt(keys_ref[0, c], q[:, c * LANES:(c + 1) * LANES].astype(BF16))
        tops.append(_topk_rows(s, PEER_TOPK))
    (s0, i0), (s1, i1) = tops
    tm = s0.shape[1]
    ss, ii, ff = [], [], []

    def rows_a(a, nb):
        ss.append(s0[a:a + 1, :] + s1[:nb])
        ii.append(i0[a:a + 1, :] * float(N_KEYS) + i1[:nb])
        ff.append(a * PEER_TOPK + lax.broadcasted_iota(I32, (nb, tm), 0))

    def rows_b(b, na, a_min):
        a_idx = lax.broadcasted_iota(I32, (na, tm), 0)
        ss.append(jnp.where(a_idx >= a_min, s0[:na] + s1[b:b + 1, :], -jnp.inf))
        ii.append(i0[:na] * float(N_KEYS) + i1[b:b + 1, :])
        ff.append(a_idx * PEER_TOPK + b)

    half = PEER_TOPK // 2
    plan_a = [(0, PEER_TOPK)] + [(a, half) for a in range(1, 4)]
    plan_b = [(0, PEER_TOPK, 4)] + [(b, half, 4) for b in range(1, 3)]
    covered = [(a, b) for a, nb in plan_a for b in range(nb)] + [(a, b) for b, na, lo in plan_b for a in range(lo, na)]
    needed = {(a, b) for a in range(PEER_TOPK) for b in range(PEER_TOPK) if (a + 1) * (b + 1) <= PEER_TOPK}
    assert len(set(covered)) == len(covered) and needed <= set(covered)
    for a, nb in plan_a:
        rows_a(a, nb)
    for b, na, lo in plan_b:
        rows_b(b, na, lo)
    cand_s, cand_i, cand_f = (jnp.concatenate(p, axis=0) for p in (ss, ii, ff))
    best_s, best_i = _topk_rows(cand_s, PEER_TOPK, payload=cand_i, order=cand_f)
    e = jnp.exp(best_s - best_s[0:1, :])
    idx_ref[0] = best_i.astype(I32)
    gate_ref[0] = e / jnp.sum(e, axis=0, keepdims=True)


def peer_route(x, g, w_q, sub_keys, tm=128):
    n = x.shape[0]
    keys = sub_keys.astype(BF16)
    hn, idx, gate = pl.pallas_call(
        _route_body,
        grid=(n // tm, PEER_HEADS),
        in_specs=[pl.BlockSpec((tm, D_MODEL), lambda i, hd: (i, 0)),
                  pl.BlockSpec((1, D_MODEL), lambda i, hd: (0, 0)),
                  pl.BlockSpec((D_MODEL, PEER_DQ), lambda i, hd: (0, hd)),
                  pl.BlockSpec((1, 2, N_KEYS, PEER_DQ // 2), lambda i, hd: (hd, 0, 0, 0))],
        out_specs=[pl.BlockSpec((tm, D_MODEL), lambda i, hd: (i, 0)),
                   pl.BlockSpec((1, PEER_TOPK, tm), lambda i, hd: (hd, 0, i)),
                   pl.BlockSpec((1, PEER_TOPK, tm), lambda i, hd: (hd, 0, i))],
        out_shape=[jax.ShapeDtypeStruct((n, D_MODEL), F32),
                   jax.ShapeDtypeStruct((PEER_HEADS, PEER_TOPK, n), I32),
                   jax.ShapeDtypeStruct((PEER_HEADS, PEER_TOPK, n), F32)],
        compiler_params=_cp("parallel", "arbitrary"),
        name="peer_route",
    )(x, g.reshape(1, -1), w_q.astype(BF16), keys)
    idx = jnp.transpose(idx, (2, 0, 1)).reshape(n, PEER_HEADS * PEER_TOPK)
    gate = jnp.transpose(gate, (2, 0, 1)).reshape(n, PEER_HEADS * PEER_TOPK)
    return hn, idx, gate


HG = LANES
QW = H_A * HG
EVEN_CKV_OFF = D_CQ
EVEN_KR_OFF = D_CQ + D_C
EVEN_PR_OFF = EVEN_KR_OFF + HG
EVEN_W = EVEN_PR_OFF + RW_WIDTH


def _mla_prep_body(p_ref, cs_ref, sn_ref, gcq_ref, wuq_ref, gq_ref, gckv_ref, gkr_ref, wuk_ref, gk_ref, wuv_ref,
                   eq_ref, ek_ref, perm_ref, q_out, k_out, v_out, ckv_out, kr_out):
    cs, sn = cs_ref[...], sn_ref[...]

    def rope(x):
        return x * cs + _dot_sel(x, perm_ref[...]) * sn

    q = _dot(_rms(p_ref[:, :D_CQ], gcq_ref[...]).astype(BF16), wuq_ref[...])
    ckv = _rms(p_ref[:, EVEN_CKV_OFF:EVEN_CKV_OFF + D_C], gckv_ref[...])
    ckv_out[...] = ckv
    krr = p_ref[:, EVEN_KR_OFF:EVEN_KR_OFF + HG]
    ms = jnp.sum(krr * krr, axis=-1, keepdims=True) * (1.0 / DR_A)
    kr = rope(krr * lax.rsqrt(ms + EPS) * gkr_ref[...])
    kr_out[...] = kr
    ckv_b = ckv.astype(BF16)
    kraw = _dot(ckv_b, wuk_ref[...])
    v_out[...] = _dot(ckv_b, wuv_ref[...]).astype(BF16)
    for h in range(H_A):
        sl = slice(h * HG, (h + 1) * HG)
        qh = q[:, sl]
        qh = qh * lax.rsqrt(_dot_f32(qh * qh, eq_ref[...]) + EPS) * gq_ref[:, sl]
        q_out[:, sl] = rope(qh).astype(BF16)
        kh = kraw[:, sl]
        kh = kh * lax.rsqrt(_dot_f32(kh * kh, ek_ref[...]) + EPS) * gk_ref[:, sl]
        k_out[:, sl] = (kh + kr).astype(BF16)


def _head_groups(w, width, off=0):
    k = w.shape[0]
    out = jnp.zeros((k, H_A, HG), w.dtype)
    out = out.at[:, :, off:off + width].set(w.reshape(k, H_A, width))
    return out.reshape(k, H_A * HG)


def mla_prep(proj, pos, P, tm=256):
    n = proj.shape[0]
    half = DR_A // 2
    freqs = ROPE_BASE ** (-jnp.arange(half, dtype=F32) / half)
    ang = pos.astype(F32)[:, None] * freqs[None, :]
    cos, sin = jnp.cos(ang), jnp.sin(ang)
    one, zero = jnp.ones((n, DN_A), F32), jnp.zeros((n, DN_A), F32)
    pad = jnp.zeros((n, HG - DN_A - DR_A), F32)
    cs = jnp.concatenate([one, cos, cos, pad], axis=1)
    sn = jnp.concatenate([zero, -sin, sin, pad], axis=1)
    wuq = P['w_uq'].reshape(D_CQ, H_A, DN_A + DR_A)
    wuq = jnp.pad(wuq, ((0, 0), (0, 0), (0, HG - DN_A - DR_A))).reshape(D_CQ, QW).astype(BF16)
    gq = jnp.tile(jnp.concatenate([P['g_qn'], P['g_qr'], jnp.zeros((HG - DN_A - DR_A,), F32)]), H_A).reshape(1, QW)
    gkr = jnp.zeros((1, HG), F32).at[0, DN_A:DN_A + DR_A].set(P['g_kr'])
    wuk = _head_groups(P['w_uk'], DN_A).astype(BF16)
    wuv = _head_groups(P['w_uv'], DV_A).astype(BF16)
    gk = _head_groups(jnp.tile(P['g_kn'], H_A).reshape(1, -1), DN_A)
    eq = np.zeros((HG, HG), np.float32)
    eq[:DN_A, :DN_A] = 1.0 / DN_A
    eq[DN_A:DN_A + DR_A, DN_A:DN_A + DR_A] = 1.0 / DR_A
    ek = np.zeros((HG, HG), np.float32)
    ek[:DN_A, :DN_A] = 1.0 / DN_A
    perm = np.zeros((HG, HG), np.float32)
    j = np.arange(half)
    perm[DN_A + half + j, DN_A + j] = 1.0
    perm[DN_A + j, DN_A + half + j] = 1.0
    row = lambda a: pl.BlockSpec(a.shape, lambda i: (0, 0))
    consts = [P['g_cq'].reshape(1, -1), wuq, gq, P['g_ckv'].reshape(1, -1), gkr, wuk, gk, wuv,
              jnp.asarray(eq, BF16), jnp.asarray(ek, BF16), jnp.asarray(perm, BF16)]
    return pl.pallas_call(
        _mla_prep_body,
        grid=(n // tm,),
        in_specs=[pl.BlockSpec((tm, EVEN_PR_OFF), lambda i: (i, 0)),
                  pl.BlockSpec((tm, HG), lambda i: (i, 0)),
                  pl.BlockSpec((tm, HG), lambda i: (i, 0))] + [row(c) for c in consts],
        out_specs=[pl.BlockSpec((tm, QW), lambda i: (i, 0))] * 3
        + [pl.BlockSpec((tm, D_C), lambda i: (i, 0)), pl.BlockSpec((tm, HG), lambda i: (i, 0))],
        out_shape=[jax.ShapeDtypeStruct((n, QW), BF16)] * 3
        + [jax.ShapeDtypeStruct((n, D_C), F32), jax.ShapeDtypeStruct((n, HG), F32)],
        compiler_params=_cp("parallel"),
        name="mla_prep",
    )(proj, cs, sn, *consts)


def _mla_prompt_body(q_ref, k_ref, v_ref, o_ref, *, tq):
    qi = pl.program_id(1)
    row = lax.broadcasted_iota(I32, (tq, tq), 0)
    col = lax.broadcasted_iota(I32, (tq, tq), 1)
    for h in range(H_A):
        sl = slice(h * HG, (h + 1) * HG)
        q = q_ref[:, sl]

        def body(kb, carry):
            m, l, acc = carry
            off = pl.multiple_of(kb * tq, tq)
            s = _dot_nt(q, k_ref[pl.ds(off, tq), sl]) * MLA_SCALE
            s = jnp.where(col + kb * tq <= row + qi * tq, s, -jnp.inf)
            m_new = jnp.maximum(m, jnp.max(s, axis=-1, keepdims=True))
            alpha = jnp.exp(m - m_new)
            p = jnp.exp(s - m_new)
            l = l * alpha + jnp.sum(p, axis=-1, keepdims=True)
            acc = acc * alpha + _dot(p.astype(BF16), v_ref[pl.ds(off, tq), sl])
            return m_new, l, acc

        init = (jnp.full((tq, 1), -jnp.inf, F32), jnp.zeros((tq, 1), F32), jnp.zeros((tq, HG), F32))
        _, l, acc = lax.fori_loop(0, qi + 1, body, init)
        o_ref[:, h * DV_A:(h + 1) * DV_A] = (acc / l)[:, :DV_A].astype(BF16)


def mla_prompt_attention(qc, kc, vc, b, t, tq=256):
    nq = t // tq
    return pl.pallas_call(
        functools.partial(_mla_prompt_body, tq=tq),
        grid=(b, nq),
        in_specs=[pl.BlockSpec((tq, QW), lambda bi, qi: (bi * nq + qi, 0)),
                  pl.BlockSpec((t, QW), lambda bi, qi: (bi, 0)),
                  pl.BlockSpec((t, QW), lambda bi, qi: (bi, 0))],
        out_specs=pl.BlockSpec((tq, H_A * DV_A), lambda bi, qi: (bi * nq + qi, 0)),
        out_shape=jax.ShapeDtypeStruct((b * t, H_A * DV_A), BF16),
        compiler_params=_cp("parallel", "arbitrary"),
        name="mla_prompt_attention",
    )(qc, kc, vc)


PAGES_PER_STEP = 16
KV_STEPS = N_PAGES // PAGES_PER_STEP


def _mla_sample_body(pt_ref, *refs, t_new):
    npg = PAGES_PER_STEP
    ckv_pages, kr_pages = refs[:npg], refs[npg:2 * npg]
    (ckv_new, kr_new, qn_ref, qr_ref, wuk_ref, e_ref, wuv_ref, hm_ref, o_ref,
     m_scr, l_scr, acc_scr) = refs[2 * npg:]
    j = pl.program_id(1)

    def process(ckv, kr, mask):
        ckv_b = ckv.astype(BF16)
        kraw = _dot(ckv_b, wuk_ref[...])
        ms = _dot_f32(kraw * kraw, e_ref[...]) * (1.0 / DN_A)
        s = (_dot(kraw.astype(BF16), qn_ref[0]) * lax.rsqrt(ms + EPS)
             + _dot_tn(kr.astype(BF16), qr_ref[0])) * MLA_SCALE
        if mask is not None:
            s = jnp.where(mask, s, -jnp.inf)
        m = m_scr[...]
        m_new = jnp.maximum(m, jnp.max(s, axis=0, keepdims=True))
        alpha = jnp.exp(m - m_new)
        p = jnp.exp(s - m_new)
        l_scr[...] = l_scr[...] * alpha + jnp.sum(p, axis=0, keepdims=True)
        acc_scr[...] = acc_scr[...] * alpha + _dot_tn(ckv_b, p.astype(BF16))
        m_scr[...] = m_new

    @pl.when(j == 0)
    def _():
        m_scr[...] = jnp.full(m_scr.shape, -jnp.inf, F32)
        l_scr[...] = jnp.zeros(l_scr.shape, F32)
        acc_scr[...] = jnp.zeros(acc_scr.shape, F32)
        nk, nq = PAGE_SIZE, H_A * t_new
        key = lax.broadcasted_iota(I32, (nk, nq), 0)
        qt = lax.broadcasted_iota(I32, (nk, nq), 1) % t_new
        process(ckv_new[0], kr_new[0], key <= qt)

    process(jnp.concatenate([r[0] for r in ckv_pages], axis=0), jnp.concatenate([r[0] for r in kr_pages], axis=1), None)

    @pl.when(j == KV_STEPS - 1)
    def _():
        o_lat =(acc_scr[...] / l_scr[...]).astype(BF16)
        full = _dot_tn(o_lat, wuv_ref[...]) * hm_ref[...]
        out = full[0:t_new]
        for h in range(1, H_A):
            out = out + full[h * t_new:(h + 1) * t_new]
        o_ref[0] = out.astype(BF16)


def mla_sample_attention(page_table, cache_ckv, cache_kr, ckv_new, kr_new, qn, qr, P):
    b, t_new = qn.shape[:2]
    nq = H_A * t_new
    eye = jnp.eye(H_A, dtype=BF16)
    qn_g = (qn.astype(F32) * P['g_kn']).astype(BF16)
    qn_bd = jnp.einsum('bthd,hg->bhdgt', qn_g, eye).reshape(b, H_A * DN_A, nq)
    qr_m = jnp.transpose(qr, (0, 3, 2, 1)).reshape(b, DR_A, nq)
    pad = lambda a: jnp.pad(a, ((0, 0), (0, PAGE_SIZE - t_new), (0, 0)))
    keys_last = lambda a: jnp.swapaxes(a, 1, 2)
    cache_kr = keys_last(cache_kr)
    hm = (np.arange(nq)[:, None] // t_new == np.arange(H_A * DV_A)[None, :] // DV_A).astype(np.float32)
    e_head = (np.arange(H_A * DN_A)[:, None] // DN_A == np.arange(nq)[None, :] // t_new).astype(np.float32)
    npg = PAGES_PER_STEP
    page = lambda p, s: pl.BlockSpec((1,) + s, lambda bi, j, pt: (pt[bi * N_PAGES + j * npg + p], 0, 0))
    per_b = lambda s: pl.BlockSpec((1,) + s, lambda bi, j, pt: (bi, 0, 0))
    const = lambda a: pl.BlockSpec(a.shape, lambda bi, j, pt: (0, 0))
    consts = [P['w_uk'].astype(BF16), jnp.asarray(e_head, BF16), P['w_uv'].astype(BF16), jnp.asarray(hm)]
    grid_spec = pltpu.PrefetchScalarGridSpec(
        num_scalar_prefetch=1,
        grid=(b, KV_STEPS),
        in_specs=[page(p, (PAGE_SIZE, D_C)) for p in range(npg)] + [page(p, (DR_A, PAGE_SIZE)) for p in range(npg)]
        + [per_b((PAGE_SIZE, D_C)), per_b((DR_A, PAGE_SIZE)), per_b((H_A * DN_A, nq)), per_b((DR_A, nq))]
        + [const(c) for c in consts],
        out_specs=pl.BlockSpec((1, t_new, H_A * DV_A), lambda bi, j, pt: (bi, 0, 0)),
        scratch_shapes=[pltpu.VMEM((1, nq), F32), pltpu.VMEM((1, nq), F32), pltpu.VMEM((D_C, nq), F32)],
    )
    out = pl.pallas_call(
        functools.partial(_mla_sample_body, t_new=t_new),
        grid_spec=grid_spec,
        out_shape=jax.ShapeDtypeStruct((b, t_new, H_A * DV_A), BF16),
        compiler_params=_cp("parallel", "arbitrary"),
        name="mla_sample_attention",
    )(page_table.reshape(-1), *([cache_ckv] * npg), *([cache_kr] * npg), pad(ckv_new), keys_last(pad(kr_new)),
      qn_bd, qr_m, *consts)
    return out.reshape(b * t_new, H_A * DV_A)


RW_LORA_OFF = 3 * HB


def _rwkv_prep_body(pr_ref, sh_ref, mu_ref, w0_ref, wup_ref, a0_ref, aup_ref, gup_ref, kk_ref, ka_ref, rk_ref, e_ref,
                    nkk_o, wr_o, w_o, kka_o, k2_o, v_o, c1_o, c2_o, g_o, bonus_o):
    pr = pr_ref[...]
    xs = pr + mu_ref[...] * (sh_ref[...] - pr)
    r, k, v = xs[:, :HB], xs[:, HB:2 * HB], xs[:, 2 * HB:3 * HB]
    xwa = xs[:, RW_LORA_OFF:RW_LORA_OFF + D_WL + D_AL]
    xg = xs[:, RW_LORA_OFF + D_WL + D_AL:]
    w = jnp.exp(-RWKV_DECAY_SCALE * _sigmoid(w0_ref[...] + _dot(jnp.tanh(xwa).astype(BF16), wup_ref[...])))
    a = _sigmoid(a0_ref[...] + _dot(xwa.astype(BF16), aup_ref[...]))
    g_o[...] = _dot(_sigmoid(xg).astype(BF16), gup_ref[...])
    kk = k * kk_ref[...]
    kk = kk * lax.rsqrt(jnp.maximum(_dot_f32(kk * kk, e_ref[...]), 1e-12))
    k2 = k * (1.0 + (a - 1.0) * ka_ref[...])
    kka = kk * a
    nkk_o[...] = -kk
    wr_o[...] = w * r
    w_o[...] = w
    kka_o[...] = kka
    k2_o[...] = k2
    v_o[...] = v
    c1_o[...] = _dot_f32(kka * r, e_ref[...])
    c2_o[...] = _dot_f32(k2 * r, e_ref[...])
    bonus_o[...] = _dot_f32(r * k2 * rk_ref[...], e_ref[...]) * v


def rwkv_prep(proj, shifted, P, tm=256):
    n = proj.shape[0]
    nblk = EVEN_PR_OFF // RW_WIDTH
    assert EVEN_PR_OFF % LANES == 0
    zw = jnp.zeros((D_WL, HB), F32)
    consts = [P['mu'].reshape(1, -1), P['w0'].reshape(1, -1),
              jnp.concatenate([P['w_up'], zw]).astype(BF16), P['a0'].reshape(1, -1),
              jnp.concatenate([zw, P['a_up']]).astype(BF16), P['g_up'].astype(BF16),
              P['k_k'].reshape(1, -1), P['k_a'].reshape(1, -1), P['r_k'].reshape(1, -1),
              jnp.asarray(_blockdiag_ones(HB, HS_B), BF16)]
    pr = lax.slice_in_dim(proj, EVEN_PR_OFF, EVEN_W, axis=1)
    return pl.pallas_call(
        _rwkv_prep_body,
        grid=(n // tm,),
        in_specs=[pl.BlockSpec((tm, RW_WIDTH), lambda i: (i, 0)), pl.BlockSpec((tm, RW_WIDTH), lambda i: (i, 0))]
        + [pl.BlockSpec(c.shape, lambda i: (0, 0)) for c in consts],
        out_specs=[pl.BlockSpec((tm, HB), lambda i: (i, 0))] * 10,
        out_shape=[jax.ShapeDtypeStruct((n, HB), F32)] * 10,
        compiler_params=_cp("parallel"),
        name="rwkv_prep",
    )(pr, shifted, *consts)


def _rwkv_scan_body(nkk_r, wr_r, w_r, kka_r, k2_r, v_r, c1_r, c2_r, s0_ref, e_ref, d_ref, o_ref, sf_ref, s_scr,
                    *, nb, tc):
    c = pl.program_id(1)

    @pl.when(c == 0)
    def _():
        s_scr[...] = s0_ref[...]

    dmask = d_ref[...]
    half = HB // 2

    def step(t, carry):
        for b in range(nb):
            row = lambda ref: ref[b, pl.ds(t, 1), :]
            s = s_scr[b]
            stacked = jnp.concatenate([s * row(nkk_r), s * row(wr_r), dmask * row(v_r)], axis=0)
            hi, lo = _split2(stacked)
            seg = jnp.concatenate(
                [_dot(hi[:, i * half:(i + 1) * half], e_ref[...]) + _dot(lo[:, i * half:(i + 1) * half], e_ref[...])
                 for i in range(2)], axis=1)
            sa, t2, vb = seg[:HS_B], seg[HS_B:2 * HS_B], seg[2 * HS_B:]
            s_scr[b] = s * row(w_r) + sa * row(kka_r) + vb * row(k2_r)
            ob = t2 + sa * row(c1_r) + vb * row(c2_r)
            o_ref[b, pl.ds(t, 1), :] = jnp.sum(ob * dmask, axis=0, keepdims=True)
        return carry

    lax.fori_loop(0, tc, step, 0)

    @pl.when(c == pl.num_programs(1) - 1)
    def _():
        sf_ref[...] = s_scr[...]


def rwkv_scan(seqs, state0, b, t, nb, tc):
    s0 = jnp.transpose(state0, (0, 2, 1, 3)).reshape(b, HS_B, HB)
    e = jnp.asarray(_blockdiag_ones(HB // 2, HS_B), BF16)
    dmask = jnp.asarray((np.arange(HS_B)[:, None] == (np.arange(HB)[None, :] % HS_B)).astype(np.float32))
    seq_spec = pl.BlockSpec((nb, tc, HB), lambda i, c: (i, c, 0))
    st_spec = pl.BlockSpec((nb, HS_B, HB), lambda i, c: (i, 0, 0))
    o, sf = pl.pallas_call(
        functools.partial(_rwkv_scan_body, nb=nb, tc=tc),
        grid=(b // nb, t // tc),
        in_specs=[seq_spec] * 8 + [st_spec, pl.BlockSpec(e.shape, lambda i, c: (0, 0)),
                                   pl.BlockSpec(dmask.shape, lambda i, c: (0, 0))],
        out_specs=[seq_spec, st_spec],
        out_shape=[jax.ShapeDtypeStruct((b, t, HB), F32), jax.ShapeDtypeStruct((b, HS_B, HB), F32)],
        scratch_shapes=[pltpu.VMEM((nb, HS_B, HB), F32)],
        compiler_params=_cp("parallel", "arbitrary"),
        name="rwkv_scan",
    )(*[a.reshape(b, t, HB) for a in seqs], s0, e, dmask)
    sf = jnp.transpose(sf.reshape(b, HS_B, H_B, HS_B), (0, 2, 1, 3))
    return o.reshape(b * t, HB), sf


def _rwkv_post_body(o_ref, g_ref, bonus_ref, lng_ref, lnb_ref, e_ref, out_ref):
    o = o_ref[...]
    mu = _dot_f32(o, e_ref[...]) * (1.0 / HS_B)
    d = o - mu
    var = _dot_f32(d * d, e_ref[...]) * (1.0 / HS_B)
    y = d * lax.rsqrt(var + RWKV_LN_EPS) * lng_ref[...] + lnb_ref[...]
    out_ref[...] = ((y + bonus_ref[...]) * g_ref[...]).astype(BF16)


def rwkv_post(o, g, bonus, P, tm=256):
    n = o.shape[0]
    e = jnp.asarray(_blockdiag_ones(HB, HS_B), BF16)
    blk = pl.BlockSpec((tm, HB), lambda i: (i, 0))
    row = pl.BlockSpec((1, HB), lambda i: (0, 0))
    return pl.pallas_call(
        _rwkv_post_body,
        grid=(n // tm,),
        in_specs=[blk, blk, blk, row, row, pl.BlockSpec(e.shape, lambda i: (0, 0))],
        out_specs=blk,
        out_shape=jax.ShapeDtypeStruct((n, HB), BF16),
        compiler_params=_cp("parallel"),
        name="rwkv_post",
    )(o, g, bonus, P['ln_g'].reshape(1, -1), P['ln_b'].reshape(1, -1), e)


def rwkv_mix(proj, prev, state0, b, t, P, nb, tc, tm):
    pr = lax.slice_in_dim(proj, EVEN_PR_OFF, EVEN_W, axis=1).reshape(b, t, RW_WIDTH)
    shifted = jnp.concatenate([prev[:, None, :], pr[:, :-1, :]], axis=1).reshape(b * t, RW_WIDTH)
    outs = rwkv_prep(proj, shifted, P, tm)
    o, s_new = rwkv_scan(outs[:8], state0, b, t, nb, tc)
    return rwkv_post(o, outs[8], outs[9], P, tm), pr[:, -1, :], s_new


ODD_Z_OFF = 3 * HC
ODD_XBC_OFF = ODD_Z_OFF + D_INNER
ODD_DT_OFF = ODD_XBC_OFF + CONV_DIM
ODD_W = ODD_DT_OFF + LANES
SB_SCALE = HD_C ** -0.5


def _sb_weights(z, mask, suffix_of, run):
    lneg = -_softplus(z)
    if mask is not None:
        lneg = jnp.where(mask, lneg, 0.0)
    w = jnp.exp(z + lneg + suffix_of(lneg) + run)
    if mask is not None:
        w = jnp.where(mask, w, 0.0)
    return w, lneg


def _sb_prompt_body(q_ref, k_ref, v_ref, u_ref, o_ref, *, tq):
    qi = pl.program_id(1)
    row = lax.broadcasted_iota(I32, (tq, tq), 0)
    col = lax.broadcasted_iota(I32, (tq, tq), 1)
    for h in range(H_C):
        sl = slice(h * HD_C, (h + 1) * HD_C)
        q = q_ref[:, sl].astype(BF16)

        def body(i, carry):
            run, acc = carry
            kb = qi - i
            off = pl.multiple_of(kb * tq, tq)
            z = _dot_nt(q, k_ref[pl.ds(off, tq), sl].astype(BF16)) * SB_SCALE
            mask = col + kb * tq < row + qi * tq
            w, lneg = _sb_weights(z, mask, lambda l: _dot_f32(l, u_ref[...]), run)
            acc = acc + _dot(w.astype(BF16), v_ref[pl.ds(off, tq), sl].astype(BF16))
            return run + jnp.sum(lneg, axis=-1, keepdims=True), acc

        _, acc = lax.fori_loop(0, qi + 1, body, (jnp.zeros((tq, 1), F32), jnp.zeros((tq, HD_C), F32)))
        o_ref[:, sl] = acc.astype(BF16)


def sb_prompt_attention(proj, b, t, tq=256):
    nq = t // tq
    u = jnp.asarray(np.tril(np.ones((tq, tq), np.float32), -1), BF16)
    return pl.pallas_call(
        functools.partial(_sb_prompt_body, tq=tq),
        grid=(b, nq),
        in_specs=[pl.BlockSpec((tq, HC), lambda bi, qi: (bi * nq + qi, 0)),
                  pl.BlockSpec((t, HC), lambda bi, qi: (bi, 1)),
                  pl.BlockSpec((t, HC), lambda bi, qi: (bi, 2)),
                  pl.BlockSpec(u.shape, lambda bi, qi: (0, 0))],
        out_specs=pl.BlockSpec((tq, HC), lambda bi, qi: (bi * nq + qi, 0)),
        out_shape=jax.ShapeDtypeStruct((b * t, HC), BF16),
        compiler_params=_cp("parallel", "arbitrary"),
        name="sb_prompt_attention",
    )(proj, proj, proj, u)


def _kv_heads_body(k_ref, v_ref, eye_ref, kt_ref, vt_ref):
    for src, dst in ((k_ref, kt_ref), (v_ref, vt_ref)):
        x = src[...]
        hi = x.astype(BF16)
        r1 = x - hi.astype(F32)
        mid = r1.astype(BF16)
        lo = (r1 - mid.astype(F32)).astype(BF16)
        for h in range(H_C):
            sl = slice(h * HD_C, (h + 1) * HD_C)
            eye = eye_ref[...]
            dst[0, h] = (_dot_nt(eye, hi[:, sl]) + _dot_nt(eye, mid[:, sl])) + _dot_nt(eye, lo[:, sl])


def sb_kv_state(proj, b, t, tq=512):
    tq = min(tq, t)
    assert t % tq == 0
    nq = t // tq
    eye = jnp.eye(HD_C, dtype=BF16)
    out = pl.BlockSpec((1, H_C, HD_C, tq), lambda bi, qi: (bi, 0, 0, qi))
    kt, vt = pl.pallas_call(
        _kv_heads_body,
        grid=(b, nq),
        in_specs=[pl.BlockSpec((tq, HC), lambda bi, qi: (bi * nq + qi, 1)),
                  pl.BlockSpec((tq, HC), lambda bi, qi: (bi * nq + qi, 2)),
                  pl.BlockSpec(eye.shape, lambda bi, qi: (0, 0))],
        out_specs=[out, out],
        out_shape=[jax.ShapeDtypeStruct((b, H_C, HD_C, t), F32)] * 2,
        compiler_params=_cp("parallel", "parallel"),
        name="sb_kv_state",
    )(proj, proj, eye)
    return jnp.transpose(kt, (0, 3, 1, 2)), jnp.transpose(vt, (0, 3, 1, 2))


def _sb_sample_body(pt_ref, *refs, t_new):
    npg = PAGES_PER_STEP
    k_pages, v_pages = refs[:npg], refs[npg:2 * npg]
    k_new, v_new, q_ref, u_ref, o_ref, q_scr, run_scr, acc_scr = refs[2 * npg:]
    j = pl.program_id(1)
    nq = H_C * t_new
    row_head = lax.broadcasted_iota(I32, (nq, PAGE_SIZE), 0) // t_new
    hsl = lambda h: slice(h * HD_C, (h + 1) * HD_C)

    def process(k_of, v_of, mask, dot_k, dot_v):
        z = dot_k(q_scr[0], k_of(0))
        for h in range(1, H_C):
            z = z + dot_k(q_scr[h], k_of(h))
        w, lneg = _sb_weights(z * SB_SCALE, mask, lambda l: _dot_f32(l, u_ref[...]), run_scr[...])
        acc = acc_scr[...]
        for h in range(H_C):
            acc = acc + dot_v(jnp.where(row_head == h, w, 0.0).astype(BF16), v_of(h))
        acc_scr[...] = acc
        run_scr[...] += jnp.sum(lneg, axis=1, keepdims=True)

    @pl.when(j == 0)
    def _():
        run_scr[...] = jnp.zeros(run_scr.shape, F32)
        acc_scr[...] = jnp.zeros(acc_scr.shape, F32)
        q = q_ref[0]
        for h in range(H_C):
            rows = [jnp.zeros((t_new, HD_C), F32)] * H_C
            rows[h] = q[:, hsl(h)]
            q_scr[h] = jnp.concatenate(rows, axis=0).astype(BF16)
        fill = jnp.zeros((PAGE_SIZE - t_new, HC), F32)
        k_pad = jnp.concatenate([k_new[0], fill], axis=0).astype(BF16)
        v_pad = jnp.concatenate([v_new[0], fill], axis=0).astype(BF16)
        key = lax.broadcasted_iota(I32, (nq, PAGE_SIZE), 1)
        qt = lax.broadcasted_iota(I32, (nq, PAGE_SIZE), 0) % t_new
        process(lambda h: k_pad[:, hsl(h)], lambda h: v_pad[:, hsl(h)], key < qt, _dot_nt, _dot)

    for p in range(npg):
        process(lambda h: k_pages[p][0, h].astype(BF16), lambda h: v_pages[p][0, h].astype(BF16), None, _dot, _dot_nt)

    @pl.when(j == KV_STEPS - 1)
    def _():
        acc = acc_scr[...]
        for h in range(H_C):
            o_ref[0, :, h * HD_C:(h + 1) * HD_C] = acc[h * t_new:(h + 1) * t_new].astype(BF16)


def sb_sample_attention(page_table, cache_k, cache_v, proj, b, t_new):
    proj = proj.reshape(b, t_new, ODD_W)
    nq = H_C * t_new
    keys_last = lambda a: jnp.transpose(a, (0, 2, 3, 1))
    cache_k, cache_v = keys_last(cache_k), keys_last(cache_v)
    u = jnp.asarray(np.tril(np.ones((PAGE_SIZE, PAGE_SIZE), np.float32), -1), BF16)
    npg = PAGES_PER_STEP
    page = lambda p: pl.BlockSpec(
        (1, H_C, HD_C, PAGE_SIZE), lambda bi, j, pt: (pt[bi * N_PAGES + (N_PAGES - 1 - (j * npg + p))], 0, 0, 0))
    col = lambda c: pl.BlockSpec((1, t_new, HC), lambda bi, j, pt: (bi, 0, c))
    grid_spec = pltpu.PrefetchScalarGridSpec(
        num_scalar_prefetch=1,
        grid=(b, KV_STEPS),
        in_specs=[page(p) for p in range(npg)] * 2
        + [col(1), col(2), col(0), pl.BlockSpec(u.shape, lambda bi, j, pt: (0, 0))],
        out_specs=col(0),
        scratch_shapes=[pltpu.VMEM((H_C, nq, HD_C), BF16), pltpu.VMEM((nq, 1), F32), pltpu.VMEM((nq, HD_C), F32)],
    )
    out = pl.pallas_call(
        functools.partial(_sb_sample_body, t_new=t_new),
        grid_spec=grid_spec,
        out_shape=jax.ShapeDtypeStruct((b, t_new, HC), BF16),
        compiler_params=_cp("parallel", "arbitrary"),
        name="sb_sample_attention",
    )(page_table.reshape(-1), *([cache_k] * npg), *([cache_v] * npg), proj, proj, proj, u)
    return out.reshape(b * t_new, HC)


TAIL = 8


def _ssd_body(z_ref, xbc_ref, dt_ref, cbuf_ref, h0_ref, cw_ref, cb_ref, dtb_ref, alog_ref, drow_ref, gn_ref, tri_ref,
              y_ref, hf_ref, h_scr, tail_scr, y_scr, *, L):
    c = pl.program_id(1)
    mm = (lambda x: x.astype(BF16)) if L >= 16 else (lambda x: x.astype(BF16).astype(F32))

    @pl.when(c == 0)
    def _():
        h_scr[...] = h0_ref[0]
        tail_scr[...] = jnp.zeros(tail_scr.shape, F32)
        tail_scr[TAIL - (CONV_W - 1):, :] = cbuf_ref[0]

    xbc = xbc_ref[...]
    ext = jnp.concatenate([tail_scr[...], xbc], axis=0)
    conv = cb_ref[...]
    for i in range(CONV_W):
        s = TAIL - (CONV_W - 1) + i
        conv = conv + cw_ref[i:i + 1, :] * ext[s:s + L]
    tail_scr[...] = xbc[L - TAIL:]
    xc = conv * _sigmoid(conv)
    dt = _softplus(dt_ref[...] + dtb_ref[...])
    a = -jnp.exp(alog_ref[...])
    lane = lax.broadcasted_iota(I32, (1, LANES), 1)
    dta = jnp.where(lane < H_D, dt * a, 0.0)
    hi = dta.astype(BF16)
    r1 = dta - hi.astype(F32)
    mid = r1.astype(BF16)
    lo = (r1 - mid.astype(F32)).astype(BF16)
    tri = tri_ref[...]
    acs = (_dot(tri, mm(hi)) + _dot(tri, mm(mid))) + _dot(tri, mm(lo))
    hi, mid, lo = acs.astype(BF16), None, None
    r1 = acs - hi.astype(F32)
    mid = r1.astype(BF16)
    lo = (r1 - mid.astype(F32)).astype(BF16)
    row = lax.broadcasted_iota(I32, (L, L), 0)
    col = lax.broadcasted_iota(I32, (L, L), 1)
    lane_l = lax.broadcasted_iota(I32, (L, LANES), 1)
    acs_last = acs[L - 1:L, :]
    cbs = []
    for g in range(G_D):
        bg = xc[:, D_INNER + g * N_D:D_INNER + (g + 1) * N_D]
        cg = xc[:, D_INNER + G_D * N_D + g * N_D:D_INNER + G_D * N_D + (g + 1) * N_D]
        cbs.append((bg, cg, _dot_nt(mm(cg), mm(bg))))
    for h in range(H_D):
        bg, cg, cb = cbs[h // (H_D // G_D)]
        sel = mm((lane_l == h).astype(F32))
        acs_row = (_dot_nt(sel, mm(hi)) + _dot_nt(sel, mm(mid))) + _dot_nt(sel, mm(lo))
        acs_col = acs[:, h:h + 1]
        lmat = jnp.exp(jnp.where(col <= row, acs_col - acs_row, -jnp.inf))
        xh = xc[:, h * P_D:(h + 1) * P_D]
        xdt = xh * dt[:, h:h + 1]
        hprev = h_scr[h]
        y = _dot(mm(cb * lmat), mm(xdt)) + _dot_nt(mm(cg * jnp.exp(acs_col)), mm(hprev))
        y_scr[:, h * P_D:(h + 1) * P_D] = y + drow_ref[:, h * P_D:(h + 1) * P_D] * xh
        last = acs_last[:, h:h + 1]
        bw = bg * jnp.exp(last - acs_col)
        if L >= 16:
            st = _dot_tn(mm(xdt), mm(bw))
        else:
            xr, br = xdt - mm(xdt), bw - mm(bw)
            st = (_dot_tn(mm(xdt), mm(bw)) + _dot_tn(mm(xdt), mm(br))) + _dot_tn(mm(xr), mm(bw))
        h_scr[h] = hprev * jnp.exp(last) + st
    z = z_ref[...]
    y_ref[...] = _rms(y_scr[...] * (z * _sigmoid(z)), gn_ref[...]).astype(BF16)

    @pl.when(c == pl.num_programs(1) - 1)
    def _():
        hf_ref[0] = h_scr[...]


def ssd_mix(proj, conv_buf, state0, b, t, P):
    L = SSD_CHUNK if t % SSD_CHUNK == 0 else t
    assert L % TAIL == 0
    nc = t // L
    pad_h = lambda v: jnp.pad(v.reshape(1, -1), ((0, 0), (0, LANES - H_D)))
    consts = [P['conv_w'], P['conv_b'].reshape(1, -1), pad_h(P['dt_bias']), pad_h(P['a_log']),
              jnp.repeat(P['d'], P_D).reshape(1, -1), P['g_norm'].reshape(1, -1)]
    tri_np = np.tril(np.ones((L, L), np.float32))
    tri = jnp.asarray(tri_np, BF16 if L >= 16 else F32)
    zc, xc, dc = ODD_Z_OFF // D_INNER, ODD_XBC_OFF // CONV_DIM, ODD_DT_OFF // LANES
    assert ODD_Z_OFF % D_INNER == 0 and ODD_XBC_OFF % CONV_DIM == 0 and ODD_DT_OFF % LANES == 0
    y, hf = pl.pallas_call(
        functools.partial(_ssd_body, L=L),
        grid=(b, nc),
        in_specs=[pl.BlockSpec((L, D_INNER), lambda bi, c: (bi * nc + c, zc)),
                  pl.BlockSpec((L, CONV_DIM), lambda bi, c: (bi * nc + c, xc)),
                  pl.BlockSpec((L, LANES), lambda bi, c: (bi * nc + c, dc)),
                  pl.BlockSpec((1, CONV_W - 1, CONV_DIM), lambda bi, c: (bi, 0, 0)),
                  pl.BlockSpec((1, H_D, P_D, N_D), lambda bi, c: (bi, 0, 0, 0))]
        + [pl.BlockSpec(cst.shape, lambda bi, c: (0, 0)) for cst in consts]
        + [pl.BlockSpec(tri.shape, lambda bi, c: (0, 0))],
        out_specs=[pl.BlockSpec((L, D_INNER), lambda bi, c: (bi * nc + c, 0)),
                   pl.BlockSpec((1, H_D, P_D, N_D), lambda bi, c: (bi, 0, 0, 0))],
        out_shape=[jax.ShapeDtypeStruct((b * t, D_INNER), BF16), jax.ShapeDtypeStruct((b, H_D, P_D, N_D), F32)],
        scratch_shapes=[pltpu.VMEM((H_D, P_D, N_D), F32), pltpu.VMEM((TAIL, CONV_DIM), F32),
                        pltpu.VMEM((L, D_INNER), F32)],
        compiler_params=_cp("parallel", "arbitrary"),
        name="ssd_mix",
    )(proj, proj, proj, conv_buf, state0, *consts, tri)
    return y, hf


SC_L = 16
SC_TB = 16
GELU_C = math.sqrt(2.0 / math.pi)


def _sc_gelu(a):
    y = GELU_C * (a + 0.044715 * (a * a * a))
    return 0.5 * a * (2.0 - 2.0 / (jnp.exp(2.0 * y) + 1.0))


def peer_experts(hn, idx, gate, u_tab, v_tab, x):
    n = hn.shape[0]
    info = plsc.get_sparse_core_info()
    nc, ns = info.num_cores, info.num_subcores
    assert info.num_lanes == SC_L and PEER_TOPK == SC_L
    nw = nc * ns
    per_w = n // nw
    assert n % (nw * SC_TB) == 0
    n_blk = per_w // SC_TB
    n_chunk = SC_TB * PEER_HEADS
    n_vec = D_MODEL // SC_L
    mesh = plsc.VectorSubcoreMesh(core_axis_name="c", subcore_axis_name="s")

    @functools.partial(
        pl.kernel, mesh=mesh, out_type=jax.ShapeDtypeStruct((n, D_MODEL), F32),
        scratch_types=[pltpu.VMEM((SC_TB, D_MODEL), F32), pltpu.VMEM((SC_TB, D_MODEL), F32),
                       pltpu.VMEM((SC_TB, PEER_HEADS, SC_L), I32), pltpu.VMEM((SC_TB, PEER_HEADS * SC_L), F32),
                       pltpu.VMEM((2, SC_L, D_MODEL), F32), pltpu.VMEM((2, SC_L, D_MODEL), F32),
                       pltpu.SemaphoreType.DMA((2,)), pltpu.SemaphoreType.DMA((2,))],
        compiler_params=pltpu.CompilerParams(needs_layout_passes=False),
        name="peer_experts")
    def run(hn_hbm, idx_hbm, gate_hbm, u_hbm, v_hbm, x_hbm, out_hbm, hnv, outv, idxv, gv, ubuf, vbuf, usem, vsem):
        wid = lax.axis_index("s") * nc + lax.axis_index("c")
        lanes = lax.iota(I32, SC_L)

        def copies(q, slot):
            tl, hd = q // PEER_HEADS, q % PEER_HEADS
            rows = idxv.at[tl, hd]
            return (pltpu.make_async_copy(u_hbm.at[rows], ubuf.at[slot], usem.at[slot]),
                    pltpu.make_async_copy(v_hbm.at[rows], vbuf.at[slot], vsem.at[slot]))

        def start(q, slot):
            for cp in copies(q, slot):
                cp.start()

        def compute(q, slot):
            tl, hd = q // PEER_HEADS, q % PEER_HEADS
            cu, cv = copies(q, slot)
            cu.wait()

            def ubody(c, accs):
                off = pl.multiple_of(c * SC_L, SC_L)
                xc = hnv[tl, pl.ds(off, SC_L)]
                return tuple(accs[j] + ubuf[slot, j, pl.ds(off, SC_L)] * xc for j in range(SC_L))

            zero = jnp.zeros((SC_L,), F32)
            accs = lax.fori_loop(0, n_vec, ubody, (zero,) * SC_L)
            acts = zero
            for j in range(SC_L):
                acts = jnp.where(lanes == j, jnp.sum(accs[j]), acts)
            coef = gv[tl, pl.ds(pl.multiple_of(hd * SC_L, SC_L), SC_L)] * _sc_gelu(acts)
            cj = [jnp.sum(jnp.where(lanes == j, coef, 0.0)) for j in range(SC_L)]
            cv.wait()

            def vbody(c, carry):
                off = pl.multiple_of(c * SC_L, SC_L)
                o = outv[tl, pl.ds(off, SC_L)]
                for j in range(SC_L):
                    o = o + cj[j] * vbuf[slot, j, pl.ds(off, SC_L)]
                outv[tl, pl.ds(off, SC_L)] = o
                return carry

            lax.fori_loop(0, n_vec, vbody, 0)

        def block(bi, carry):
            t0 = pl.multiple_of(wid * per_w + bi * SC_TB, SC_TB)
            pltpu.sync_copy(hn_hbm.at[pl.ds(t0, SC_TB)], hnv)
            pltpu.sync_copy(x_hbm.at[pl.ds(t0, SC_TB)], outv)
            pltpu.sync_copy(idx_hbm.at[pl.ds(t0, SC_TB)], idxv)
            pltpu.sync_copy(gate_hbm.at[pl.ds(t0, SC_TB)], gv)
            start(0, 0)

            def pair(p, c2):
                q = 2 * p
                start(q + 1, 1)
                compute(q, 0)

                @pl.when(q + 2 < n_chunk)
                def _():
                    start(q + 2, 0)

                compute(q + 1, 1)
                return c2

            lax.fori_loop(0, n_chunk // 2, pair, 0)
            pltpu.sync_copy(outv, out_hbm.at[pl.ds(t0, SC_TB)])
            return carry

        lax.fori_loop(0, n_blk, block, 0)

    return run(hn, idx.reshape(n, PEER_HEADS, SC_L), gate, u_tab, v_tab, x)


def _even_w_in(w):
    out = jnp.zeros((D_MODEL, EVEN_W), F32)
    out = out.at[:, :EVEN_KR_OFF].set(w[:, :D_CQ + D_C])
    out = out.at[:, EVEN_KR_OFF + DN_A:EVEN_KR_OFF + DN_A + DR_A].set(w[:, D_CQ + D_C:D_CQ + D_C + DR_A])
    out = out.at[:, EVEN_PR_OFF:].set(w[:, D_CQ + D_C + DR_A:])
    return out.astype(BF16)


def _even_prompt(x, b, t, pos, g_mix, P, W):
    proj = norm_matmul(x, g_mix, W['in'], after=W['after'])
    qc, kc, vc, ckv, kr = mla_prep(proj, pos, P)
    o_a = mla_prompt_attention(qc, kc, vc, b, t)
    o_b, sh, wkv = rwkv_mix(proj, jnp.zeros((b, RW_WIDTH), F32), jnp.zeros((b, H_B, HS_B, HS_B), F32),
                            b, t, P, nb=min(b, 4), tc=64, tm=256)
    x = matmul_res([o_a, o_b], W['out'], x)
    return x, lambda: (ckv.reshape(b, t, D_C), kr[:, DN_A:DN_A + DR_A].reshape(b, t, DR_A), sh, wkv)


def _even_sample(x, b, t, pos, page_table, cache_ckv, cache_kr, rw_shift, rw_wkv, g_mix, P, W):
    proj = norm_matmul(x, g_mix, W['in'], after=W['after'])
    qc, _, _, ckv, kr = mla_prep(proj, pos, P)
    qc = qc.reshape(b, t, H_A, HG)
    ckv = ckv.reshape(b, t, D_C)
    kr = kr[:, DN_A:DN_A + DR_A].reshape(b, t, DR_A)
    o_a = mla_sample_attention(page_table, cache_ckv, cache_kr, ckv, kr, qc[..., :DN_A], qc[..., DN_A:DN_A + DR_A], P)
    o_b, sh, wkv = rwkv_mix(proj, rw_shift, rw_wkv, b, t, P, nb=8, tc=t, tm=256)
    x = matmul_res([o_a, o_b], W['out'], x)
    return x, lambda: (ckv, kr, sh, wkv)


def _odd_states(proj, b, t):
    p3 = proj.reshape(b, t, ODD_W)
    if t % LANES == 0:
        k_new, v_new = sb_kv_state(proj, b, t)
    else:
        k_new, v_new = (p3[:, :, c * HC:(c + 1) * HC].reshape(b, t, H_C, HD_C) for c in (1, 2))
    return k_new, v_new, p3[:, t - (CONV_W - 1):, ODD_XBC_OFF:ODD_XBC_OFF + CONV_DIM]


def _odd_prompt(x, b, t, g_mix, P, W):
    proj = norm_matmul(x, g_mix, W['in'], after=W['after'])
    o_c = sb_prompt_attention(proj, b, t)
    y, ssm = ssd_mix(proj, jnp.zeros((b, CONV_W - 1, CONV_DIM), F32), jnp.zeros((b, H_D, P_D, N_D), F32), b, t, P)
    x = matmul_res([o_c, y], W['out'], x)
    return x, lambda: _odd_states(proj, b, t) + (ssm,)


def _odd_sample(x, b, t, page_table, cache_k, cache_v, conv_state, ssm_state, g_mix, P, W):
    proj = norm_matmul(x, g_mix, W['in'], after=W['after'])
    o_c = sb_sample_attention(page_table, cache_k, cache_v, proj, b, t)
    y, ssm = ssd_mix(proj, conv_state, ssm_state, b, t, P)
    x = matmul_res([o_c, y], W['out'], x)
    return x, lambda: _odd_states(proj, b, t) + (ssm,)


PROMPT_SPLITS = 4


def kernel(x_prompt, x_sample, mem_prompt, page_table, cache_mla_ckv, cache_mla_krope, state_rwkv_shift, state_rwkv_wkv, cache_sb_k, cache_sb_v, state_ssm_conv, state_ssm, cache_mem_k, cache_mem_v, norm_mix, norm_mem, norm_ffn, w_in_even, w_out_even, mla_g_cq, mla_w_uq, mla_g_ckv, mla_w_uk, mla_w_uv, mla_g_qn, mla_g_kn, mla_g_qr, mla_g_kr, rw_mu, rw_w0, rw_w_up, rw_a0, rw_a_up, rw_g_up, rw_k_k, rw_k_a, rw_r_k, rw_ln_g, rw_ln_b, w_in_odd, w_out_odd, ssm_conv_w, ssm_conv_b, ssm_dt_bias, ssm_a_log, ssm_d, ssm_g_norm, mem_g_src, mem_w_q, mem_w_kv, mem_g_q, mem_g_k, mem_w_o, peer_w_q, peer_sub_keys, peer_u, peer_v):
    bp, tp = x_prompt.shape[:2]
    bs, ts = x_sample.shape[:2]
    depth = norm_mix.shape[0]
    nsp = PROMPT_SPLITS if bp % PROMPT_SPLITS == 0 else 1
    bh = bp // nsp
    part = lambda a, k: a[k * bh:(k + 1) * bh]
    xps = [part(x_prompt, k).reshape(bh * tp, D_MODEL) for k in range(nsp)]
    xs = x_sample.reshape(bs * ts, D_MODEL)
    pos_p = jnp.tile(jnp.arange(tp, dtype=I32), bh)
    pos_s = jnp.tile(PAST_LEN + jnp.arange(ts, dtype=I32), bs)
    def layer_context(layer):
        i = layer // 2
        even = layer % 2 == 0
        if even:
            P = dict(w_in=w_in_even[i], w_out=w_out_even[i], g_cq=mla_g_cq[i], w_uq=mla_w_uq[i],
                     g_ckv=mla_g_ckv[i], w_uk=mla_w_uk[i], w_uv=mla_w_uv[i], g_qn=mla_g_qn[i],
                     g_kn=mla_g_kn[i], g_qr=mla_g_qr[i], g_kr=mla_g_kr[i],
                     mu=rw_mu[i], w0=rw_w0[i], w_up=rw_w_up[i], a0=rw_a0[i], a_up=rw_a_up[i],
                     g_up=rw_g_up[i], k_k=rw_k_k[i], k_a=rw_k_a[i], r_k=rw_r_k[i],
                     ln_g=rw_ln_g[i], ln_b=rw_ln_b[i])
            w_out = P['w_out'].astype(BF16)
            W = {'in': _even_w_in(P['w_in']), 'out': [w_out[:H_A * DV_A], w_out[H_A * DV_A:]]}
        else:
            P = dict(w_in=w_in_odd[i], w_out=w_out_odd[i], conv_w=ssm_conv_w[i], conv_b=ssm_conv_b[i],
                     dt_bias=ssm_dt_bias[i], a_log=ssm_a_log[i], d=ssm_d[i], g_norm=ssm_g_norm[i])
            w_out = P['w_out'].astype(BF16)
            W = {'in': jnp.pad(P['w_in'], ((0, 0), (0, ODD_W - P['w_in'].shape[1]))).astype(BF16),
                 'out': [w_out[:HC], w_out[HC:]]}
        mk, mv = memory_kv(mem_prompt.reshape(bp * N_MEM, D_MODEL), mem_g_src[layer], mem_w_kv[layer], mem_g_k[layer])
        return dict(i=i, even=even, P=P, W=W, mk=mk.reshape(bp, N_MEM, MEM_W), mv=mv.reshape(bp, N_MEM, MEM_W))

    contexts, p_state, s_state = {}, {}, {}
    state = {'xs': xs, 'tc_done': None}

    def run(group, layer):
        if layer not in contexts:
            contexts[layer] = layer_context(layer)
        c = contexts[layer]
        i, even, P = c['i'], c['even'], c['P']
        W = dict(c['W'], after=state['tc_done'])
        g_mix = norm_mix[layer]
        if group < nsp:
            if even:
                x, p_state[layer, group] = _even_prompt(xps[group], bh, tp, pos_p, g_mix, P, W)
            else:
                x, p_state[layer, group] = _odd_prompt(xps[group], bh, tp, g_mix, P, W)
            t, mk, mv, tq = tp, part(c['mk'], group), part(c['mv'], group), 256
        else:
            if even:
                x, s_state[layer] = _even_sample(state['xs'], bs, ts, pos_s, page_table, cache_mla_ckv[i],
                                                 cache_mla_krope[i], state_rwkv_shift[i], state_rwkv_wkv[i], g_mix, P, W)
            else:
                x, s_state[layer] = _odd_sample(state['xs'], bs, ts, page_table, cache_sb_k[i], cache_sb_v[i],
                                                state_ssm_conv[i], state_ssm[i], g_mix, P, W)
            t, tq = ts, ts
            mk, mv = (cache_mem_k[layer].reshape(bs, N_MEM, MEM_W), cache_mem_v[layer].reshape(bs, N_MEM, MEM_W))
        x = memory_attend(x, t, norm_mem[layer], mem_w_q[layer], mem_g_q[layer], mk, mv, mem_w_o[layer], tq=tq)
        hn, idx, gate = peer_route(x, norm_ffn[layer], peer_w_q[layer], peer_sub_keys[layer])
        x = peer_experts(hn, idx, gate, peer_u[layer], peer_v[layer], x)
        state['tc_done'] = idx
        if group < nsp:
            xps[group] = x
        else:
            state['xs'] = x

    for layer in range(depth):
        for group in range(nsp + 1):
            run(group, layer)
    xs = state['xs']

    def join(layer):
        parts = [p_state[layer, k]() for k in range(nsp)]
        return tuple(jnp.concatenate([p[j] for p in parts], axis=0) for j in range(len(parts[0])))

    even_p = [join(layer) for layer in range(0, depth, 2)]
    odd_p = [join(layer) for layer in range(1, depth, 2)]
    even_s = [s_state[layer]() for layer in range(0, depth, 2)]
    odd_s = [s_state[layer]() for layer in range(1, depth, 2)]
    mem_k = [contexts[layer]['mk'].reshape(bp, N_MEM, MEM_HEADS, MEM_HD) for layer in range(depth)]
    mem_v = [contexts[layer]['mv'].reshape(bp, N_MEM, MEM_HEADS, MEM_HD) for layer in range(depth)]
    stack = lambda groups, k: jnp.stack([g[k] for g in groups])
    xp = jnp.concatenate(xps, axis=0)
    return (xp.reshape(bp, tp, D_MODEL), xs.reshape(bs, ts, D_MODEL),
            stack(even_p, 0), stack(even_p, 1), stack(even_p, 2), stack(even_p, 3),
            stack(odd_p, 0), stack(odd_p, 1), stack(odd_p, 2), stack(odd_p, 3),
            jnp.stack(mem_k), jnp.stack(mem_v),
            stack(even_s, 0), stack(even_s, 1), stack(even_s, 2), stack(even_s, 3),
            stack(odd_s, 0), stack(odd_s, 1), stack(odd_s, 2), stack(odd_s, 3))
```

```python
import functools
import math

import numpy as np
import jax
import jax.numpy as jnp
from jax import lax
from jax.experimental import pallas as pl
from jax.experimental.pallas import tpu as pltpu
from jax.experimental.pallas import tpu_sc as plsc

F32 = jnp.float32
BF16 = jnp.bfloat16
I32 = jnp.int32

D_MODEL = 1024
EPS = 1e-6
PAST_LEN = 8192
PAGE_SIZE = 128
N_PAGES = PAST_LEN // PAGE_SIZE

H_A, DN_A, DR_A, DV_A, D_CQ, D_C = 8, 64, 32, 64, 384, 256
ROPE_BASE = 10000.0
MLA_SCALE = (DN_A + DR_A) ** -0.5
H_B, HS_B, D_WL, D_AL, D_GL = 8, 64, 64, 64, 128
HB = H_B * HS_B
RW_WIDTH = 3 * HB + D_WL + D_AL + D_GL
RWKV_DECAY_SCALE = 0.606531
RWKV_LN_EPS = 64e-5
H_C, HD_C = 8, 64
HC = H_C * HD_C
H_D, P_D, N_D, G_D, CONV_W = 8, 64, 128, 2, 4
D_INNER = H_D * P_D
CONV_DIM = D_INNER + 2 * G_D * N_D
SSD_CHUNK = 128
N_MEM, MEM_HEADS, MEM_HD = 256, 4, 64
MEM_W = MEM_HEADS * MEM_HD
PEER_HEADS, N_KEYS, PEER_DQ, PEER_TOPK = 8, 128, 256, 16
LANES = 128
VMEM_LIMIT = 56 * 1024 * 1024


def _cp(*sem):
    return pltpu.CompilerParams(dimension_semantics=sem, vmem_limit_bytes=VMEM_LIMIT)


def _dot(a, b):
    return jnp.dot(a, b, preferred_element_type=F32)


def _dot_nt(a, b):
    return lax.dot_general(a, b, (((1,), (1,)), ((), ())), preferred_element_type=F32)


def _dot_tn(a, b):
    return lax.dot_general(a, b, (((0,), (0,)), ((), ())), preferred_element_type=F32)


def _split2(x):
    hi = x.astype(BF16)
    lo = (x - hi.astype(F32)).astype(BF16)
    return hi, lo


def _dot_f32(x, m):
    hi, lo = _split2(x)
    return _dot(hi, m) + _dot(lo, m)


def _dot_sel(x, m):
    hi = x.astype(BF16)
    r1 = x - hi.astype(F32)
    mid = r1.astype(BF16)
    lo = (r1 - mid.astype(F32)).astype(BF16)
    return (_dot(hi, m) + _dot(mid, m)) + _dot(lo, m)


def _rms(x, g):
    return x * lax.rsqrt(jnp.mean(x * x, axis=-1, keepdims=True) + EPS) * g


def _sigmoid(x):
    return 1.0 / (1.0 + jnp.exp(-x))


def _softplus(x):
    return jnp.maximum(x, 0.0) + jnp.log(1.0 + jnp.exp(-jnp.abs(x)))


def _blockdiag_ones(n, seg):
    i = np.arange(n)
    return (i[:, None] // seg == i[None, :] // seg).astype(np.float32)


def _nm_body(x_ref, g_ref, w_ref, *rest):
    o_ref = rest[-1]
    h = _rms(x_ref[...], g_ref[...])
    o_ref[...] = _dot(h.astype(BF16), w_ref[...])


def norm_matmul(x, g, w, tm=256, after=None):
    n, k = x.shape
    m = w.shape[1]
    extra = [] if after is None else [after]
    return pl.pallas_call(
        _nm_body,
        grid=(n // tm,),
        in_specs=[pl.BlockSpec((tm, k), lambda i: (i, 0)),
                  pl.BlockSpec((1, k), lambda i: (0, 0)),
                  pl.BlockSpec((k, m), lambda i: (0, 0))] + [pl.BlockSpec(memory_space=pl.ANY) for _ in extra],
        out_specs=pl.BlockSpec((tm, m), lambda i: (i, 0)),
        out_shape=jax.ShapeDtypeStruct((n, m), F32),
        compiler_params=_cp("parallel"),
        name="norm_matmul",
    )(x, g.reshape(1, k), w, *extra)


def _mr_body(*refs, n_in):
    a_refs, w_refs, r_ref, o_ref = refs[:n_in], refs[n_in:2 * n_in], refs[2 * n_in], refs[2 * n_in + 1]
    acc = r_ref[...]
    for a_ref, w_ref in zip(a_refs, w_refs):
        acc = acc + _dot(a_ref[...].astype(BF16), w_ref[...])
    o_ref[...] = acc


def matmul_res(a_list, w_list, res, tm=256):
    n, m = res.shape
    n_in = len(a_list)
    in_specs = ([pl.BlockSpec((tm, a.shape[1]), lambda i: (i, 0)) for a in a_list]
                + [pl.BlockSpec(w.shape, lambda i: (0, 0)) for w in w_list]
                + [pl.BlockSpec((tm, m), lambda i: (i, 0))])
    return pl.pallas_call(
        functools.partial(_mr_body, n_in=n_in),
        grid=(n // tm,),
        in_specs=in_specs,
        out_specs=pl.BlockSpec((tm, m), lambda i: (i, 0)),
        out_shape=jax.ShapeDtypeStruct((n, m), F32),
        compiler_params=_cp("parallel"),
        name="matmul_res",
    )(*a_list, *w_list, res)


def _memkv_body(m_ref, g_ref, w_ref, gk_ref, e_ref, k_ref, v_ref):
    h = _rms(m_ref[...], g_ref[...])
    kv = _dot(h.astype(BF16), w_ref[...])
    k = kv[:, :MEM_W]
    ms = _dot_f32(k * k, e_ref[...]) * (1.0 / MEM_HD)
    k_ref[...] = k * lax.rsqrt(ms + EPS) * gk_ref[...]
    v_ref[...] = kv[:, MEM_W:]


def memory_kv(mem2d, g_src, w_kv, g_k, tm=256):
    n = mem2d.shape[0]
    e = jnp.asarray(_blockdiag_ones(MEM_W, MEM_HD), BF16)
    gk = jnp.tile(g_k, MEM_HEADS).reshape(1, MEM_W)
    return pl.pallas_call(
        _memkv_body,
        grid=(n // tm,),
        in_specs=[pl.BlockSpec((tm, D_MODEL), lambda i: (i, 0)),
                  pl.BlockSpec((1, D_MODEL), lambda i: (0, 0)),
                  pl.BlockSpec((D_MODEL, 2 * MEM_W), lambda i: (0, 0)),
                  pl.BlockSpec((1, MEM_W), lambda i: (0, 0)),
                  pl.BlockSpec((MEM_W, MEM_W), lambda i: (0, 0))],
        out_specs=[pl.BlockSpec((tm, MEM_W), lambda i: (i, 0))] * 2,
        out_shape=[jax.ShapeDtypeStruct((n, MEM_W), F32)] * 2,
        compiler_params=_cp("parallel"),
        name="memory_kv",
    )(mem2d, g_src.reshape(1, -1), w_kv.astype(BF16), gk, e)


def _memattn_body(x_ref, g_ref, wq_ref, gq_ref, e_ref, k_ref, v_ref, wo_ref, o_ref, att_ref):
    x = x_ref[...]
    h = _rms(x, g_ref[...])
    q = _dot(h.astype(BF16), wq_ref[...])
    ms = _dot_f32(q * q, e_ref[...]) * (1.0 / MEM_HD)
    q = (q * lax.rsqrt(ms + EPS) * gq_ref[...]).astype(BF16)
    k = k_ref[0].astype(BF16)
    v = v_ref[0].astype(BF16)
    for hd in range(MEM_HEADS):
        sl = slice(hd * MEM_HD, (hd + 1) * MEM_HD)
        s = _dot_nt(q[:, sl], k[:, sl]) * (MEM_HD ** -0.5)
        p = jnp.exp(s - jnp.max(s, axis=-1, keepdims=True))
        p = p / jnp.sum(p, axis=-1, keepdims=True)
        att_ref[:, sl] = _dot(p.astype(BF16), v[:, sl])
    o_ref[...] = x + _dot(att_ref[...].astype(BF16), wo_ref[...])


def memory_attend(x, seq_t, g_norm, w_q, g_q, k, v, w_o, tq):
    n = x.shape[0]
    per_b = seq_t // tq
    e = jnp.asarray(_blockdiag_ones(MEM_W, MEM_HD), BF16)
    gq = jnp.tile(g_q, MEM_HEADS).reshape(1, MEM_W)
    return pl.pallas_call(
        _memattn_body,
        grid=(n // tq,),
        in_specs=[pl.BlockSpec((tq, D_MODEL), lambda i: (i, 0)),
                  pl.BlockSpec((1, D_MODEL), lambda i: (0, 0)),
                  pl.BlockSpec((D_MODEL, MEM_W), lambda i: (0, 0)),
                  pl.BlockSpec((1, MEM_W), lambda i: (0, 0)),
                  pl.BlockSpec((MEM_W, MEM_W), lambda i: (0, 0)),
                  pl.BlockSpec((1, N_MEM, MEM_W), lambda i: (i // per_b, 0, 0)),
                  pl.BlockSpec((1, N_MEM, MEM_W), lambda i: (i // per_b, 0, 0)),
                  pl.BlockSpec((MEM_W, D_MODEL), lambda i: (0, 0))],
        out_specs=pl.BlockSpec((tq, D_MODEL), lambda i: (i, 0)),
        out_shape=jax.ShapeDtypeStruct((n, D_MODEL), F32),
        scratch_shapes=[pltpu.VMEM((tq, MEM_W), F32)],
        compiler_params=_cp("parallel"),
        name="memory_attend",
    )(x, g_norm.reshape(1, -1), w_q.astype(BF16), gq, e, k, v, w_o.astype(BF16))


def _topk_rows(s, k, payload=None, order=None):
    rows, cols = s.shape
    ridx = lax.broadcasted_iota(I32, (rows, cols), 0) if order is None else order
    orow = lax.broadcasted_iota(I32, (k, cols), 0)

    def body(r, carry):
        s, acc_v, acc_i = carry
        m = jnp.max(s, axis=0, keepdims=True)
        first = jnp.min(jnp.where(s == m, ridx, jnp.iinfo(jnp.int32).max), axis=0, keepdims=True)
        hit = ridx == first
        if payload is None:
            pick = first.astype(F32)
        else:
            pick = jnp.max(jnp.where(hit, payload, -1.0), axis=0, keepdims=True)
        here = orow == r
        return jnp.where(hit, -jnp.inf, s), jnp.where(here, m, acc_v), jnp.where(here, pick, acc_i)

    z = jnp.zeros((k, cols), F32)
    _, acc_v, acc_i = lax.fori_loop(0, k, body, (s, z, z))
    return acc_v, acc_i


def _route_body(x_ref, g_ref, wq_ref, keys_ref, hn_ref, idx_ref, gate_ref):
    h = _rms(x_ref[...], g_ref[...])
    hn_ref[...] = h
    q = _dot(h.astype(BF16), wq_ref[...])
    tops = []
    for c in range(2):
        s = _dot_nt(keys_ref[0, c], q[:, c * LANES:(c + 1) * LANES].astype(BF16))
        tops.append(_topk_rows(s, PEER_TOPK))
    (s0, i0), (s1, i1) = tops
    tm = s0.shape[1]
    ss, ii, ff = [], [], []

    def rows_a(a, nb):
        ss.append(s0[a:a + 1, :] + s1[:nb])
        ii.append(i0[a:a + 1, :] * float(N_KEYS) + i1[:nb])
        ff.append(a * PEER_TOPK + lax.broadcasted_iota(I32, (nb, tm), 0))

    def rows_b(b, na, a_min):
        a_idx = lax.broadcasted_iota(I32, (na, tm), 0)
        ss.append(jnp.where(a_idx >= a_min, s0[:na] + s1[b:b + 1, :], -jnp.inf))
        ii.append(i0[:na] * float(N_KEYS) + i1[b:b + 1, :])
        ff.append(a_idx * PEER_TOPK + b)

    half = PEER_TOPK // 2
    plan_a = [(0, PEER_TOPK)] + [(a, half) for a in range(1, 4)]
    plan_b = [(0, PEER_TOPK, 4)] + [(b, half, 4) for b in range(1, 3)]
    covered = [(a, b) for a, nb in plan_a for b in range(nb)] + [(a, b) for b, na, lo in plan_b for a in range(lo, na)]
    needed = {(a, b) for a in range(PEER_TOPK) for b in range(PEER_TOPK) if (a + 1) * (b + 1) <= PEER_TOPK}
    assert len(set(covered)) == len(covered) and needed <= set(covered)
    for a, nb in plan_a:
        rows_a(a, nb)
    for b, na, lo in plan_b:
        rows_b(b, na, lo)
    cand_s, cand_i, cand_f = (jnp.concatenate(p, axis=0) for p in (ss, ii, ff))
    best_s, best_i = _topk_rows(cand_s, PEER_TOPK, payload=cand_i, order=cand_f)
    e = jnp.exp(best_s - best_s[0:1, :])
    idx_ref[0] = best_i.astype(I32)
    gate_ref[0] = e / jnp.sum(e, axis=0, keepdims=True)


def peer_route(x, g, w_q, sub_keys, tm=128):
    n = x.shape[0]
    keys = sub_keys.astype(BF16)
    hn, idx, gate = pl.pallas_call(
        _route_body,
        grid=(n // tm, PEER_HEADS),
        in_specs=[pl.BlockSpec((tm, D_MODEL), lambda i, hd: (i, 0)),
                  pl.BlockSpec((1, D_MODEL), lambda i, hd: (0, 0)),
                  pl.BlockSpec((D_MODEL, PEER_DQ), lambda i, hd: (0, hd)),
                  pl.BlockSpec((1, 2, N_KEYS, PEER_DQ // 2), lambda i, hd: (hd, 0, 0, 0))],
        out_specs=[pl.BlockSpec((tm, D_MODEL), lambda i, hd: (i, 0)),
                   pl.BlockSpec((1, PEER_TOPK, tm), lambda i, hd: (hd, 0, i)),
                   pl.BlockSpec((1, PEER_TOPK, tm), lambda i, hd: (hd, 0, i))],
        out_shape=[jax.ShapeDtypeStruct((n, D_MODEL), F32),
                   jax.ShapeDtypeStruct((PEER_HEADS, PEER_TOPK, n), I32),
                   jax.ShapeDtypeStruct((PEER_HEADS, PEER_TOPK, n), F32)],
        compiler_params=_cp("parallel", "arbitrary"),
        name="peer_route",
    )(x, g.reshape(1, -1), w_q.astype(BF16), keys)
    idx = jnp.transpose(idx, (2, 0, 1)).reshape(n, PEER_HEADS * PEER_TOPK)
    gate = jnp.transpose(gate, (2, 0, 1)).reshape(n, PEER_HEADS * PEER_TOPK)
    return hn, idx, gate


HG = LANES
QW = H_A * HG
EVEN_CKV_OFF = D_CQ
EVEN_KR_OFF = D_CQ + D_C
EVEN_PR_OFF = EVEN_KR_OFF + HG
EVEN_W = EVEN_PR_OFF + RW_WIDTH


def _mla_prep_body(p_ref, cs_ref, sn_ref, gcq_ref, wuq_ref, gq_ref, gckv_ref, gkr_ref, wuk_ref, gk_ref, wuv_ref,
                   eq_ref, ek_ref, perm_ref, q_out, k_out, v_out, ckv_out, kr_out):
    cs, sn = cs_ref[...], sn_ref[...]

    def rope(x):
        return x * cs + _dot_sel(x, perm_ref[...]) * sn

    q = _dot(_rms(p_ref[:, :D_CQ], gcq_ref[...]).astype(BF16), wuq_ref[...])
    ckv = _rms(p_ref[:, EVEN_CKV_OFF:EVEN_CKV_OFF + D_C], gckv_ref[...])
    ckv_out[...] = ckv
    krr = p_ref[:, EVEN_KR_OFF:EVEN_KR_OFF + HG]
    ms = jnp.sum(krr * krr, axis=-1, keepdims=True) * (1.0 / DR_A)
    kr = rope(krr * lax.rsqrt(ms + EPS) * gkr_ref[...])
    kr_out[...] = kr
    ckv_b = ckv.astype(BF16)
    kraw = _dot(ckv_b, wuk_ref[...])
    v_out[...] = _dot(ckv_b, wuv_ref[...]).astype(BF16)
    for h in range(H_A):
        sl = slice(h * HG, (h + 1) * HG)
        qh = q[:, sl]
        qh = qh * lax.rsqrt(_dot_f32(qh * qh, eq_ref[...]) + EPS) * gq_ref[:, sl]
        q_out[:, sl] = rope(qh).astype(BF16)
        kh = kraw[:, sl]
        kh = kh * lax.rsqrt(_dot_f32(kh * kh, ek_ref[...]) + EPS) * gk_ref[:, sl]
        k_out[:, sl] = (kh + kr).astype(BF16)


def _head_groups(w, width, off=0):
    k = w.shape[0]
    out = jnp.zeros((k, H_A, HG), w.dtype)
    out = out.at[:, :, off:off + width].set(w.reshape(k, H_A, width))
    return out.reshape(k, H_A * HG)


def mla_prep(proj, pos, P, tm=256):
    n = proj.shape[0]
    half = DR_A // 2
    freqs = ROPE_BASE ** (-jnp.arange(half, dtype=F32) / half)
    ang = pos.astype(F32)[:, None] * freqs[None, :]
    cos, sin = jnp.cos(ang), jnp.sin(ang)
    one, zero = jnp.ones((n, DN_A), F32), jnp.zeros((n, DN_A), F32)
    pad = jnp.zeros((n, HG - DN_A - DR_A), F32)
    cs = jnp.concatenate([one, cos, cos, pad], axis=1)
    sn = jnp.concatenate([zero, -sin, sin, pad], axis=1)
    wuq = P['w_uq'].reshape(D_CQ, H_A, DN_A + DR_A)
    wuq = jnp.pad(wuq, ((0, 0), (0, 0), (0, HG - DN_A - DR_A))).reshape(D_CQ, QW).astype(BF16)
    gq = jnp.tile(jnp.concatenate([P['g_qn'], P['g_qr'], jnp.zeros((HG - DN_A - DR_A,), F32)]), H_A).reshape(1, QW)
    gkr = jnp.zeros((1, HG), F32).at[0, DN_A:DN_A + DR_A].set(P['g_kr'])
    wuk = _head_groups(P['w_uk'], DN_A).astype(BF16)
    wuv = _head_groups(P['w_uv'], DV_A).astype(BF16)
    gk = _head_groups(jnp.tile(P['g_kn'], H_A).reshape(1, -1), DN_A)
    eq = np.zeros((HG, HG), np.float32)
    eq[:DN_A, :DN_A] = 1.0 / DN_A
    eq[DN_A:DN_A + DR_A, DN_A:DN_A + DR_A] = 1.0 / DR_A
    ek = np.zeros((HG, HG), np.float32)
    ek[:DN_A, :DN_A] = 1.0 / DN_A
    perm = np.zeros((HG, HG), np.float32)
    j = np.arange(half)
    perm[DN_A + half + j, DN_A + j] = 1.0
    perm[DN_A + j, DN_A + half + j] = 1.0
    row = lambda a: pl.BlockSpec(a.shape, lambda i: (0, 0))
    consts = [P['g_cq'].reshape(1, -1), wuq, gq, P['g_ckv'].reshape(1, -1), gkr, wuk, gk, wuv,
              jnp.asarray(eq, BF16), jnp.asarray(ek, BF16), jnp.asarray(perm, BF16)]
    return pl.pallas_call(
        _mla_prep_body,
        grid=(n // tm,),
        in_specs=[pl.BlockSpec((tm, EVEN_PR_OFF), lambda i: (i, 0)),
                  pl.BlockSpec((tm, HG), lambda i: (i, 0)),
                  pl.BlockSpec((tm, HG), lambda i: (i, 0))] + [row(c) for c in consts],
        out_specs=[pl.BlockSpec((tm, QW), lambda i: (i, 0))] * 3
        + [pl.BlockSpec((tm, D_C), lambda i: (i, 0)), pl.BlockSpec((tm, HG), lambda i: (i, 0))],
        out_shape=[jax.ShapeDtypeStruct((n, QW), BF16)] * 3
        + [jax.ShapeDtypeStruct((n, D_C), F32), jax.ShapeDtypeStruct((n, HG), F32)],
        compiler_params=_cp("parallel"),
        name="mla_prep",
    )(proj, cs, sn, *consts)


def _mla_prompt_body(q_ref, k_ref, v_ref, o_ref, *, tq):
    qi = pl.program_id(1)
    row = lax.broadcasted_iota(I32, (tq, tq), 0)
    col = lax.broadcasted_iota(I32, (tq, tq), 1)
    for h in range(H_A):
        sl = slice(h * HG, (h + 1) * HG)
        q = q_ref[:, sl]

        def body(kb, carry):
            m, l, acc = carry
            off = pl.multiple_of(kb * tq, tq)
            s = _dot_nt(q, k_ref[pl.ds(off, tq), sl]) * MLA_SCALE
            s = jnp.where(col + kb * tq <= row + qi * tq, s, -jnp.inf)
            m_new = jnp.maximum(m, jnp.max(s, axis=-1, keepdims=True))
            alpha = jnp.exp(m - m_new)
            p = jnp.exp(s - m_new)
            l = l * alpha + jnp.sum(p, axis=-1, keepdims=True)
            acc = acc * alpha + _dot(p.astype(BF16), v_ref[pl.ds(off, tq), sl])
            return m_new, l, acc

        init = (jnp.full((tq, 1), -jnp.inf, F32), jnp.zeros((tq, 1), F32), jnp.zeros((tq, HG), F32))
        _, l, acc = lax.fori_loop(0, qi + 1, body, init)
        o_ref[:, h * DV_A:(h + 1) * DV_A] = (acc / l)[:, :DV_A].astype(BF16)


def mla_prompt_attention(qc, kc, vc, b, t, tq=256):
    nq = t // tq
    return pl.pallas_call(
        functools.partial(_mla_prompt_body, tq=tq),
        grid=(b, nq),
        in_specs=[pl.BlockSpec((tq, QW), lambda bi, qi: (bi * nq + qi, 0)),
                  pl.BlockSpec((t, QW), lambda bi, qi: (bi, 0)),
                  pl.BlockSpec((t, QW), lambda bi, qi: (bi, 0))],
        out_specs=pl.BlockSpec((tq, H_A * DV_A), lambda bi, qi: (bi * nq + qi, 0)),
        out_shape=jax.ShapeDtypeStruct((b * t, H_A * DV_A), BF16),
        compiler_params=_cp("parallel", "arbitrary"),
        name="mla_prompt_attention",
    )(qc, kc, vc)


PAGES_PER_STEP = 16
KV_STEPS = N_PAGES // PAGES_PER_STEP


def _mla_sample_body(pt_ref, *refs, t_new):
    npg = PAGES_PER_STEP
    ckv_pages, kr_pages = refs[:npg], refs[npg:2 * npg]
    (ckv_new, kr_new, qn_ref, qr_ref, wuk_ref, e_ref, wuv_ref, hm_ref, o_ref,
     m_scr, l_scr, acc_scr) = refs[2 * npg:]
    j = pl.program_id(1)

    def process(ckv, kr, mask):
        ckv_b = ckv.astype(BF16)
        kraw = _dot(ckv_b, wuk_ref[...])
        ms = _dot_f32(kraw * kraw, e_ref[...]) * (1.0 / DN_A)
        s = (_dot(kraw.astype(BF16), qn_ref[0]) * lax.rsqrt(ms + EPS)
             + _dot_tn(kr.astype(BF16), qr_ref[0])) * MLA_SCALE
        if mask is not None:
            s = jnp.where(mask, s, -jnp.inf)
        m = m_scr[...]
        m_new = jnp.maximum(m, jnp.max(s, axis=0, keepdims=True))
        alpha = jnp.exp(m - m_new)
        p = jnp.exp(s - m_new)
        l_scr[...] = l_scr[...] * alpha + jnp.sum(p, axis=0, keepdims=True)
        acc_scr[...] = acc_scr[...] * alpha + _dot_tn(ckv_b, p.astype(BF16))
        m_scr[...] = m_new

    @pl.when(j == 0)
    def _():
        m_scr[...] = jnp.full(m_scr.shape, -jnp.inf, F32)
        l_scr[...] = jnp.zeros(l_scr.shape, F32)
        acc_scr[...] = jnp.zeros(acc_scr.shape, F32)
        nk, nq = PAGE_SIZE, H_A * t_new
        key = lax.broadcasted_iota(I32, (nk, nq), 0)
        qt = lax.broadcasted_iota(I32, (nk, nq), 1) % t_new
        process(ckv_new[0], kr_new[0], key <= qt)

    process(jnp.concatenate([r[0] for r in ckv_pages], axis=0), jnp.concatenate([r[0] for r in kr_pages], axis=1), None)

    @pl.when(j == KV_STEPS - 1)
    def _():
        o_lat =(acc_scr[...] / l_scr[...]).astype(BF16)
        full = _dot_tn(o_lat, wuv_ref[...]) * hm_ref[...]
        out = full[0:t_new]
        for h in range(1, H_A):
            out = out + full[h * t_new:(h + 1) * t_new]
        o_ref[0] = out.astype(BF16)


def mla_sample_attention(page_table, cache_ckv, cache_kr, ckv_new, kr_new, qn, qr, P):
    b, t_new = qn.shape[:2]
    nq = H_A * t_new
    eye = jnp.eye(H_A, dtype=BF16)
    qn_g = (qn.astype(F32) * P['g_kn']).astype(BF16)
    qn_bd = jnp.einsum('bthd,hg->bhdgt', qn_g, eye).reshape(b, H_A * DN_A, nq)
    qr_m = jnp.transpose(qr, (0, 3, 2, 1)).reshape(b, DR_A, nq)
    pad = lambda a: jnp.pad(a, ((0, 0), (0, PAGE_SIZE - t_new), (0, 0)))
    keys_last = lambda a: jnp.swapaxes(a, 1, 2)
    cache_kr = keys_last(cache_kr)
    hm = (np.arange(nq)[:, None] // t_new == np.arange(H_A * DV_A)[None, :] // DV_A).astype(np.float32)
    e_head = (np.arange(H_A * DN_A)[:, None] // DN_A == np.arange(nq)[None, :] // t_new).astype(np.float32)
    npg = PAGES_PER_STEP
    page = lambda p, s: pl.BlockSpec((1,) + s, lambda bi, j, pt: (pt[bi * N_PAGES + j * npg + p], 0, 0))
    per_b = lambda s: pl.BlockSpec((1,) + s, lambda bi, j, pt: (bi, 0, 0))
    const = lambda a: pl.BlockSpec(a.shape, lambda bi, j, pt: (0, 0))
    consts = [P['w_uk'].astype(BF16), jnp.asarray(e_head, BF16), P['w_uv'].astype(BF16), jnp.asarray(hm)]
    grid_spec = pltpu.PrefetchScalarGridSpec(
        num_scalar_prefetch=1,
        grid=(b, KV_STEPS),
        in_specs=[page(p, (PAGE_SIZE, D_C)) for p in range(npg)] + [page(p, (DR_A, PAGE_SIZE)) for p in range(npg)]
        + [per_b((PAGE_SIZE, D_C)), per_b((DR_A, PAGE_SIZE)), per_b((H_A * DN_A, nq)), per_b((DR_A, nq))]
        + [const(c) for c in consts],
        out_specs=pl.BlockSpec((1, t_new, H_A * DV_A), lambda bi, j, pt: (bi, 0, 0)),
        scratch_shapes=[pltpu.VMEM((1, nq), F32), pltpu.VMEM((1, nq), F32), pltpu.VMEM((D_C, nq), F32)],
    )
    out = pl.pallas_call(
        functools.partial(_mla_sample_body, t_new=t_new),
        grid_spec=grid_spec,
        out_shape=jax.ShapeDtypeStruct((b, t_new, H_A * DV_A), BF16),
        compiler_params=_cp("parallel", "arbitrary"),
        name="mla_sample_attention",
    )(page_table.reshape(-1), *([cache_ckv] * npg), *([cache_kr] * npg), pad(ckv_new), keys_last(pad(kr_new)),
      qn_bd, qr_m, *consts)
    return out.reshape(b * t_new, H_A * DV_A)


RW_LORA_OFF = 3 * HB


def _rwkv_prep_body(pr_ref, sh_ref, mu_ref, w0_ref, wup_ref, a0_ref, aup_ref, gup_ref, kk_ref, ka_ref, rk_ref, e_ref,
                    nkk_o, wr_o, w_o, kka_o, k2_o, v_o, c1_o, c2_o, g_o, bonus_o):
    pr = pr_ref[...]
    xs = pr + mu_ref[...] * (sh_ref[...] - pr)
    r, k, v = xs[:, :HB], xs[:, HB:2 * HB], xs[:, 2 * HB:3 * HB]
    xwa = xs[:, RW_LORA_OFF:RW_LORA_OFF + D_WL + D_AL]
    xg = xs[:, RW_LORA_OFF + D_WL + D_AL:]
    w = jnp.exp(-RWKV_DECAY_SCALE * _sigmoid(w0_ref[...] + _dot(jnp.tanh(xwa).astype(BF16), wup_ref[...])))
    a = _sigmoid(a0_ref[...] + _dot(xwa.astype(BF16), aup_ref[...]))
    g_o[...] = _dot(_sigmoid(xg).astype(BF16), gup_ref[...])
    kk = k * kk_ref[...]
    kk = kk * lax.rsqrt(jnp.maximum(_dot_f32(kk * kk, e_ref[...]), 1e-12))
    k2 = k * (1.0 + (a - 1.0) * ka_ref[...])
    kka = kk * a
    nkk_o[...] = -kk
    wr_o[...] = w * r
    w_o[...] = w
    kka_o[...] = kka
    k2_o[...] = k2
    v_o[...] = v
    c1_o[...] = _dot_f32(kka * r, e_ref[...])
    c2_o[...] = _dot_f32(k2 * r, e_ref[...])
    bonus_o[...] = _dot_f32(r * k2 * rk_ref[...], e_ref[...]) * v


def rwkv_prep(proj, shifted, P, tm=256):
    n = proj.shape[0]
    nblk = EVEN_PR_OFF // RW_WIDTH
    assert EVEN_PR_OFF % LANES == 0
    zw = jnp.zeros((D_WL, HB), F32)
    consts = [P['mu'].reshape(1, -1), P['w0'].reshape(1, -1),
              jnp.concatenate([P['w_up'], zw]).astype(BF16), P['a0'].reshape(1, -1),
              jnp.concatenate([zw, P['a_up']]).astype(BF16), P['g_up'].astype(BF16),
              P['k_k'].reshape(1, -1), P['k_a'].reshape(1, -1), P['r_k'].reshape(1, -1),
              jnp.asarray(_blockdiag_ones(HB, HS_B), BF16)]
    pr = lax.slice_in_dim(proj, EVEN_PR_OFF, EVEN_W, axis=1)
    return pl.pallas_call(
        _rwkv_prep_body,
        grid=(n // tm,),
        in_specs=[pl.BlockSpec((tm, RW_WIDTH), lambda i: (i, 0)), pl.BlockSpec((tm, RW_WIDTH), lambda i: (i, 0))]
        + [pl.BlockSpec(c.shape, lambda i: (0, 0)) for c in consts],
        out_specs=[pl.BlockSpec((tm, HB), lambda i: (i, 0))] * 10,
        out_shape=[jax.ShapeDtypeStruct((n, HB), F32)] * 10,
        compiler_params=_cp("parallel"),
        name="rwkv_prep",
    )(pr, shifted, *consts)


def _rwkv_scan_body(nkk_r, wr_r, w_r, kka_r, k2_r, v_r, c1_r, c2_r, s0_ref, e_ref, d_ref, o_ref, sf_ref, s_scr,
                    *, nb, tc):
    c = pl.program_id(1)

    @pl.when(c == 0)
    def _():
        s_scr[...] = s0_ref[...]

    dmask = d_ref[...]
    half = HB // 2

    def step(t, carry):
        for b in range(nb):
            row = lambda ref: ref[b, pl.ds(t, 1), :]
            s = s_scr[b]
            stacked = jnp.concatenate([s * row(nkk_r), s * row(wr_r), dmask * row(v_r)], axis=0)
            hi, lo = _split2(stacked)
            seg = jnp.concatenate(
                [_dot(hi[:, i * half:(i + 1) * half], e_ref[...]) + _dot(lo[:, i * half:(i + 1) * half], e_ref[...])
                 for i in range(2)], axis=1)
            sa, t2, vb = seg[:HS_B], seg[HS_B:2 * HS_B], seg[2 * HS_B:]
            s_scr[b] = s * row(w_r) + sa * row(kka_r) + vb * row(k2_r)
            ob = t2 + sa * row(c1_r) + vb * row(c2_r)
            o_ref[b, pl.ds(t, 1), :] = jnp.sum(ob * dmask, axis=0, keepdims=True)
        return carry

    lax.fori_loop(0, tc, step, 0)

    @pl.when(c == pl.num_programs(1) - 1)
    def _():
        sf_ref[...] = s_scr[...]


def rwkv_scan(seqs, state0, b, t, nb, tc):
    s0 = jnp.transpose(state0, (0, 2, 1, 3)).reshape(b, HS_B, HB)
    e = jnp.asarray(_blockdiag_ones(HB // 2, HS_B), BF16)
    dmask = jnp.asarray((np.arange(HS_B)[:, None] == (np.arange(HB)[None, :] % HS_B)).astype(np.float32))
    seq_spec = pl.BlockSpec((nb, tc, HB), lambda i, c: (i, c, 0))
    st_spec = pl.BlockSpec((nb, HS_B, HB), lambda i, c: (i, 0, 0))
    o, sf = pl.pallas_call(
        functools.partial(_rwkv_scan_body, nb=nb, tc=tc),
        grid=(b // nb, t // tc),
        in_specs=[seq_spec] * 8 + [st_spec, pl.BlockSpec(e.shape, lambda i, c: (0, 0)),
                                   pl.BlockSpec(dmask.shape, lambda i, c: (0, 0))],
        out_specs=[seq_spec, st_spec],
        out_shape=[jax.ShapeDtypeStruct((b, t, HB), F32), jax.ShapeDtypeStruct((b, HS_B, HB), F32)],
        scratch_shapes=[pltpu.VMEM((nb, HS_B, HB), F32)],
        compiler_params=_cp("parallel", "arbitrary"),
        name="rwkv_scan",
    )(*[a.reshape(b, t, HB) for a in seqs], s0, e, dmask)
    sf = jnp.transpose(sf.reshape(b, HS_B, H_B, HS_B), (0, 2, 1, 3))
    return o.reshape(b * t, HB), sf


def _rwkv_post_body(o_ref, g_ref, bonus_ref, lng_ref, lnb_ref, e_ref, out_ref):
    o = o_ref[...]
    mu = _dot_f32(o, e_ref[...]) * (1.0 / HS_B)
    d = o - mu
    var = _dot_f32(d * d, e_ref[...]) * (1.0 / HS_B)
    y = d * lax.rsqrt(var + RWKV_LN_EPS) * lng_ref[...] + lnb_ref[...]
    out_ref[...] = ((y + bonus_ref[...]) * g_ref[...]).astype(BF16)


def rwkv_post(o, g, bonus, P, tm=256):
    n = o.shape[0]
    e = jnp.asarray(_blockdiag_ones(HB, HS_B), BF16)
    blk = pl.BlockSpec((tm, HB), lambda i: (i, 0))
    row = pl.BlockSpec((1, HB), lambda i: (0, 0))
    return pl.pallas_call(
        _rwkv_post_body,
        grid=(n // tm,),
        in_specs=[blk, blk, blk, row, row, pl.BlockSpec(e.shape, lambda i: (0, 0))],
        out_specs=blk,
        out_shape=jax.ShapeDtypeStruct((n, HB), BF16),
        compiler_params=_cp("parallel"),
        name="rwkv_post",
    )(o, g, bonus, P['ln_g'].reshape(1, -1), P['ln_b'].reshape(1, -1), e)


def rwkv_mix(proj, prev, state0, b, t, P, nb, tc, tm):
    pr = lax.slice_in_dim(proj, EVEN_PR_OFF, EVEN_W, axis=1).reshape(b, t, RW_WIDTH)
    shifted = jnp.concatenate([prev[:, None, :], pr[:, :-1, :]], axis=1).reshape(b * t, RW_WIDTH)
    outs = rwkv_prep(proj, shifted, P, tm)
    o, s_new = rwkv_scan(outs[:8], state0, b, t, nb, tc)
    return rwkv_post(o, outs[8], outs[9], P, tm), pr[:, -1, :], s_new


ODD_Z_OFF = 3 * HC
ODD_XBC_OFF = ODD_Z_OFF + D_INNER
ODD_DT_OFF = ODD_XBC_OFF + CONV_DIM
ODD_W = ODD_DT_OFF + LANES
SB_SCALE = HD_C ** -0.5


def _sb_weights(z, mask, suffix_of, run):
    lneg = -_softplus(z)
    if mask is not None:
        lneg = jnp.where(mask, lneg, 0.0)
    w = jnp.exp(z + lneg + suffix_of(lneg) + run)
    if mask is not None:
        w = jnp.where(mask, w, 0.0)
    return w, lneg


def _sb_prompt_body(q_ref, k_ref, v_ref, u_ref, o_ref, *, tq):
    qi = pl.program_id(1)
    row = lax.broadcasted_iota(I32, (tq, tq), 0)
    col = lax.broadcasted_iota(I32, (tq, tq), 1)
    for h in range(H_C):
        sl = slice(h * HD_C, (h + 1) * HD_C)
        q = q_ref[:, sl].astype(BF16)

        def body(i, carry):
            run, acc = carry
            kb = qi - i
            off = pl.multiple_of(kb * tq, tq)
            z = _dot_nt(q, k_ref[pl.ds(off, tq), sl].astype(BF16)) * SB_SCALE
            mask = col + kb * tq < row + qi * tq
            w, lneg = _sb_weights(z, mask, lambda l: _dot_f32(l, u_ref[...]), run)
            acc = acc + _dot(w.astype(BF16), v_ref[pl.ds(off, tq), sl].astype(BF16))
            return run + jnp.sum(lneg, axis=-1, keepdims=True), acc

        _, acc = lax.fori_loop(0, qi + 1, body, (jnp.zeros((tq, 1), F32), jnp.zeros((tq, HD_C), F32)))
        o_ref[:, sl] = acc.astype(BF16)


def sb_prompt_attention(proj, b, t, tq=256):
    nq = t // tq
    u = jnp.asarray(np.tril(np.ones((tq, tq), np.float32), -1), BF16)
    return pl.pallas_call(
        functools.partial(_sb_prompt_body, tq=tq),
        grid=(b, nq),
        in_specs=[pl.BlockSpec((tq, HC), lambda bi, qi: (bi * nq + qi, 0)),
                  pl.BlockSpec((t, HC), lambda bi, qi: (bi, 1)),
                  pl.BlockSpec((t, HC), lambda bi, qi: (bi, 2)),
                  pl.BlockSpec(u.shape, lambda bi, qi: (0, 0))],
        out_specs=pl.BlockSpec((tq, HC), lambda bi, qi: (bi * nq + qi, 0)),
        out_shape=jax.ShapeDtypeStruct((b * t, HC), BF16),
        compiler_params=_cp("parallel", "arbitrary"),
        name="sb_prompt_attention",
    )(proj, proj, proj, u)


def _kv_heads_body(k_ref, v_ref, eye_ref, kt_ref, vt_ref):
    for src, dst in ((k_ref, kt_ref), (v_ref, vt_ref)):
        x = src[...]
        hi = x.astype(BF16)
        r1 = x - hi.astype(F32)
        mid = r1.astype(BF16)
        lo = (r1 - mid.astype(F32)).astype(BF16)
        for h in range(H_C):
            sl = slice(h * HD_C, (h + 1) * HD_C)
            eye = eye_ref[...]
            dst[0, h] = (_dot_nt(eye, hi[:, sl]) + _dot_nt(eye, mid[:, sl])) + _dot_nt(eye, lo[:, sl])


def sb_kv_state(proj, b, t, tq=512):
    tq = min(tq, t)
    assert t % tq == 0
    nq = t // tq
    eye = jnp.eye(HD_C, dtype=BF16)
    out = pl.BlockSpec((1, H_C, HD_C, tq), lambda bi, qi: (bi, 0, 0, qi))
    kt, vt = pl.pallas_call(
        _kv_heads_body,
        grid=(b, nq),
        in_specs=[pl.BlockSpec((tq, HC), lambda bi, qi: (bi * nq + qi, 1)),
                  pl.BlockSpec((tq, HC), lambda bi, qi: (bi * nq + qi, 2)),
                  pl.BlockSpec(eye.shape, lambda bi, qi: (0, 0))],
        out_specs=[out, out],
        out_shape=[jax.ShapeDtypeStruct((b, H_C, HD_C, t), F32)] * 2,
        compiler_params=_cp("parallel", "parallel"),
        name="sb_kv_state",
    )(proj, proj, eye)
    return jnp.transpose(kt, (0, 3, 1, 2)), jnp.transpose(vt, (0, 3, 1, 2))


def _sb_sample_body(pt_ref, *refs, t_new):
    npg = PAGES_PER_STEP
    k_pages, v_pages = refs[:npg], refs[npg:2 * npg]
    k_new, v_new, q_ref, u_ref, o_ref, q_scr, run_scr, acc_scr = refs[2 * npg:]
    j = pl.program_id(1)
    nq = H_C * t_new
    row_head = lax.broadcasted_iota(I32, (nq, PAGE_SIZE), 0) // t_new
    hsl = lambda h: slice(h * HD_C, (h + 1) * HD_C)

    def process(k_of, v_of, mask, dot_k, dot_v):
        z = dot_k(q_scr[0], k_of(0))
        for h in range(1, H_C):
            z = z + dot_k(q_scr[h], k_of(h))
        w, lneg = _sb_weights(z * SB_SCALE, mask, lambda l: _dot_f32(l, u_ref[...]), run_scr[...])
        acc = acc_scr[...]
        for h in range(H_C):
            acc = acc + dot_v(jnp.where(row_head == h, w, 0.0).astype(BF16), v_of(h))
        acc_scr[...] = acc
        run_scr[...] += jnp.sum(lneg, axis=1, keepdims=True)

    @pl.when(j == 0)
    def _():
        run_scr[...] = jnp.zeros(run_scr.shape, F32)
        acc_scr[...] = jnp.zeros(acc_scr.shape, F32)
        q = q_ref[0]
        for h in range(H_C):
            rows = [jnp.zeros((t_new, HD_C), F32)] * H_C
            rows[h] = q[:, hsl(h)]
            q_scr[h] = jnp.concatenate(rows, axis=0).astype(BF16)
        fill = jnp.zeros((PAGE_SIZE - t_new, HC), F32)
        k_pad = jnp.concatenate([k_new[0], fill], axis=0).astype(BF16)
        v_pad = jnp.concatenate([v_new[0], fill], axis=0).astype(BF16)
        key = lax.broadcasted_iota(I32, (nq, PAGE_SIZE), 1)
        qt = lax.broadcasted_iota(I32, (nq, PAGE_SIZE), 0) % t_new
        process(lambda h: k_pad[:, hsl(h)], lambda h: v_pad[:, hsl(h)], key < qt, _dot_nt, _dot)

    for p in range(npg):
        process(lambda h: k_pages[p][0, h].astype(BF16), lambda h: v_pages[p][0, h].astype(BF16), None, _dot, _dot_nt)

    @pl.when(j == KV_STEPS - 1)
    def _():
        acc = acc_scr[...]
        for h in range(H_C):
            o_ref[0, :, h * HD_C:(h + 1) * HD_C] = acc[h * t_new:(h + 1) * t_new].astype(BF16)


def sb_sample_attention(page_table, cache_k, cache_v, proj, b, t_new):
    proj = proj.reshape(b, t_new, ODD_W)
    nq = H_C * t_new
    keys_last = lambda a: jnp.transpose(a, (0, 2, 3, 1))
    cache_k, cache_v = keys_last(cache_k), keys_last(cache_v)
    u = jnp.asarray(np.tril(np.ones((PAGE_SIZE, PAGE_SIZE), np.float32), -1), BF16)
    npg = PAGES_PER_STEP
    page = lambda p: pl.BlockSpec(
        (1, H_C, HD_C, PAGE_SIZE), lambda bi, j, pt: (pt[bi * N_PAGES + (N_PAGES - 1 - (j * npg + p))], 0, 0, 0))
    col = lambda c: pl.BlockSpec((1, t_new, HC), lambda bi, j, pt: (bi, 0, c))
    grid_spec = pltpu.PrefetchScalarGridSpec(
        num_scalar_prefetch=1,
        grid=(b, KV_STEPS),
        in_specs=[page(p) for p in range(npg)] * 2
        + [col(1), col(2), col(0), pl.BlockSpec(u.shape, lambda bi, j, pt: (0, 0))],
        out_specs=col(0),
        scratch_shapes=[pltpu.VMEM((H_C, nq, HD_C), BF16), pltpu.VMEM((nq, 1), F32), pltpu.VMEM((nq, HD_C), F32)],
    )
    out = pl.pallas_call(
        functools.partial(_sb_sample_body, t_new=t_new),
        grid_spec=grid_spec,
        out_shape=jax.ShapeDtypeStruct((b, t_new, HC), BF16),
        compiler_params=_cp("parallel", "arbitrary"),
        name="sb_sample_attention",
    )(page_table.reshape(-1), *([cache_k] * npg), *([cache_v] * npg), proj, proj, proj, u)
    return out.reshape(b * t_new, HC)


TAIL = 8


def _ssd_body(z_ref, xbc_ref, dt_ref, cbuf_ref, h0_ref, cw_ref, cb_ref, dtb_ref, alog_ref, drow_ref, gn_ref, tri_ref,
              y_ref, hf_ref, h_scr, tail_scr, y_scr, *, L):
    c = pl.program_id(1)
    mm = (lambda x: x.astype(BF16)) if L >= 16 else (lambda x: x.astype(BF16).astype(F32))

    @pl.when(c == 0)
    def _():
        h_scr[...] = h0_ref[0]
        tail_scr[...] = jnp.zeros(tail_scr.shape, F32)
        tail_scr[TAIL - (CONV_W - 1):, :] = cbuf_ref[0]

    xbc = xbc_ref[...]
    ext = jnp.concatenate([tail_scr[...], xbc], axis=0)
    conv = cb_ref[...]
    for i in range(CONV_W):
        s = TAIL - (CONV_W - 1) + i
        conv = conv + cw_ref[i:i + 1, :] * ext[s:s + L]
    tail_scr[...] = xbc[L - TAIL:]
    xc = conv * _sigmoid(conv)
    dt = _softplus(dt_ref[...] + dtb_ref[...])
    a = -jnp.exp(alog_ref[...])
    lane = lax.broadcasted_iota(I32, (1, LANES), 1)
    dta = jnp.where(lane < H_D, dt * a, 0.0)
    hi = dta.astype(BF16)
    r1 = dta - hi.astype(F32)
    mid = r1.astype(BF16)
    lo = (r1 - mid.astype(F32)).astype(BF16)
    tri = tri_ref[...]
    acs = (_dot(tri, mm(hi)) + _dot(tri, mm(mid))) + _dot(tri, mm(lo))
    hi, mid, lo = acs.astype(BF16), None, None
    r1 = acs - hi.astype(F32)
    mid = r1.astype(BF16)
    lo = (r1 - mid.astype(F32)).astype(BF16)
    row = lax.broadcasted_iota(I32, (L, L), 0)
    col = lax.broadcasted_iota(I32, (L, L), 1)
    lane_l = lax.broadcasted_iota(I32, (L, LANES), 1)
    acs_last = acs[L - 1:L, :]
    cbs = []
    for g in range(G_D):
        bg = xc[:, D_INNER + g * N_D:D_INNER + (g + 1) * N_D]
        cg = xc[:, D_INNER + G_D * N_D + g * N_D:D_INNER + G_D * N_D + (g + 1) * N_D]
        cbs.append((bg, cg, _dot_nt(mm(cg), mm(bg))))
    for h in range(H_D):
        bg, cg, cb = cbs[h // (H_D // G_D)]
        sel = mm((lane_l == h).astype(F32))
        acs_row = (_dot_nt(sel, mm(hi)) + _dot_nt(sel, mm(mid))) + _dot_nt(sel, mm(lo))
        acs_col = acs[:, h:h + 1]
        lmat = jnp.exp(jnp.where(col <= row, acs_col - acs_row, -jnp.inf))
        xh = xc[:, h * P_D:(h + 1) * P_D]
        xdt = xh * dt[:, h:h + 1]
        hprev = h_scr[h]
        y = _dot(mm(cb * lmat), mm(xdt)) + _dot_nt(mm(cg * jnp.exp(acs_col)), mm(hprev))
        y_scr[:, h * P_D:(h + 1) * P_D] = y + drow_ref[:, h * P_D:(h + 1) * P_D] * xh
        last = acs_last[:, h:h + 1]
        bw = bg * jnp.exp(last - acs_col)
        if L >= 16:
            st = _dot_tn(mm(xdt), mm(bw))
        else:
            xr, br = xdt - mm(xdt), bw - mm(bw)
            st = (_dot_tn(mm(xdt), mm(bw)) + _dot_tn(mm(xdt), mm(br))) + _dot_tn(mm(xr), mm(bw))
        h_scr[h] = hprev * jnp.exp(last) + st
    z = z_ref[...]
    y_ref[...] = _rms(y_scr[...] * (z * _sigmoid(z)), gn_ref[...]).astype(BF16)

    @pl.when(c == pl.num_programs(1) - 1)
    def _():
        hf_ref[0] = h_scr[...]


def ssd_mix(proj, conv_buf, state0, b, t, P):
    L = SSD_CHUNK if t % SSD_CHUNK == 0 else t
    assert L % TAIL == 0
    nc = t // L
    pad_h = lambda v: jnp.pad(v.reshape(1, -1), ((0, 0), (0, LANES - H_D)))
    consts = [P['conv_w'], P['conv_b'].reshape(1, -1), pad_h(P['dt_bias']), pad_h(P['a_log']),
              jnp.repeat(P['d'], P_D).reshape(1, -1), P['g_norm'].reshape(1, -1)]
    tri_np = np.tril(np.ones((L, L), np.float32))
    tri = jnp.asarray(tri_np, BF16 if L >= 16 else F32)
    zc, xc, dc = ODD_Z_OFF // D_INNER, ODD_XBC_OFF // CONV_DIM, ODD_DT_OFF // LANES
    assert ODD_Z_OFF % D_INNER == 0 and ODD_XBC_OFF % CONV_DIM == 0 and ODD_DT_OFF % LANES == 0
    y, hf = pl.pallas_call(
        functools.partial(_ssd_body, L=L),
        grid=(b, nc),
        in_specs=[pl.BlockSpec((L, D_INNER), lambda bi, c: (bi * nc + c, zc)),
                  pl.BlockSpec((L, CONV_DIM), lambda bi, c: (bi * nc + c, xc)),
                  pl.BlockSpec((L, LANES), lambda bi, c: (bi * nc + c, dc)),
                  pl.BlockSpec((1, CONV_W - 1, CONV_DIM), lambda bi, c: (bi, 0, 0)),
                  pl.BlockSpec((1, H_D, P_D, N_D), lambda bi, c: (bi, 0, 0, 0))]
        + [pl.BlockSpec(cst.shape, lambda bi, c: (0, 0)) for cst in consts]
        + [pl.BlockSpec(tri.shape, lambda bi, c: (0, 0))],
        out_specs=[pl.BlockSpec((L, D_INNER), lambda bi, c: (bi * nc + c, 0)),
                   pl.BlockSpec((1, H_D, P_D, N_D), lambda bi, c: (bi, 0, 0, 0))],
        out_shape=[jax.ShapeDtypeStruct((b * t, D_INNER), BF16), jax.ShapeDtypeStruct((b, H_D, P_D, N_D), F32)],
        scratch_shapes=[pltpu.VMEM((H_D, P_D, N_D), F32), pltpu.VMEM((TAIL, CONV_DIM), F32),
                        pltpu.VMEM((L, D_INNER), F32)],
        compiler_params=_cp("parallel", "arbitrary"),
        name="ssd_mix",
    )(proj, proj, proj, conv_buf, state0, *consts, tri)
    return y, hf


SC_L = 16
SC_TB = 16
GELU_C = math.sqrt(2.0 / math.pi)


def _sc_gelu(a):
    y = GELU_C * (a + 0.044715 * (a * a * a))
    return 0.5 * a * (2.0 - 2.0 / (jnp.exp(2.0 * y) + 1.0))


def peer_experts(hn, idx, gate, u_tab, v_tab, x):
    n = hn.shape[0]
    info = plsc.get_sparse_core_info()
    nc, ns = info.num_cores, info.num_subcores
    assert info.num_lanes == SC_L and PEER_TOPK == SC_L
    nw = nc * ns
    per_w = n // nw
    assert n % (nw * SC_TB) == 0
    n_blk = per_w // SC_TB
    n_chunk = SC_TB * PEER_HEADS
    n_vec = D_MODEL // SC_L
    mesh = plsc.VectorSubcoreMesh(core_axis_name="c", subcore_axis_name="s")

    @functools.partial(
        pl.kernel, mesh=mesh, out_type=jax.ShapeDtypeStruct((n, D_MODEL), F32),
        scratch_types=[pltpu.VMEM((SC_TB, D_MODEL), F32), pltpu.VMEM((SC_TB, D_MODEL), F32),
                       pltpu.VMEM((SC_TB, PEER_HEADS, SC_L), I32), pltpu.VMEM((SC_TB, PEER_HEADS * SC_L), F32),
                       pltpu.VMEM((2, SC_L, D_MODEL), F32), pltpu.VMEM((2, SC_L, D_MODEL), F32),
                       pltpu.SemaphoreType.DMA((2,)), pltpu.SemaphoreType.DMA((2,))],
        compiler_params=pltpu.CompilerParams(needs_layout_passes=False),
        name="peer_experts")
    def run(hn_hbm, idx_hbm, gate_hbm, u_hbm, v_hbm, x_hbm, out_hbm, hnv, outv, idxv, gv, ubuf, vbuf, usem, vsem):
        wid = lax.axis_index("s") * nc + lax.axis_index("c")
        lanes = lax.iota(I32, SC_L)

        def copies(q, slot):
            tl, hd = q // PEER_HEADS, q % PEER_HEADS
            rows = idxv.at[tl, hd]
            return (pltpu.make_async_copy(u_hbm.at[rows], ubuf.at[slot], usem.at[slot]),
                    pltpu.make_async_copy(v_hbm.at[rows], vbuf.at[slot], vsem.at[slot]))

        def start(q, slot):
            for cp in copies(q, slot):
                cp.start()

        def compute(q, slot):
            tl, hd = q // PEER_HEADS, q % PEER_HEADS
            cu, cv = copies(q, slot)
            cu.wait()

            def ubody(c, accs):
                off = pl.multiple_of(c * SC_L, SC_L)
                xc = hnv[tl, pl.ds(off, SC_L)]
                return tuple(accs[j] + ubuf[slot, j, pl.ds(off, SC_L)] * xc for j in range(SC_L))

            zero = jnp.zeros((SC_L,), F32)
            accs = lax.fori_loop(0, n_vec, ubody, (zero,) * SC_L)
            acts = zero
            for j in range(SC_L):
                acts = jnp.where(lanes == j, jnp.sum(accs[j]), acts)
            coef = gv[tl, pl.ds(pl.multiple_of(hd * SC_L, SC_L), SC_L)] * _sc_gelu(acts)
            cj = [jnp.sum(jnp.where(lanes == j, coef, 0.0)) for j in range(SC_L)]
            cv.wait()

            def vbody(c, carry):
                off = pl.multiple_of(c * SC_L, SC_L)
                o = outv[tl, pl.ds(off, SC_L)]
                for j in range(SC_L):
                    o = o + cj[j] * vbuf[slot, j, pl.ds(off, SC_L)]
                outv[tl, pl.ds(off, SC_L)] = o
                return carry

            lax.fori_loop(0, n_vec, vbody, 0)

        def block(bi, carry):
            t0 = pl.multiple_of(wid * per_w + bi * SC_TB, SC_TB)
            pltpu.sync_copy(hn_hbm.at[pl.ds(t0, SC_TB)], hnv)
            pltpu.sync_copy(x_hbm.at[pl.ds(t0, SC_TB)], outv)
            pltpu.sync_copy(idx_hbm.at[pl.ds(t0, SC_TB)], idxv)
            pltpu.sync_copy(gate_hbm.at[pl.ds(t0, SC_TB)], gv)
            start(0, 0)

            def pair(p, c2):
                q = 2 * p
                start(q + 1, 1)
                compute(q, 0)

                @pl.when(q + 2 < n_chunk)
                def _():
                    start(q + 2, 0)

                compute(q + 1, 1)
                return c2

            lax.fori_loop(0, n_chunk // 2, pair, 0)
            pltpu.sync_copy(outv, out_hbm.at[pl.ds(t0, SC_TB)])
            return carry

        lax.fori_loop(0, n_blk, block, 0)

    return run(hn, idx.reshape(n, PEER_HEADS, SC_L), gate, u_tab, v_tab, x)


def _even_w_in(w):
    out = jnp.zeros((D_MODEL, EVEN_W), F32)
    out = out.at[:, :EVEN_KR_OFF].set(w[:, :D_CQ + D_C])
    out = out.at[:, EVEN_KR_OFF + DN_A:EVEN_KR_OFF + DN_A + DR_A].set(w[:, D_CQ + D_C:D_CQ + D_C + DR_A])
    out = out.at[:, EVEN_PR_OFF:].set(w[:, D_CQ + D_C + DR_A:])
    return out.astype(BF16)


def _rwkv_prompt(x, b, t, g_mix, P, W):
    proj = norm_matmul(x, g_mix, W['in'], after=W['after'])
    return rwkv_mix(proj, jnp.zeros((b, RW_WIDTH), F32), jnp.zeros((b, H_B, HS_B, HS_B), F32),
                    b, t, P, nb=min(b, 4), tc=64, tm=256)


def _even_prompt(x, b, t, pos, g_mix, P, W, rw=None):
    proj = norm_matmul(x, g_mix, W['in'], after=W['after'])
    qc, kc, vc, ckv, kr = mla_prep(proj, pos, P)
    o_a = mla_prompt_attention(qc, kc, vc, b, t)
    if rw is None:
        rw = rwkv_mix(proj, jnp.zeros((b, RW_WIDTH), F32), jnp.zeros((b, H_B, HS_B, HS_B), F32),
                      b, t, P, nb=min(b, 4), tc=64, tm=256)
    o_b, sh, wkv = rw
    x = matmul_res([o_a, o_b], W['out'], x)
    return x, lambda: (ckv.reshape(b, t, D_C), kr[:, DN_A:DN_A + DR_A].reshape(b, t, DR_A), sh, wkv)


def _even_sample(x, b, t, pos, page_table, cache_ckv, cache_kr, rw_shift, rw_wkv, g_mix, P, W):
    proj = norm_matmul(x, g_mix, W['in'], after=W['after'])
    qc, _, _, ckv, kr = mla_prep(proj, pos, P)
    qc = qc.reshape(b, t, H_A, HG)
    ckv = ckv.reshape(b, t, D_C)
    kr = kr[:, DN_A:DN_A + DR_A].reshape(b, t, DR_A)
    o_a = mla_sample_attention(page_table, cache_ckv, cache_kr, ckv, kr, qc[..., :DN_A], qc[..., DN_A:DN_A + DR_A], P)
    o_b, sh, wkv = rwkv_mix(proj, rw_shift, rw_wkv, b, t, P, nb=8, tc=t, tm=256)
    x = matmul_res([o_a, o_b], W['out'], x)
    return x, lambda: (ckv, kr, sh, wkv)


def _odd_states(proj, b, t):
    p3 = proj.reshape(b, t, ODD_W)
    if t % LANES == 0:
        k_new, v_new = sb_kv_state(proj, b, t)
    else:
        k_new, v_new = (p3[:, :, c * HC:(c + 1) * HC].reshape(b, t, H_C, HD_C) for c in (1, 2))
    return k_new, v_new, p3[:, t - (CONV_W - 1):, ODD_XBC_OFF:ODD_XBC_OFF + CONV_DIM]


def _odd_prompt(x, b, t, g_mix, P, W):
    proj = norm_matmul(x, g_mix, W['in'], after=W['after'])
    o_c = sb_prompt_attention(proj, b, t)
    y, ssm = ssd_mix(proj, jnp.zeros((b, CONV_W - 1, CONV_DIM), F32), jnp.zeros((b, H_D, P_D, N_D), F32), b, t, P)
    x = matmul_res([o_c, y], W['out'], x)
    return x, lambda: _odd_states(proj, b, t) + (ssm,)


def _odd_sample(x, b, t, page_table, cache_k, cache_v, conv_state, ssm_state, g_mix, P, W):
    proj = norm_matmul(x, g_mix, W['in'], after=W['after'])
    o_c = sb_sample_attention(page_table, cache_k, cache_v, proj, b, t)
    y, ssm = ssd_mix(proj, conv_state, ssm_state, b, t, P)
    x = matmul_res([o_c, y], W['out'], x)
    return x, lambda: _odd_states(proj, b, t) + (ssm,)


PROMPT_SPLITS = 4


def kernel(x_prompt, x_sample, mem_prompt, page_table, cache_mla_ckv, cache_mla_krope, state_rwkv_shift, state_rwkv_wkv, cache_sb_k, cache_sb_v, state_ssm_conv, state_ssm, cache_mem_k, cache_mem_v, norm_mix, norm_mem, norm_ffn, w_in_even, w_out_even, mla_g_cq, mla_w_uq, mla_g_ckv, mla_w_uk, mla_w_uv, mla_g_qn, mla_g_kn, mla_g_qr, mla_g_kr, rw_mu, rw_w0, rw_w_up, rw_a0, rw_a_up, rw_g_up, rw_k_k, rw_k_a, rw_r_k, rw_ln_g, rw_ln_b, w_in_odd, w_out_odd, ssm_conv_w, ssm_conv_b, ssm_dt_bias, ssm_a_log, ssm_d, ssm_g_norm, mem_g_src, mem_w_q, mem_w_kv, mem_g_q, mem_g_k, mem_w_o, peer_w_q, peer_sub_keys, peer_u, peer_v):
    bp, tp = x_prompt.shape[:2]
    bs, ts = x_sample.shape[:2]
    depth = norm_mix.shape[0]
    nsp = PROMPT_SPLITS if bp % PROMPT_SPLITS == 0 else 1
    bh = bp // nsp
    part = lambda a, k: a[k * bh:(k + 1) * bh]
    xps = [part(x_prompt, k).reshape(bh * tp, D_MODEL) for k in range(nsp)]
    xs = x_sample.reshape(bs * ts, D_MODEL)
    pos_p = jnp.tile(jnp.arange(tp, dtype=I32), bh)
    pos_s = jnp.tile(PAST_LEN + jnp.arange(ts, dtype=I32), bs)
    def layer_context(layer):
        i = layer // 2
        even = layer % 2 == 0
        if even:
            P = dict(w_in=w_in_even[i], w_out=w_out_even[i], g_cq=mla_g_cq[i], w_uq=mla_w_uq[i],
                     g_ckv=mla_g_ckv[i], w_uk=mla_w_uk[i], w_uv=mla_w_uv[i], g_qn=mla_g_qn[i],
                     g_kn=mla_g_kn[i], g_qr=mla_g_qr[i], g_kr=mla_g_kr[i],
                     mu=rw_mu[i], w0=rw_w0[i], w_up=rw_w_up[i], a0=rw_a0[i], a_up=rw_a_up[i],
                     g_up=rw_g_up[i], k_k=rw_k_k[i], k_a=rw_k_a[i], r_k=rw_r_k[i],
                     ln_g=rw_ln_g[i], ln_b=rw_ln_b[i])
            w_out = P['w_out'].astype(BF16)
            W = {'in': _even_w_in(P['w_in']), 'out': [w_out[:H_A * DV_A], w_out[H_A * DV_A:]]}
        else:
            P = dict(w_in=w_in_odd[i], w_out=w_out_odd[i], conv_w=ssm_conv_w[i], conv_b=ssm_conv_b[i],
                     dt_bias=ssm_dt_bias[i], a_log=ssm_a_log[i], d=ssm_d[i], g_norm=ssm_g_norm[i])
            w_out = P['w_out'].astype(BF16)
            W = {'in': jnp.pad(P['w_in'], ((0, 0), (0, ODD_W - P['w_in'].shape[1]))).astype(BF16),
                 'out': [w_out[:HC], w_out[HC:]]}
        mk, mv = memory_kv(mem_prompt.reshape(bp * N_MEM, D_MODEL), mem_g_src[layer], mem_w_kv[layer], mem_g_k[layer])
        return dict(i=i, even=even, P=P, W=W, mk=mk.reshape(bp, N_MEM, MEM_W), mv=mv.reshape(bp, N_MEM, MEM_W))

    contexts, p_state, s_state, rw_pair = {}, {}, {}, {}
    state = {'xs': xs, 'tc_done': None}

    def run(group, layer):
        if layer not in contexts:
            contexts[layer] = layer_context(layer)
        c = contexts[layer]
        i, even, P = c['i'], c['even'], c['P']
        W = dict(c['W'], after=state['tc_done'])
        g_mix = norm_mix[layer]
        if group < nsp:
            if even:
                rw = None
                if nsp % 2 == 0 and layer == 0:
                    if group % 2 == 0:
                        pair = jnp.concatenate([xps[group], xps[group + 1]], axis=0)
                        rw_pair[layer, group // 2] = _rwkv_prompt(pair, 2 * bh, tp, g_mix, P, W)
                    o_b, sh, wkv = rw_pair[layer, group // 2]
                    h = group % 2
                    rw = (o_b[h * bh * tp:(h + 1) * bh * tp], sh[h * bh:(h + 1) * bh], wkv[h * bh:(h + 1) * bh])
                x, p_state[layer, group] = _even_prompt(xps[group], bh, tp, pos_p, g_mix, P, W, rw)
            else:
                x, p_state[layer, group] = _odd_prompt(xps[group], bh, tp, g_mix, P, W)
            t, mk, mv, tq = tp, part(c['mk'], group), part(c['mv'], group), 256
        else:
            if even:
                x, s_state[layer] = _even_sample(state['xs'], bs, ts, pos_s, page_table, cache_mla_ckv[i],
                                                 cache_mla_krope[i], state_rwkv_shift[i], state_rwkv_wkv[i], g_mix, P, W)
            else:
                x, s_state[layer] = _odd_sample(state['xs'], bs, ts, page_table, cache_sb_k[i], cache_sb_v[i],
                                                state_ssm_conv[i], state_ssm[i], g_mix, P, W)
            t, tq = ts, ts
            mk, mv = (cache_mem_k[layer].reshape(bs, N_MEM, MEM_W), cache_mem_v[layer].reshape(bs, N_MEM, MEM_W))
        x = memory_attend(x, t, norm_mem[layer], mem_w_q[layer], mem_g_q[layer], mk, mv, mem_w_o[layer], tq=tq)
        hn, idx, gate = peer_route(x, norm_ffn[layer], peer_w_q[layer], peer_sub_keys[layer])
        x = peer_experts(hn, idx, gate, peer_u[layer], peer_v[layer], x)
        state['tc_done'] = idx
        if group < nsp:
            xps[group] = x
        else:
            state['xs'] = x

    for layer in range(depth):
        for group in range(nsp + 1):
            run(group, layer)
    xs = state['xs']

    def join(layer):
        parts = [p_state[layer, k]() for k in range(nsp)]
        return tuple(jnp.concatenate([p[j] for p in parts], axis=0) for j in range(len(parts[0])))

    even_p = [join(layer) for layer in range(0, depth, 2)]
    odd_p = [join(layer) for layer in range(1, depth, 2)]
    even_s = [s_state[layer]() for layer in range(0, depth, 2)]
    odd_s = [s_state[layer]() for layer in range(1, depth, 2)]
    mem_k = [contexts[layer]['mk'].reshape(bp, N_MEM, MEM_HEADS, MEM_HD) for layer in range(depth)]
    mem_v = [contexts[layer]['mv'].reshape(bp, N_MEM, MEM_HEADS, MEM_HD) for layer in range(depth)]
    stack = lambda groups, k: jnp.stack([g[k] for g in groups])
    xp = jnp.concatenate(xps, axis=0)
    return (xp.reshape(bp, tp, D_MODEL), xs.reshape(bs, ts, D_MODEL),
            stack(even_p, 0), stack(even_p, 1), stack(even_p, 2), stack(even_p, 3),
            stack(odd_p, 0), stack(odd_p, 1), stack(odd_p, 2), stack(odd_p, 3),
            jnp.stack(mem_k), jnp.stack(mem_v),
            stack(even_s, 0), stack(even_s, 1), stack(even_s, 2), stack(even_s, 3),
            stack(odd_s, 0), stack(odd_s, 1), stack(odd_s, 2), stack(odd_s, 3))
```
